```python
import math
import jax
import jax.numpy as jnp
from jax import lax
import numpy as np

D_MODEL = 1024
BATCH = 16
SEQ = 2048
DEPTH = 4

CTX_LEN = 256
GRID_W = 64
CONV_W = 4
EPS = 1e-6

LRU_WIDTH = D_MODEL
LRU_BLOCKS = 8
LRU_BW = LRU_WIDTH // LRU_BLOCKS
LRU_C = 8.0

GLA_HEADS = 4
GLA_DK = D_MODEL // 2
GLA_DV = D_MODEL
GLA_DKH = GLA_DK // GLA_HEADS
GLA_DVH = GLA_DV // GLA_HEADS
GLA_RANK = 16
GLA_TAU = 16.0
GLA_CHUNK = 64

SSD_INNER = 2 * D_MODEL
SSD_HEADDIM = 64
SSD_HEADS = SSD_INNER // SSD_HEADDIM
SSD_STATE = 128
SSD_GROUPS = 4
SSD_HPG = SSD_HEADS // SSD_GROUPS
SSD_XBC = SSD_INNER + 2 * SSD_GROUPS * SSD_STATE
SSD_CHUNK = 64

N_BRANCH = 3
IN_WIDTHS = (LRU_WIDTH, LRU_WIDTH, GLA_DK, GLA_DK, GLA_DV, GLA_DV, 2 * GLA_RANK,
             SSD_INNER, SSD_XBC, 2 * SSD_HEADS, N_BRANCH * D_MODEL)
IN_TOTAL = sum(IN_WIDTHS)

kernel_name = 'hybrid_lru_gla_ssd_prefix_dit'


def _split_cols(u):
    parts = []
    start = 0
    for w in IN_WIDTHS:
        parts.append(u[..., start:start + w])
        start += w
    return parts


def _rmsnorm(x, g):
    xf = x.astype(jnp.float32)
    y = xf * lax.rsqrt(jnp.mean(xf * xf, axis=-1, keepdims=True) + EPS)
    return (y * g.astype(jnp.float32)).astype(x.dtype)


def _to_col_major(h, rows):
    b, length, dm = h.shape
    return h.reshape(b, rows, GRID_W, dm).transpose(0, 2, 1, 3).reshape(b, length, dm)


def _from_col_major(h, rows):
    b, length, dm = h.shape
    return h.reshape(b, GRID_W, rows, dm).transpose(0, 2, 1, 3).reshape(b, length, dm)


def _dwconv(u, w, b, line_len):
    bn, length, ch = u.shape
    ul = u.reshape(bn, length // line_len, line_len, ch)
    left = (CONV_W - 1) // 2
    up = jnp.pad(ul, ((0, 0), (0, 0), (left, CONV_W - 1 - left), (0, 0)))
    out = b + w[0] * up[:, :, 0:line_len]
    for k in range(1, CONV_W):
        out = out + w[k] * up[:, :, k:k + line_len]
    return out.reshape(bn, length, ch)


def _flip(ts):
    return tuple(jnp.flip(t, axis=1) for t in ts)


def _bidir(step, ctx_f, lat_f, ctx_b, lat_b, s0, need_ctx):
    yc_f, sc_f = step(*ctx_f, s0)
    yl_f, _ = step(*lat_f, sc_f)
    yc_b, sc_b = step(*_flip(ctx_b), s0)
    yl_b, _ = step(*_flip(lat_b), sc_b)
    y_lat = yl_f + jnp.flip(yl_b, axis=1)
    y_ctx = yc_f + jnp.flip(yc_b, axis=1) if need_ctx else None
    return y_ctx, y_lat


def _to_chunks(t, chunk):
    b, length = t.shape[:2]
    return jnp.moveaxis(t.reshape((b, length // chunk, chunk) + t.shape[2:]), 1, 0)


def _from_chunks(t):
    n, b, chunk = t.shape[:3]
    return jnp.moveaxis(t, 0, 1).reshape((b, n * chunk) + t.shape[3:])


def _lru_gates(xa, wr, br, wi, bi, lam):
    bn, length, width = xa.shape
    xb = xa.reshape(bn, length, LRU_BLOCKS, LRU_BW)
    r = jax.nn.sigmoid(jnp.einsum('blnk,nkj->blnj', xb, wr).reshape(bn, length, width) + br)
    i = jax.nn.sigmoid(jnp.einsum('blnk,nkj->blnj', xb, wi).reshape(bn, length, width) + bi)
    log_a = (-LRU_C * jax.nn.softplus(-lam) * r).astype(jnp.float32)
    a = jnp.exp(log_a)
    b = jnp.sqrt(-jnp.expm1(2.0 * log_a)) * (i * xa).astype(jnp.float32)
    return a, b


def _lru_scan(a, b, h0):
    def combine(lhs, rhs):
        return lhs[0] * rhs[0], rhs[0] * lhs[1] + rhs[1]
    a_cum, h = lax.associative_scan(combine, (a, b), axis=1)
    h = h + a_cum * h0[:, None]
    return h, h[:, -1]


def _gla_chunked(q, k, v, logg, s0):
    mask = jnp.tril(jnp.ones((GLA_CHUNK, GLA_CHUNK), dtype=bool))

    def body(state, xs):
        qc, kc, vc, gc = xs
        bc = jnp.cumsum(gc, axis=1)
        btot = bc[:, -1]
        qe = qc * jnp.exp(bc)
        ke = kc * jnp.exp(-bc)
        att = jnp.where(mask, jnp.einsum('bthk,bshk->bhts', qe, ke), 0.0)
        o = jnp.einsum('bhts,bshv->bthv', att, vc) + jnp.einsum('bthk,bhkv->bthv', qe, state)
        kd = kc * jnp.exp(btot[:, None] - bc)
        state = state * jnp.exp(btot)[..., None] + jnp.einsum('bshk,bshv->bhkv', kd, vc)
        return state, o

    xs = tuple(_to_chunks(t, GLA_CHUNK) for t in (q, k, v, logg))
    state, o = lax.scan(body, s0, xs)
    return _from_chunks(o), state


def _gla_inputs(s, p):
    bn, length = s[2].shape[:2]
    q = (s[2] * GLA_DKH ** -0.5).reshape(bn, length, GLA_HEADS, GLA_DKH)
    k = s[3].reshape(bn, length, GLA_HEADS, GLA_DKH)
    v = s[4].reshape(bn, length, GLA_HEADS, GLA_DVH)
    low = s[6].reshape(bn, length, 2, GLA_RANK)
    dirs = []
    for d in range(2):
        z = jnp.einsum('blr,rk->blk', low[:, :, d], p['gla_alpha_up'][d]) + p['gla_alpha_b'][d]
        logg = jax.nn.log_sigmoid(z.astype(jnp.float32)) / GLA_TAU
        dirs.append((q, k, v, logg.reshape(bn, length, GLA_HEADS, GLA_DKH)))
    return dirs[0], dirs[1]


def _ssd_chunked(xdt, adt, bm, cm, s0):
    idx = jnp.arange(SSD_CHUNK)
    mask = (idx[:, None] >= idx[None, :])[None, :, :, None, None]

    def body(state, xs):
        xc, ac, bc, cc = xs
        cum = jnp.cumsum(ac, axis=1)
        seg = cum[:, :, None] - cum[:, None, :]
        lmat = jnp.exp(jnp.where(mask, seg, -jnp.inf))
        cb = jnp.einsum('btgn,bsgn->btsg', cc, bc)
        y = jnp.einsum('btsgh,bsghp->btghp', cb[..., None] * lmat, xc)
        y = y + jnp.einsum('btgn,bghnp->btghp', cc, state) * jnp.exp(cum)[..., None]
        dec = jnp.exp(cum[:, -1:] - cum)
        state = state * jnp.exp(cum[:, -1])[..., None, None] + jnp.einsum('bsgn,bsghp->bghnp', bc, xc * dec[..., None])
        return state, y

    xs = tuple(_to_chunks(t, SSD_CHUNK) for t in (xdt, adt, bm, cm))
    state, y = lax.scan(body, s0, xs)
    return _from_chunks(y), state


def _ssd_inputs(s, line_len, p):
    bn, length = s[8].shape[:2]
    xbc = jax.nn.silu(_dwconv(s[8], p['conv_c_w'], p['conv_c_b'], line_len))
    gn = SSD_GROUPS * SSD_STATE
    xs = xbc[..., :SSD_INNER].reshape(bn, length, SSD_GROUPS, SSD_HPG, SSD_HEADDIM)
    bm = xbc[..., SSD_INNER:SSD_INNER + gn].reshape(bn, length, SSD_GROUPS, SSD_STATE)
    cm = xbc[..., SSD_INNER + gn:].reshape(bn, length, SSD_GROUPS, SSD_STATE)
    dt_raw = s[9].reshape(bn, length, 2, SSD_HEADS).astype(jnp.float32)
    dirs = []
    for d in range(2):
        dt = jax.nn.softplus(dt_raw[:, :, d] + p['ssd_dt_bias'][d]).reshape(bn, length, SSD_GROUPS, SSD_HPG)
        a = -jnp.exp(p['ssd_a_log'][d].astype(jnp.float32)).reshape(SSD_GROUPS, SSD_HPG)
        dirs.append((xs * dt[..., None], dt * a, bm, cm))
    return xs, dirs[0], dirs[1]


def _finish(s, ya, yb, yc, xs, p):
    bn, length = s[1].shape[:2]
    pa = (ya * jax.nn.silu(s[1])) @ p['w_pa']
    ob = _rmsnorm(yb, p['gla_norm_g']).reshape(bn, length, GLA_DV) * jax.nn.silu(s[5])
    pb = ob @ p['w_pb']
    yc = (yc + p['ssd_d'].reshape(SSD_GROUPS, SSD_HPG, 1) * xs).reshape(bn, length, SSD_INNER)
    gsz = SSD_INNER // SSD_GROUPS
    oc = _rmsnorm((yc * jax.nn.silu(s[7])).reshape(bn, length, SSD_GROUPS, gsz),
                  p['ssd_norm_g'].reshape(SSD_GROUPS, gsz))
    pc = oc.reshape(bn, length, SSD_INNER) @ p['w_pc']
    g = jax.nn.sigmoid(s[10]).reshape(bn, length, N_BRANCH, D_MODEL)
    merged = g[:, :, 0] * pa + g[:, :, 1] * pb + g[:, :, 2] * pc
    return merged @ p['w_out']


def _mixer(h_ctx, h_lat, line_len, need_ctx, p):
    f32 = jnp.float32
    s_c = _split_cols(h_ctx @ p['w_in'])
    s_l = _split_cols(h_lat @ p['w_in'])
    bn = h_lat.shape[0]
    lc = h_ctx.shape[1]

    def lru_in(s, line):
        xa = _dwconv(s[0], p['conv_a_w'], p['conv_a_b'], line)
        return [_lru_gates(xa, p['lru_wr'][d], p['lru_br'][d], p['lru_wi'][d], p['lru_bi'][d], p['lru_lam'][d])
                for d in range(2)]
    a_c = lru_in(s_c, lc)
    a_l = lru_in(s_l, line_len)
    ya_c, ya_l = _bidir(_lru_scan, a_c[0], a_l[0], a_c[1], a_l[1],
                        jnp.zeros((bn, LRU_WIDTH), f32), need_ctx)

    bf_c, bb_c = _gla_inputs(s_c, p)
    bf_l, bb_l = _gla_inputs(s_l, p)
    yb_c, yb_l = _bidir(_gla_chunked, bf_c, bf_l, bb_c, bb_l,
                        jnp.zeros((bn, GLA_HEADS, GLA_DKH, GLA_DVH), f32), need_ctx)

    xs_c, cf_c, cb_c = _ssd_inputs(s_c, lc, p)
    xs_l, cf_l, cb_l = _ssd_inputs(s_l, line_len, p)
    yc_c, yc_l = _bidir(_ssd_chunked, cf_c, cf_l, cb_c, cb_l,
                        jnp.zeros((bn, SSD_GROUPS, SSD_HPG, SSD_STATE, SSD_HEADDIM), f32), need_ctx)

    out_lat = _finish(s_l, ya_l, yb_l, yc_l, xs_l, p)
    out_ctx = _finish(s_c, ya_c, yb_c, yc_c, xs_c, p) if need_ctx else None
    return out_ctx, out_lat


def setup_inputs(seed: int = 0) -> dict:
    key = jax.random.key(seed)
    ks = jax.random.split(key, 32)
    f32 = jnp.float32

    def nrm(k, shape, scale):
        return jax.random.normal(k, shape, f32) * scale

    L = DEPTH
    a0 = jax.random.uniform(ks[20], (L, 2, LRU_WIDTH), f32, 0.9, 0.999)
    p_a = a0 ** (1.0 / LRU_C)
    lru_lam = jnp.log(p_a) - jnp.log1p(-p_a)
    a_init = jax.random.uniform(ks[21], (L, 2, SSD_HEADS), f32, 1.0, 16.0)
    dt0 = jnp.exp(jax.random.uniform(ks[22], (L, 2, SSD_HEADS), f32, math.log(1e-3), math.log(1e-1)))
    return {
        'x': nrm(ks[0], (BATCH, SEQ, D_MODEL), 1.0),
        'c': nrm(ks[1], (BATCH, D_MODEL), 1.0),
        'ctx': nrm(ks[2], (BATCH, CTX_LEN, D_MODEL), 1.0),
        'c_ctx': nrm(ks[3], (D_MODEL,), 1.0),
        'ada_w': nrm(ks[4], (L, D_MODEL, 3 * D_MODEL), 0.5 * D_MODEL ** -0.5),
        'ada_b': nrm(ks[5], (L, 3 * D_MODEL), 0.01),
        'pre_g': 1.0 + nrm(ks[6], (L, D_MODEL), 0.05),
        'post_g': 1.0 + nrm(ks[7], (L, D_MODEL), 0.05),
        'w_in': nrm(ks[8], (L, D_MODEL, IN_TOTAL), D_MODEL ** -0.5),
        'conv_a_w': nrm(ks[9], (L, CONV_W, LRU_WIDTH), CONV_W ** -0.5),
        'conv_a_b': nrm(ks[10], (L, LRU_WIDTH), 0.01),
        'lru_wr': nrm(ks[11], (L, 2, LRU_BLOCKS, LRU_BW, LRU_BW), LRU_BW ** -0.5),
        'lru_br': nrm(ks[12], (L, 2, LRU_WIDTH), 0.01),
        'lru_wi': nrm(ks[13], (L, 2, LRU_BLOCKS, LRU_BW, LRU_BW), LRU_BW ** -0.5),
        'lru_bi': nrm(ks[14], (L, 2, LRU_WIDTH), 0.01),
        'lru_lam': lru_lam,
        'gla_alpha_up': nrm(ks[15], (L, 2, GLA_RANK, GLA_DK), GLA_RANK ** -0.5),
        'gla_alpha_b': nrm(ks[16], (L, 2, GLA_DK), 0.1),
        'gla_norm_g': 1.0 + nrm(ks[17], (L, GLA_DVH), 0.05),
        'conv_c_w': nrm(ks[18], (L, CONV_W, SSD_XBC), CONV_W ** -0.5),
        'conv_c_b': nrm(ks[19], (L, SSD_XBC), 0.01),
        'ssd_a_log': jnp.log(a_init),
        'ssd_dt_bias': dt0 + jnp.log(-jnp.expm1(-dt0)),
        'ssd_d': 1.0 + nrm(ks[23], (L, SSD_HEADS), 0.05),
        'ssd_norm_g': 1.0 + nrm(ks[24], (L, SSD_INNER), 0.05),
        'w_pa': nrm(ks[25], (L, LRU_WIDTH, D_MODEL), LRU_WIDTH ** -0.5),
        'w_pb': nrm(ks[26], (L, GLA_DV, D_MODEL), GLA_DV ** -0.5),
        'w_pc': nrm(ks[27], (L, SSD_INNER, D_MODEL), SSD_INNER ** -0.5),
        'w_out': nrm(ks[28], (L, D_MODEL, D_MODEL), D_MODEL ** -0.5),
    }


def reference(x, c, ctx, c_ctx, ada_w, ada_b, pre_g, post_g, w_in, conv_a_w, conv_a_b,
              lru_wr, lru_br, lru_wi, lru_bi, lru_lam, gla_alpha_up, gla_alpha_b, gla_norm_g,
              conv_c_w, conv_c_b, ssd_a_log, ssd_dt_bias, ssd_d, ssd_norm_g,
              w_pa, w_pb, w_pc, w_out):
    rows = x.shape[1] // GRID_W
    x_lat, x_ctx = x, ctx
    for l in range(DEPTH):
        p = {
            'w_in': w_in[l], 'conv_a_w': conv_a_w[l], 'conv_a_b': conv_a_b[l],
            'lru_wr': lru_wr[l], 'lru_br': lru_br[l], 'lru_wi': lru_wi[l], 'lru_bi': lru_bi[l],
            'lru_lam': lru_lam[l], 'gla_alpha_up': gla_alpha_up[l], 'gla_alpha_b': gla_alpha_b[l],
            'gla_norm_g': gla_norm_g[l], 'conv_c_w': conv_c_w[l], 'conv_c_b': conv_c_b[l],
            'ssd_a_log': ssd_a_log[l], 'ssd_dt_bias': ssd_dt_bias[l], 'ssd_d': ssd_d[l],
            'ssd_norm_g': ssd_norm_g[l], 'w_pa': w_pa[l], 'w_pb': w_pb[l], 'w_pc': w_pc[l],
            'w_out': w_out[l],
        }
        col_major = (l % 2 == 1)
        need_ctx = l < DEPTH - 1
        line_len = rows if col_major else GRID_W

        shift, scale, gate = jnp.split(jax.nn.silu(c) @ ada_w[l] + ada_b[l], 3, axis=-1)
        shift_c, scale_c, gate_c = jnp.split(jax.nn.silu(c_ctx) @ ada_w[l] + ada_b[l], 3, axis=-1)

        h_lat = _rmsnorm(x_lat, pre_g[l]) * (1.0 + scale[:, None]) + shift[:, None]
        h_ctx = _rmsnorm(x_ctx, pre_g[l]) * (1.0 + scale_c) + shift_c
        if col_major:
            h_lat = _to_col_major(h_lat, rows)

        o_ctx, o_lat = _mixer(h_ctx, h_lat, line_len, need_ctx, p)

        if col_major:
            o_lat = _from_col_major(o_lat, rows)
        x_lat = x_lat + gate[:, None] * _rmsnorm(o_lat, post_g[l])
        if need_ctx:
            x_ctx = x_ctx + gate_c * _rmsnorm(o_ctx, post_g[l])
    return x_lat
```

```python
import functools

import jax
import jax.numpy as jnp
import numpy as np
from jax import lax
from jax.experimental import pallas as pl
from jax.experimental.pallas import tpu as pltpu

F32 = jnp.float32
BF16 = jnp.bfloat16

D_MODEL = 1024
GRID_W = 64
CONV_W = 4
EPS = 1e-6
LRU_BW = 128
LRU_C = 8.0
GLA_HEADS = 4
GLA_DK = 512
GLA_DV = 1024
GLA_DKH = 128
GLA_DVH = 256
GLA_RANK = 16
GLA_TAU = 16.0
SSD_INNER = 2048
SSD_P = 64
SSD_HEADS = 32
SSD_N = 128
SSD_G = 4
SSD_GW = SSD_INNER // SSD_G
SSD_XBC = SSD_INNER + 2 * SSD_G * SSD_N
CHUNK = 64

TN = 256
C_XBC, C_MRG, C_Z, C_XA, C_V, C_LG, C_GG, C_Q, C_K, C_SM = (
    0, 3072, 6144, 8192, 9216, 10240, 11264, 12288, 12800, 13312)
N_TOT = C_SM + TN
SM_LOW = 0
SM_DT = 32

VMEM_LIMIT = 56 * 1024 * 1024


def _silu(x):
    return x * jax.nn.sigmoid(x)


def _softplus(x):
    return jnp.maximum(x, 0.0) + jnp.log1p(jnp.exp(-jnp.abs(x)))


def _dot(a, b):
    return jnp.dot(a, b, preferred_element_type=F32)


def _dot_exact(a, b):
    return jnp.dot(a, b, preferred_element_type=F32, precision=lax.Precision.HIGHEST)


def _dot_nt(a, b):
    return lax.dot_general(a, b, (((1,), (1,)), ((), ())), preferred_element_type=F32)


def _dot_tn(a, b):
    return lax.dot_general(a, b, (((0,), (0,)), ((), ())), preferred_element_type=F32)


def _adaln_kernel(c_ref, w_ref, b_ref, o_ref):
    c = c_ref[...]
    o_ref[0] = _dot(_silu(c).astype(BF16), w_ref[0]) + b_ref[0]


def _adaln(cc, ada_w, ada_b):
    depth, d, n3 = ada_w.shape
    rows = cc.shape[0]
    tn = 1024
    return pl.pallas_call(
        _adaln_kernel,
        grid=(depth, n3 // tn),
        in_specs=[
            pl.BlockSpec((rows, d), lambda l, n: (0, 0)),
            pl.BlockSpec((1, d, tn), lambda l, n: (l, 0, n)),
            pl.BlockSpec((1, 1, tn), lambda l, n: (l, 0, n)),
        ],
        out_specs=pl.BlockSpec((1, rows, tn), lambda l, n: (l, 0, n)),
        out_shape=jax.ShapeDtypeStruct((depth, rows, n3), F32),
        name="adaln",
    )(cc, ada_w.astype(BF16), ada_b.reshape(depth, 1, n3))


def _in_kernel(x_ref, ml_ref, mc_ref, pg_ref, w_ref, cw_ref, cb_ref, u_ref, h_s, *, lc, line):
    j = pl.program_id(1)
    tt = x_ref.shape[1]
    d = x_ref.shape[2]

    @pl.when(j == 0)
    def _():
        def norm_mod(x, mod):
            y = x * lax.rsqrt(jnp.mean(x * x, axis=-1, keepdims=True) + EPS) * pg_ref[...]
            return (y * (1.0 + mod[:, d:2 * d]) + mod[:, 0:d]).astype(BF16)
        h_s[0:lc, :] = norm_mod(x_ref[0, 0:lc, :], mc_ref[...])
        h_s[lc:tt, :] = norm_mod(x_ref[0, lc:tt, :], ml_ref[0])

    u_ref[0] = _dot(h_s[...], w_ref[...])

    def conv(act):
        a = u_ref[0]
        t = lax.broadcasted_iota(jnp.int32, (tt, 1), 0)
        is_ctx = t < lc
        pos = jnp.where(is_ctx, t, (t - lc) % line)
        ll = jnp.where(is_ctx, lc, line)
        cw = cw_ref[...]
        out = cb_ref[...] + cw[1:2, :] * a
        out = out + cw[0:1, :] * jnp.where(pos >= 1, pltpu.roll(a, 1, 0), 0.0)
        out = out + cw[2:3, :] * jnp.where(pos + 1 < ll, pltpu.roll(a, tt - 1, 0), 0.0)
        out = out + cw[3:4, :] * jnp.where(pos + 2 < ll, pltpu.roll(a, tt - 2, 0), 0.0)
        u_ref[0] = act(out)

    @pl.when(j < SSD_XBC // TN)
    def _():
        conv(_silu)

    @pl.when((j >= C_XA // TN) & (j < (C_XA + D_MODEL) // TN))
    def _():
        conv(lambda v: v)


def _in_proj(xall, mod_lat, mod_ctx, pre_g, w_in_p, cw, cb, *, lc, line):
    b, tt, d = xall.shape
    return pl.pallas_call(
        functools.partial(_in_kernel, lc=lc, line=line),
        grid=(b, N_TOT // TN),
        in_specs=[
            pl.BlockSpec((1, tt, d), lambda i, j: (i, 0, 0)),
            pl.BlockSpec((1, 1, 3 * d), lambda i, j: (i, 0, 0)),
            pl.BlockSpec((1, 3 * d), lambda i, j: (0, 0)),
            pl.BlockSpec((1, d), lambda i, j: (0, 0)),
            pl.BlockSpec((d, TN), lambda i, j: (0, j)),
            pl.BlockSpec((CONV_W, TN), lambda i, j: (0, j)),
            pl.BlockSpec((1, TN), lambda i, j: (0, j)),
        ],
        out_specs=pl.BlockSpec((1, tt, TN), lambda i, j: (i, 0, j)),
        out_shape=jax.ShapeDtypeStruct((b, tt, N_TOT), F32),
        scratch_shapes=[pltpu.VMEM((tt, d), BF16)],
        compiler_params=pltpu.CompilerParams(
            dimension_semantics=("arbitrary", "arbitrary"), vmem_limit_bytes=VMEM_LIMIT),
        name="in_proj",
    )(xall, mod_lat, mod_ctx, pre_g, w_in_p, cw, cb)


LRU_TC = 64
LRU_WT = 512


def _lru_kernel(xf_ref, xb_ref, wr_ref, wi_ref, br_ref, bi_ref, lam_ref, yf_ref, yb_ref,
                a_s, b_s, o_s, h_s):
    i = pl.program_id(1)
    nb, tc, wt = xf_ref.shape

    @pl.when(i == 0)
    def _():
        h_s[...] = jnp.zeros_like(h_s)

    for dr, x_ref in enumerate((xf_ref, xb_ref)):
        x = x_ref[...].reshape(nb * tc, wt)
        sp = _softplus(-lam_ref[dr:dr + 1, :])
        for n in range(wt // LRU_BW):
            sl = slice(n * LRU_BW, (n + 1) * LRU_BW)
            xs = x[:, sl]
            xs16 = xs.astype(BF16)
            r = jax.nn.sigmoid(_dot(xs16, wr_ref[dr, n]) + br_ref[dr:dr + 1, sl])
            g = jax.nn.sigmoid(_dot(xs16, wi_ref[dr, n]) + bi_ref[dr:dr + 1, sl])
            log_a = (-LRU_C * sp[:, sl]) * r
            a = jnp.exp(log_a)
            a_s[dr, n] = a
            b_s[dr, n] = jnp.sqrt(-jnp.tanh(log_a) * (a * a + 1.0)) * (g * xs)

    nl = wt // LRU_BW
    hf = [h_s[0, n] for n in range(nl)]
    hb = [h_s[1, n] for n in range(nl)]
    for t in range(tc):
        rows_f = pl.ds(t, nb, stride=tc)
        rows_b = pl.ds(tc - 1 - t, nb, stride=tc)
        for n in range(nl):
            hf[n] = a_s[0, n, rows_f, :] * hf[n] + b_s[0, n, rows_f, :]
            hb[n] = a_s[1, n, rows_b, :] * hb[n] + b_s[1, n, rows_b, :]
            o_s[0, n, rows_f, :] = hf[n]
            o_s[1, n, rows_b, :] = hb[n]
    for n in range(nl):
        sl = slice(n * LRU_BW, (n + 1) * LRU_BW)
        h_s[0, n] = hf[n]
        h_s[1, n] = hb[n]
        yf_ref[:, :, sl] = o_s[0, n].reshape(nb, tc, LRU_BW)
        yb_ref[:, :, sl] = o_s[1, n].reshape(nb, tc, LRU_BW)


def _bwd_block(i, nctx, ntot):
    return jnp.where(i < nctx, nctx - 1 - i, ntot - 1 + nctx - i)


def _lru(u, wr, wi, br, bi, lam, *, lc):
    b, tt, _ = u.shape
    w = D_MODEL
    nt = tt // LRU_TC
    nctx = lc // LRU_TC
    c0 = C_XA // LRU_WT
    nblk = LRU_WT // LRU_BW
    y_spec_f = pl.BlockSpec((b, LRU_TC, LRU_WT), lambda j, i: (0, i, j))
    y_spec_b = pl.BlockSpec((b, LRU_TC, LRU_WT), lambda j, i: (0, _bwd_block(i, nctx, nt), j))
    return pl.pallas_call(
        _lru_kernel,
        grid=(w // LRU_WT, nt),
        in_specs=[
            pl.BlockSpec((b, LRU_TC, LRU_WT), lambda j, i: (0, i, c0 + j)),
            pl.BlockSpec((b, LRU_TC, LRU_WT), lambda j, i: (0, _bwd_block(i, nctx, nt), c0 + j)),
            pl.BlockSpec((2, nblk, LRU_BW, LRU_BW), lambda j, i: (0, j, 0, 0)),
            pl.BlockSpec((2, nblk, LRU_BW, LRU_BW), lambda j, i: (0, j, 0, 0)),
            pl.BlockSpec((2, LRU_WT), lambda j, i: (0, j)),
            pl.BlockSpec((2, LRU_WT), lambda j, i: (0, j)),
            pl.BlockSpec((2, LRU_WT), lambda j, i: (0, j)),
        ],
        out_specs=[y_spec_f, y_spec_b],
        out_shape=[jax.ShapeDtypeStruct((b, tt, w), F32)] * 2,
        scratch_shapes=[
            pltpu.VMEM((2, nblk, b * LRU_TC, LRU_BW), F32),
            pltpu.VMEM((2, nblk, b * LRU_TC, LRU_BW), F32),
            pltpu.VMEM((2, nblk, b * LRU_TC, LRU_BW), F32),
            pltpu.VMEM((2, nblk, b, LRU_BW), F32),
        ],
        compiler_params=pltpu.CompilerParams(
            dimension_semantics=("arbitrary", "arbitrary"), vmem_limit_bytes=VMEM_LIMIT),
        name="lru_scan",
    )(u, u, wr, wi, br, bi, lam)


SCAN_TB = 128


def _gla_chunk(dr, qk, v, sm16, wup_ref, gb_ref, tri, causal, gla_s, last):
    z = _dot(sm16, wup_ref[dr]) + gb_ref[dr:dr + 1, :]
    logg = -_softplus(-z) * (1.0 / GLA_TAU)
    bc = _dot_exact(tri, logg)
    btot = bc[last:last + 1, :]
    qe = (qk[:, 0:GLA_DK] * (GLA_DKH ** -0.5)) * jnp.exp(bc)
    k = qk[:, GLA_DK:2 * GLA_DK]
    ke = k * jnp.exp(-bc)
    kd = k * jnp.exp(btot - bc)
    etot = jnp.exp(btot)
    outs = []
    for h in range(GLA_HEADS):
        ks = slice(h * GLA_DKH, (h + 1) * GLA_DKH)
        vs = slice(h * GLA_DVH, (h + 1) * GLA_DVH)
        qe16 = qe[:, ks].astype(BF16)
        v16 = v[:, vs].astype(BF16)
        att = jnp.where(causal, _dot_nt(qe16, ke[:, ks].astype(BF16)), 0.0)
        st = gla_s[dr, h]
        outs.append(_dot(att.astype(BF16), v16) + _dot_nt(qe16, st.astype(BF16)))
        gla_s[dr, h] = st * etot[:, ks] + _dot_tn(v16, kd[:, ks].astype(BF16))
    return jnp.concatenate(outs, axis=1)


def _ssd_chunk(dr, xbc, sm, e_ref, dtb_ref, na_ref, tri, causal_x, diag_x, blk_mask, ssd_s, last):
    dt = _softplus(sm + dtb_ref[dr:dr + 1, :])
    adt = dt * na_ref[dr:dr + 1, :]
    cum = _dot_exact(tri, adt)
    clast = cum[last:last + 1, :]
    e = e_ref[dr]
    cum_x = _dot_exact(cum, e)
    stack = jnp.concatenate([dt, jnp.exp(cum), jnp.exp(clast - cum)], axis=0)
    stack_x = _dot_exact(stack, e)
    dt_x = stack_x[0:CHUNK]
    ecum_x = stack_x[CHUNK:2 * CHUNK]
    dec_x = stack_x[2 * CHUNK:3 * CHUNK]
    cum_row = _dot_exact(jnp.ones((CHUNK, CHUNK), F32), jnp.where(diag_x, cum_x, 0.0))
    lmat = jnp.exp(jnp.where(causal_x, cum_x - cum_row, -jnp.inf))
    xs = xbc[:, 0:SSD_INNER]
    xdt = xs * dt_x
    xdec16 = (xdt * dec_x).astype(BF16)
    xdt16 = xdt.astype(BF16)
    ys = []
    for g in range(SSD_G):
        gs = slice(g * SSD_GW, (g + 1) * SSD_GW)
        bm16 = xbc[:, SSD_INNER + g * SSD_N:SSD_INNER + (g + 1) * SSD_N].astype(BF16)
        cm16 = xbc[:, SSD_INNER + (SSD_G + g) * SSD_N:SSD_INNER + (SSD_G + g + 1) * SSD_N].astype(BF16)
        b_tiled = jnp.concatenate([bm16] * (SSD_GW // CHUNK), axis=0)
        m16 = (_dot_nt(cm16, b_tiled) * lmat[:, gs]).astype(BF16)
        parts = []
        for pr in range(SSD_GW // 128):
            ls = slice(g * SSD_GW + pr * 128, g * SSD_GW + (pr + 1) * 128)
            x2 = xdt16[:, ls]
            bd = jnp.where(blk_mask, jnp.concatenate([x2, x2], axis=0), jnp.zeros((), BF16))
            parts.append(_dot(m16[:, pr * 128:(pr + 1) * 128], bd))
        st = ssd_s[dr, g]
        ys.append(jnp.concatenate(parts, axis=1) + _dot(cm16, st.astype(BF16)) * ecum_x[:, gs])
        ssd_s[dr, g] = st * ecum_x[last:last + 1, gs] + _dot_tn(bm16, xdec16[:, gs])
    return jnp.concatenate(ys, axis=1)


def _scan_kernel(qkf_ref, vf_ref, xf_ref, sf_ref, qkb_ref, vb_ref, xb_ref, sb_ref,
                 wup_ref, gb_ref, e_ref, dtb_ref, na_ref,
                 ybf_ref, ycf_ref, ybb_ref, ycb_ref, gla_s, ssd_s):
    i = pl.program_id(1)

    @pl.when(i == 0)
    def _():
        gla_s[...] = jnp.zeros_like(gla_s)
        ssd_s[...] = jnp.zeros_like(ssd_s)

    tb = qkf_ref.shape[1]
    nch = tb // CHUNK
    row = lax.broadcasted_iota(jnp.int32, (CHUNK, CHUNK), 0)
    col = lax.broadcasted_iota(jnp.int32, (CHUNK, CHUNK), 1)
    row_x = lax.broadcasted_iota(jnp.int32, (CHUNK, SSD_INNER), 0)
    col_x = lax.broadcasted_iota(jnp.int32, (CHUNK, SSD_INNER), 1) & (CHUNK - 1)
    diag_x = row_x == col_x
    r2 = lax.broadcasted_iota(jnp.int32, (2 * CHUNK, 128), 0)
    c2 = lax.broadcasted_iota(jnp.int32, (2 * CHUNK, 128), 1)
    blk_mask = (r2 < CHUNK) == (c2 < CHUNK)

    dirs = (
        (0, qkf_ref, vf_ref, xf_ref, sf_ref, ybf_ref, ycf_ref),
        (1, qkb_ref, vb_ref, xb_ref, sb_ref, ybb_ref, ycb_ref),
    )
    for step in range(nch):
        for dr, qk_ref, v_ref, x_ref, s_ref, yb_ref, yc_ref in dirs:
            fwd = dr == 0
            c = step if fwd else nch - 1 - step
            rs = slice(c * CHUNK, (c + 1) * CHUNK)
            causal = (row >= col) if fwd else (row <= col)
            causal_x = (row_x >= col_x) if fwd else (row_x <= col_x)
            tri = causal.astype(F32)
            last = CHUNK - 1 if fwd else 0
            sm = s_ref[0, rs, :]
            yb_ref[0, rs, :] = _gla_chunk(dr, qk_ref[0, rs, :], v_ref[0, rs, :], sm.astype(BF16),
                                          wup_ref, gb_ref, tri, causal, gla_s, last)
            yc_ref[0, rs, :] = _ssd_chunk(dr, x_ref[0, rs, :], sm, e_ref, dtb_ref, na_ref, tri,
                                          causal_x, diag_x, blk_mask, ssd_s, last)


def _scan(u, wup, gb, e_x, dtb, na, *, lc):
    b, tt, _ = u.shape
    tb = SCAN_TB
    nt = tt // tb
    nctx = lc // tb

    def fmap(cb):
        return lambda bi, i: (bi, i, cb)

    def bmap(cb):
        return lambda bi, i: (bi, _bwd_block(i, nctx, nt), cb)

    def tok_specs(mk):
        return [
            pl.BlockSpec((1, tb, 2 * GLA_DK), mk(C_Q // (2 * GLA_DK))),
            pl.BlockSpec((1, tb, GLA_DV), mk(C_V // GLA_DV)),
            pl.BlockSpec((1, tb, SSD_XBC), mk(C_XBC // SSD_XBC)),
            pl.BlockSpec((1, tb, TN), mk(C_SM // TN)),
        ]

    def const_spec(a):
        nd = a.ndim
        return pl.BlockSpec(a.shape, lambda bi, i: (0,) * nd)

    consts = (wup, gb, e_x, dtb, na)
    return pl.pallas_call(
        _scan_kernel,
        grid=(b, nt),
        in_specs=tok_specs(fmap) + tok_specs(bmap) + [const_spec(a) for a in consts],
        out_specs=[
            pl.BlockSpec((1, tb, GLA_DV), fmap(0)),
            pl.BlockSpec((1, tb, SSD_INNER), fmap(0)),
            pl.BlockSpec((1, tb, GLA_DV), bmap(0)),
            pl.BlockSpec((1, tb, SSD_INNER), bmap(0)),
        ],
        out_shape=[
            jax.ShapeDtypeStruct((b, tt, GLA_DV), F32),
            jax.ShapeDtypeStruct((b, tt, SSD_INNER), F32),
            jax.ShapeDtypeStruct((b, tt, GLA_DV), F32),
            jax.ShapeDtypeStruct((b, tt, SSD_INNER), F32),
        ],
        scratch_shapes=[
            pltpu.VMEM((2, GLA_HEADS, GLA_DVH, GLA_DKH), F32),
            pltpu.VMEM((2, SSD_G, SSD_N, SSD_GW), F32),
        ],
        compiler_params=pltpu.CompilerParams(
            dimension_semantics=("arbitrary", "arbitrary"), vmem_limit_bytes=VMEM_LIMIT),
        name="gla_ssd_scan",
    )(u, u, u, u, u, u, u, u, *consts)


FIN_TM = 128


def _group_rms(x, width):
    parts = []
    for s in range(x.shape[1] // width):
        xs = x[:, s * width:(s + 1) * width]
        parts.append(xs * lax.rsqrt(jnp.mean(xs * xs, axis=-1, keepdims=True) + EPS))
    return jnp.concatenate(parts, axis=1)


def _fin_kernel(yaf_ref, yab_ref, ybf_ref, ybb_ref, ycf_ref, ycb_ref, lg_ref, gg_ref, z_ref, mg_ref,
                xs_ref, x_ref, ml_ref, mc_ref, gng_ref, sng_ref, sd_ref, pog_ref,
                wpa_ref, wpb_ref, wpc_ref, wo_ref, o_ref, *, nctx):
    i = pl.program_id(1)
    d = D_MODEL
    ya = yaf_ref[0] + yab_ref[0]
    pa = _dot((ya * _silu(lg_ref[0])).astype(BF16), wpa_ref[...])
    yb = ybf_ref[0] + ybb_ref[0]
    ob = (_group_rms(yb, GLA_DVH) * gng_ref[...]) * _silu(gg_ref[0])
    pb = _dot(ob.astype(BF16), wpb_ref[...])
    yc = (ycf_ref[0] + ycb_ref[0]) + sd_ref[...] * xs_ref[0]
    oc = _group_rms(yc * _silu(z_ref[0]), SSD_GW) * sng_ref[...]
    pc = _dot(oc.astype(BF16), wpc_ref[...])
    gates = jax.nn.sigmoid(mg_ref[0])
    merged = gates[:, 0:d] * pa + gates[:, d:2 * d] * pb + gates[:, 2 * d:3 * d] * pc
    out = _dot(merged.astype(BF16), wo_ref[...])
    gate = jnp.where(i < nctx, mc_ref[:, 2 * d:3 * d], ml_ref[0][:, 2 * d:3 * d])
    o_ref[0] = x_ref[0] + gate * (_group_rms(out, d) * pog_ref[...])


def _finish(ya_f, ya_b, yb_f, yb_b, yc_f, yc_b, u, xall, mod_lat, mod_ctx,
            gng, sng, sd_x, post_g, w_pa, w_pb, w_pc, w_out, *, lc):
    b, tt, d = xall.shape
    tm = FIN_TM

    def tok(width, cb=0):
        return pl.BlockSpec((1, tm, width), lambda bi, i: (bi, i, cb))

    def const_spec(a):
        nd = a.ndim
        return pl.BlockSpec(a.shape, lambda bi, i: (0,) * nd)

    consts = (gng, sng, sd_x, post_g, w_pa, w_pb, w_pc, w_out)
    return pl.pallas_call(
        functools.partial(_fin_kernel, nctx=lc // tm),
        grid=(b, tt // tm),
        in_specs=[
            tok(d), tok(d), tok(GLA_DV), tok(GLA_DV), tok(SSD_INNER), tok(SSD_INNER),
            tok(d, C_LG // d), tok(GLA_DV, C_GG // GLA_DV), tok(SSD_INNER, C_Z // SSD_INNER),
            tok(3 * d, C_MRG // (3 * d)), tok(SSD_INNER, C_XBC // SSD_INNER), tok(d),
            pl.BlockSpec((1, 1, 3 * d), lambda bi, i: (bi, 0, 0)),
            pl.BlockSpec((1, 3 * d), lambda bi, i: (0, 0)),
        ] + [const_spec(a) for a in consts],
        out_specs=tok(d),
        out_shape=jax.ShapeDtypeStruct((b, tt, d), F32),
        compiler_params=pltpu.CompilerParams(
            dimension_semantics=("arbitrary", "arbitrary"), vmem_limit_bytes=VMEM_LIMIT),
        name="finish",
    )(ya_f, ya_b, yb_f, yb_b, yc_f, yc_b, u, u, u, u, u, xall, mod_lat, mod_ctx, *consts)


def _reorder_in_weights(w_in):
    d = w_in.shape[0]
    o = 0
    seg = {}
    for name, wd in (("xa", D_MODEL), ("lg", D_MODEL), ("q", GLA_DK), ("k", GLA_DK), ("v", GLA_DV),
                     ("gg", GLA_DV), ("low", 2 * GLA_RANK), ("z", SSD_INNER), ("xbc", SSD_XBC),
                     ("dt", 2 * SSD_HEADS), ("mrg", 3 * D_MODEL)):
        seg[name] = w_in[:, o:o + wd]
        o += wd
    pad = jnp.zeros((d, TN - 2 * GLA_RANK - 2 * SSD_HEADS), w_in.dtype)
    cols = [seg["xbc"], seg["mrg"], seg["z"], seg["xa"], seg["v"], seg["lg"], seg["gg"],
            seg["q"], seg["k"], seg["low"], seg["dt"], pad]
    return jnp.concatenate(cols, axis=1).astype(BF16)


def _conv_tables(conv_a_w, conv_a_b, conv_c_w, conv_c_b):
    cw = jnp.zeros((CONV_W, N_TOT), F32)
    cw = cw.at[:, C_XBC:C_XBC + SSD_XBC].set(conv_c_w).at[:, C_XA:C_XA + D_MODEL].set(conv_a_w)
    cb = jnp.zeros((1, N_TOT), F32)
    cb = cb.at[0, C_XBC:C_XBC + SSD_XBC].set(conv_c_b).at[0, C_XA:C_XA + D_MODEL].set(conv_a_b)
    return cw, cb


def _scan_tables(gla_alpha_up, gla_alpha_b, ssd_dt_bias, ssd_a_log):
    wup = jnp.zeros((2, TN, GLA_DK), F32)
    dtb = jnp.zeros((2, TN), F32)
    a_log = jnp.zeros((2, TN), F32)
    valid = np.zeros((2, TN), np.float32)
    e_x = np.zeros((2, TN, SSD_INNER), np.float32)
    for dr in range(2):
        lo = SM_LOW + dr * GLA_RANK
        wup = wup.at[dr, lo:lo + GLA_RANK, :].set(gla_alpha_up[dr])
        do = SM_DT + dr * SSD_HEADS
        dtb = dtb.at[dr, do:do + SSD_HEADS].set(ssd_dt_bias[dr])
        a_log = a_log.at[dr, do:do + SSD_HEADS].set(ssd_a_log[dr])
        valid[dr, do:do + SSD_HEADS] = 1.0
        for h in range(SSD_HEADS):
            e_x[dr, do + h, h * SSD_P:(h + 1) * SSD_P] = 1.0
    return wup.astype(BF16), gla_alpha_b, jnp.asarray(e_x), dtb, a_log, jnp.asarray(valid)


def _to_col_major(h, rows):
    b, length, dm = h.shape
    return h.reshape(b, rows, GRID_W, dm).transpose(0, 2, 1, 3).reshape(b, length, dm)


def _from_col_major(h, rows):
    b, length, dm = h.shape
    return h.reshape(b, GRID_W, rows, dm).transpose(0, 2, 1, 3).reshape(b, length, dm)


def _neg_exp_kernel(a_ref, v_ref, o_ref):
    o_ref[...] = -jnp.exp(a_ref[...]) * v_ref[...]


def kernel(x, c, ctx, c_ctx, ada_w, ada_b, pre_g, post_g, w_in, conv_a_w, conv_a_b, lru_wr, lru_br,
           lru_wi, lru_bi, lru_lam, gla_alpha_up, gla_alpha_b, gla_norm_g, conv_c_w, conv_c_b,
           ssd_a_log, ssd_dt_bias, ssd_d, ssd_norm_g, w_pa, w_pb, w_pc, w_out):
    b, seq, d = x.shape
    lc = ctx.shape[1]
    depth = w_in.shape[0]
    rows = seq // GRID_W
    assert d == D_MODEL and lc % SCAN_TB == 0 and seq % SCAN_TB == 0 and b % 8 == 0

    pad_rows = (-(b + 1)) % 8
    cc = jnp.concatenate([c, c_ctx[None, :], jnp.zeros((pad_rows, d), F32)], axis=0)
    mods = _adaln(cc, ada_w, ada_b)

    x_lat, x_ctx = x, ctx
    for l in range(depth):
        col_major = l % 2 == 1
        line = rows if col_major else GRID_W
        mod_lat = mods[l, :b].reshape(b, 1, 3 * d)
        mod_ctx = mods[l, b:b + 1]
        lat = _to_col_major(x_lat, rows) if col_major else x_lat
        xall = jnp.concatenate([x_ctx, lat], axis=1)

        cw, cb = _conv_tables(conv_a_w[l], conv_a_b[l], conv_c_w[l], conv_c_b[l])
        u = _in_proj(xall, mod_lat, mod_ctx, pre_g[l].reshape(1, d), _reorder_in_weights(w_in[l]),
                     cw, cb, lc=lc, line=line)

        ya_f, ya_b = _lru(u, lru_wr[l].astype(BF16), lru_wi[l].astype(BF16), lru_br[l], lru_bi[l],
                          lru_lam[l], lc=lc)

        wup, gb, e_x, dtb, a_log, valid = _scan_tables(gla_alpha_up[l], gla_alpha_b[l],
                                                       ssd_dt_bias[l], ssd_a_log[l])
        na = pl.pallas_call(_neg_exp_kernel, out_shape=jax.ShapeDtypeStruct(a_log.shape, F32),
                            name="ssd_neg_a")(a_log, valid)
        yb_f, yc_f, yb_b, yc_b = _scan(u, wup, gb, e_x, dtb, na, lc=lc)

        xnew = _finish(ya_f, ya_b, yb_f, yb_b, yc_f, yc_b, u, xall, mod_lat, mod_ctx,
                       jnp.tile(gla_norm_g[l], GLA_HEADS).reshape(1, GLA_DV),
                       ssd_norm_g[l].reshape(1, SSD_INNER),
                       jnp.repeat(ssd_d[l], SSD_P).reshape(1, SSD_INNER),
                       post_g[l].reshape(1, d),
                       w_pa[l].astype(BF16), w_pb[l].astype(BF16), w_pc[l].astype(BF16),
                       w_out[l].astype(BF16), lc=lc)
        x_ctx = xnew[:, :lc]
        lat = xnew[:, lc:]
        x_lat = _from_col_major(lat, rows) if col_major else lat
    return x_lat
```

```python
import functools

import jax
import jax.numpy as jnp
import numpy as np
from jax import lax
from jax.experimental import pallas as pl
from jax.experimental.pallas import tpu as pltpu

F32 = jnp.float32
BF16 = jnp.bfloat16

D_MODEL = 1024
GRID_W = 64
CONV_W = 4
EPS = 1e-6
LRU_BW = 128
LRU_C = 8.0
GLA_HEADS = 4
GLA_DK = 512
GLA_DV = 1024
GLA_DKH = 128
GLA_DVH = 256
GLA_RANK = 16
GLA_TAU = 16.0
SSD_INNER = 2048
SSD_P = 64
SSD_HEADS = 32
SSD_N = 128
SSD_G = 4
SSD_GW = SSD_INNER // SSD_G
SSD_XBC = SSD_INNER + 2 * SSD_G * SSD_N
CHUNK = 64

TN = 256
C_XBC, C_MRG, C_Z, C_XA, C_V, C_LG, C_GG, C_Q, C_K, C_SM = (
    0, 3072, 6144, 8192, 9216, 10240, 11264, 12288, 12800, 13312)
N_TOT = C_SM + TN

VMEM_LIMIT = 56 * 1024 * 1024


def _silu(x):
    return x * jax.nn.sigmoid(x)


def _softplus(x):
    return jnp.maximum(x, 0.0) + jnp.log1p(jnp.exp(-jnp.abs(x)))


def _dot(a, b):
    return jnp.dot(a, b, preferred_element_type=F32)


def _dot_nt(a, b):
    return lax.dot_general(a, b, (((1,), (1,)), ((), ())), preferred_element_type=F32)


def _dot_tn(a, b):
    return lax.dot_general(a, b, (((0,), (0,)), ((), ())), preferred_element_type=F32)


def _adaln_kernel(c_ref, w_ref, b_ref, o_ref):
    c = c_ref[...]
    o_ref[0] = _dot(_silu(c).astype(BF16), w_ref[0]) + b_ref[0]


def _adaln(cc, ada_w, ada_b):
    depth, d, n3 = ada_w.shape
    rows = cc.shape[0]
    tn = 1024
    return pl.pallas_call(
        _adaln_kernel,
        grid=(depth, n3 // tn),
        in_specs=[
            pl.BlockSpec((rows, d), lambda l, n: (0, 0)),
            pl.BlockSpec((1, d, tn), lambda l, n: (l, 0, n)),
            pl.BlockSpec((1, 1, tn), lambda l, n: (l, 0, n)),
        ],
        out_specs=pl.BlockSpec((1, rows, tn), lambda l, n: (l, 0, n)),
        out_shape=jax.ShapeDtypeStruct((depth, rows, n3), F32),
        name="adaln",
    )(cc, ada_w.astype(BF16), ada_b.reshape(depth, 1, n3))


def _in_kernel(x_ref, ml_ref, mc_ref, pg_ref, w_ref, cw_ref, cb_ref, u_ref, h_s, *, lc, line):
    j = pl.program_id(1)
    tt = x_ref.shape[1]
    d = x_ref.shape[2]

    @pl.when(j == 0)
    def _():
        def norm_mod(x, mod):
            y = x * lax.rsqrt(jnp.mean(x * x, axis=-1, keepdims=True) + EPS) * pg_ref[...]
            return (y * (1.0 + mod[:, d:2 * d]) + mod[:, 0:d]).astype(BF16)
        h_s[0:lc, :] = norm_mod(x_ref[0, 0:lc, :], mc_ref[...])
        h_s[lc:tt, :] = norm_mod(x_ref[0, lc:tt, :], ml_ref[0])

    u_ref[0] = _dot(h_s[...], w_ref[...])

    def conv(act):
        a = u_ref[0]
        t = lax.broadcasted_iota(jnp.int32, (tt, 1), 0)
        is_ctx = t < lc
        pos = jnp.where(is_ctx, t, (t - lc) % line)
        ll = jnp.where(is_ctx, lc, line)
        cw = cw_ref[...]
        out = cb_ref[...] + cw[1:2, :] * a
        out = out + cw[0:1, :] * jnp.where(pos >= 1, pltpu.roll(a, 1, 0), 0.0)
        out = out + cw[2:3, :] * jnp.where(pos + 1 < ll, pltpu.roll(a, tt - 1, 0), 0.0)
        out = out + cw[3:4, :] * jnp.where(pos + 2 < ll, pltpu.roll(a, tt - 2, 0), 0.0)
        u_ref[0] = act(out)

    @pl.when(j < SSD_XBC // TN)
    def _():
        conv(_silu)

    @pl.when((j >= C_XA // TN) & (j < (C_XA + D_MODEL) // TN))
    def _():
        conv(lambda v: v)


def _in_proj(xall, mod_lat, mod_ctx, pre_g, w_in_p, cw, cb, *, lc, line):
    b, tt, d = xall.shape
    return pl.pallas_call(
        functools.partial(_in_kernel, lc=lc, line=line),
        grid=(b, N_TOT // TN),
        in_specs=[
            pl.BlockSpec((1, tt, d), lambda i, j: (i, 0, 0)),
            pl.BlockSpec((1, 1, 3 * d), lambda i, j: (i, 0, 0)),
            pl.BlockSpec((1, 3 * d), lambda i, j: (0, 0)),
            pl.BlockSpec((1, d), lambda i, j: (0, 0)),
            pl.BlockSpec((d, TN), lambda i, j: (0, j)),
            pl.BlockSpec((CONV_W, TN), lambda i, j: (0, j)),
            pl.BlockSpec((1, TN), lambda i, j: (0, j)),
        ],
        out_specs=pl.BlockSpec((1, tt, TN), lambda i, j: (i, 0, j)),
        out_shape=jax.ShapeDtypeStruct((b, tt, N_TOT), F32),
        scratch_shapes=[pltpu.VMEM((tt, d), BF16)],
        compiler_params=pltpu.CompilerParams(
            dimension_semantics=("arbitrary", "arbitrary"), vmem_limit_bytes=VMEM_LIMIT),
        name="in_proj",
    )(xall, mod_lat, mod_ctx, pre_g, w_in_p, cw, cb)


LRU_TC = 64
LRU_WT = 512


def _lru_kernel(xf_ref, xb_ref, wr_ref, wi_ref, br_ref, bi_ref, lam_ref, yf_ref, yb_ref,
                a_s, b_s, o_s, h_s):
    i = pl.program_id(1)
    nb, tc, wt = xf_ref.shape

    @pl.when(i == 0)
    def _():
        h_s[...] = jnp.zeros_like(h_s)

    for dr, x_ref in enumerate((xf_ref, xb_ref)):
        x = x_ref[...].reshape(nb * tc, wt)
        sp = _softplus(-lam_ref[dr:dr + 1, :])
        for n in range(wt // LRU_BW):
            sl = slice(n * LRU_BW, (n + 1) * LRU_BW)
            xs = x[:, sl]
            xs16 = xs.astype(BF16)
            r = jax.nn.sigmoid(_dot(xs16, wr_ref[dr, n]) + br_ref[dr:dr + 1, sl])
            g = jax.nn.sigmoid(_dot(xs16, wi_ref[dr, n]) + bi_ref[dr:dr + 1, sl])
            log_a = (-LRU_C * sp[:, sl]) * r
            a = jnp.exp(log_a)
            a_s[dr, n] = a
            b_s[dr, n] = jnp.sqrt(-jnp.tanh(log_a) * (a * a + 1.0)) * (g * xs)

    nl = wt // LRU_BW
    hf = [h_s[0, n] for n in range(nl)]
    hb = [h_s[1, n] for n in range(nl)]
    for t in range(tc):
        rows_f = pl.ds(t, nb, stride=tc)
        rows_b = pl.ds(tc - 1 - t, nb, stride=tc)
        for n in range(nl):
            hf[n] = a_s[0, n, rows_f, :] * hf[n] + b_s[0, n, rows_f, :]
            hb[n] = a_s[1, n, rows_b, :] * hb[n] + b_s[1, n, rows_b, :]
            o_s[0, n, rows_f, :] = hf[n]
            o_s[1, n, rows_b, :] = hb[n]
    for n in range(nl):
        sl = slice(n * LRU_BW, (n + 1) * LRU_BW)
        h_s[0, n] = hf[n]
        h_s[1, n] = hb[n]
        yf_ref[:, :, sl] = o_s[0, n].reshape(nb, tc, LRU_BW)
        yb_ref[:, :, sl] = o_s[1, n].reshape(nb, tc, LRU_BW)


def _bwd_block(i, nctx, ntot):
    return jnp.where(i < nctx, nctx - 1 - i, ntot - 1 + nctx - i)


def _lru(u, wr, wi, br, bi, lam, *, lc):
    b, tt, _ = u.shape
    w = D_MODEL
    nt = tt // LRU_TC
    nctx = lc // LRU_TC
    c0 = C_XA // LRU_WT
    nblk = LRU_WT // LRU_BW
    y_spec_f = pl.BlockSpec((b, LRU_TC, LRU_WT), lambda j, i: (0, i, j))
    y_spec_b = pl.BlockSpec((b, LRU_TC, LRU_WT), lambda j, i: (0, _bwd_block(i, nctx, nt), j))
    return pl.pallas_call(
        _lru_kernel,
        grid=(w // LRU_WT, nt),
        in_specs=[
            pl.BlockSpec((b, LRU_TC, LRU_WT), lambda j, i: (0, i, c0 + j)),
            pl.BlockSpec((b, LRU_TC, LRU_WT), lambda j, i: (0, _bwd_block(i, nctx, nt), c0 + j)),
            pl.BlockSpec((2, nblk, LRU_BW, LRU_BW), lambda j, i: (0, j, 0, 0)),
            pl.BlockSpec((2, nblk, LRU_BW, LRU_BW), lambda j, i: (0, j, 0, 0)),
            pl.BlockSpec((2, LRU_WT), lambda j, i: (0, j)),
            pl.BlockSpec((2, LRU_WT), lambda j, i: (0, j)),
            pl.BlockSpec((2, LRU_WT), lambda j, i: (0, j)),
        ],
        out_specs=[y_spec_f, y_spec_b],
        out_shape=[jax.ShapeDtypeStruct((b, tt, w), F32)] * 2,
        scratch_shapes=[
            pltpu.VMEM((2, nblk, b * LRU_TC, LRU_BW), F32),
            pltpu.VMEM((2, nblk, b * LRU_TC, LRU_BW), F32),
            pltpu.VMEM((2, nblk, b * LRU_TC, LRU_BW), F32),
            pltpu.VMEM((2, nblk, b, LRU_BW), F32),
        ],
        compiler_params=pltpu.CompilerParams(
            dimension_semantics=("arbitrary", "arbitrary"), vmem_limit_bytes=VMEM_LIMIT),
        name="lru_scan",
    )(u, u, wr, wi, br, bi, lam)


SCAN_TB = 128
SM_W = 128
SM_DT = 0
SM_LOW = SSD_HEADS


def _split3(x):
    hi = x.astype(BF16).astype(F32)
    r = x - hi
    mid = r.astype(BF16).astype(F32)
    return hi, mid, r - mid


def _cumsum_rows(tri16, x):
    n = x.shape[1]
    p = _dot(tri16, jnp.concatenate([s.astype(BF16) for s in _split3(x)], axis=1))
    return (p[:, 0:n] + p[:, n:2 * n]) + p[:, 2 * n:3 * n]


def _pack3(x, lane):
    hi, mid, lo = _split3(x)
    h = SSD_HEADS
    return jnp.where(lane < h, hi,
                     jnp.where(lane < 2 * h, pltpu.roll(mid, h, 1),
                               jnp.where(lane < 3 * h, pltpu.roll(lo, 2 * h, 1), 0.0)))


def _gla_chunk(dr, qk, v, sm16, wup_ref, gb_ref, tri16, causal, gla_s, last):
    z = _dot(sm16, wup_ref[dr]) + gb_ref[dr:dr + 1, :]
    logg = -_softplus(-z) * (1.0 / GLA_TAU)
    bc = _cumsum_rows(tri16, logg)
    btot = bc[last:last + 1, :]
    qe = (qk[:, 0:GLA_DK] * (GLA_DKH ** -0.5)) * jnp.exp(bc)
    k = qk[:, GLA_DK:2 * GLA_DK]
    ke = k * jnp.exp(-bc)
    kd = k * jnp.exp(btot - bc)
    etot = jnp.exp(btot)
    outs = []
    for h in range(GLA_HEADS):
        ks = slice(h * GLA_DKH, (h + 1) * GLA_DKH)
        vs = slice(h * GLA_DVH, (h + 1) * GLA_DVH)
        qe16 = qe[:, ks].astype(BF16)
        v16 = v[:, vs].astype(BF16)
        att = jnp.where(causal, _dot_nt(qe16, ke[:, ks].astype(BF16)), 0.0)
        st = gla_s[dr, h]
        outs.append(_dot(att.astype(BF16), v16) + _dot_nt(qe16, st.astype(BF16)))
        gla_s[dr, h] = st * etot[:, ks] + _dot_tn(v16, kd[:, ks].astype(BF16))
    return jnp.concatenate(outs, axis=1)


def _ssd_chunk(dr, xbc, sm, e3_ref, dtb_ref, na_ref, tri16, causal_x, blk_mask, lane, ones_part, ssd_s, last):
    dt = _softplus(sm + dtb_ref[dr:dr + 1, :])
    cum = _cumsum_rows(tri16, dt * na_ref[dr:dr + 1, :])
    clast = cum[last:last + 1, :]
    packed = _pack3(jnp.concatenate([cum, dt, jnp.exp(cum), jnp.exp(clast - cum)], axis=0), lane)
    pt = packed[0:CHUNK].T
    pt2 = (-jnp.concatenate([pt, pt], axis=1)).astype(BF16)
    rm = jnp.concatenate([pt2] * (SSD_INNER // 128), axis=1) * e3_ref[...]
    a16 = jnp.concatenate([packed.astype(BF16), ones_part], axis=1)
    big = _dot(a16, jnp.concatenate([e3_ref[...], rm], axis=0))
    lmat = jnp.exp(jnp.where(causal_x, big[0:CHUNK], -jnp.inf))
    dt_x = big[CHUNK:2 * CHUNK]
    ecum_x = big[2 * CHUNK:3 * CHUNK]
    dec_x = big[3 * CHUNK:4 * CHUNK]
    xs = xbc[:, 0:SSD_INNER]
    xdt = xs * dt_x
    xdec16 = (xdt * dec_x).astype(BF16)
    xdt16 = xdt.astype(BF16)
    ys = []
    for g in range(SSD_G):
        gs = slice(g * SSD_GW, (g + 1) * SSD_GW)
        bm16 = xbc[:, SSD_INNER + g * SSD_N:SSD_INNER + (g + 1) * SSD_N].astype(BF16)
        cm16 = xbc[:, SSD_INNER + (SSD_G + g) * SSD_N:SSD_INNER + (SSD_G + g + 1) * SSD_N].astype(BF16)
        b_tiled = jnp.concatenate([bm16] * (SSD_GW // CHUNK), axis=0)
        m16 = (_dot_nt(cm16, b_tiled) * lmat[:, gs]).astype(BF16)
        parts = []
        for pr in range(SSD_GW // 128):
            ls = slice(g * SSD_GW + pr * 128, g * SSD_GW + (pr + 1) * 128)
            x2 = xdt16[:, ls]
            bd = jnp.where(blk_mask, jnp.concatenate([x2, x2], axis=0), jnp.zeros((), BF16))
            parts.append(_dot(m16[:, pr * 128:(pr + 1) * 128], bd))
        st = ssd_s[dr, g]
        ys.append(jnp.concatenate(parts, axis=1) + _dot(cm16, st.astype(BF16)) * ecum_x[:, gs])
        ssd_s[dr, g] = st * ecum_x[last:last + 1, gs] + _dot_tn(bm16, xdec16[:, gs])
    return jnp.concatenate(ys, axis=1)


def _scan_kernel(qkf_ref, vf_ref, xf_ref, sf_ref, qkb_ref, vb_ref, xb_ref, sb_ref,
                 wup_ref, gb_ref, e3_ref, dtb_ref, na_ref,
                 ybf_ref, ycf_ref, ybb_ref, ycb_ref, gla_s, ssd_s):
    i = pl.program_id(1)

    @pl.when(i == 0)
    def _():
        gla_s[...] = jnp.zeros_like(gla_s)
        ssd_s[...] = jnp.zeros_like(ssd_s)

    tb = qkf_ref.shape[1]
    nch = tb // CHUNK
    row = lax.broadcasted_iota(jnp.int32, (CHUNK, CHUNK), 0)
    col = lax.broadcasted_iota(jnp.int32, (CHUNK, CHUNK), 1)
    row_x = lax.broadcasted_iota(jnp.int32, (CHUNK, SSD_INNER), 0)
    col_x = lax.broadcasted_iota(jnp.int32, (CHUNK, SSD_INNER), 1) & (CHUNK - 1)
    r2 = lax.broadcasted_iota(jnp.int32, (2 * CHUNK, 128), 0)
    c2 = lax.broadcasted_iota(jnp.int32, (2 * CHUNK, 128), 1)
    blk_mask = (r2 < CHUNK) == (c2 < CHUNK)
    lane = lax.broadcasted_iota(jnp.int32, (4 * CHUNK, SM_W), 1)
    prow = lax.broadcasted_iota(jnp.int32, (4 * CHUNK, SM_W), 0)
    ones_part = jnp.where((prow < CHUNK) & (lane < 3 * SSD_HEADS), 1.0, 0.0).astype(BF16)

    dirs = (
        (0, qkf_ref, vf_ref, xf_ref, sf_ref, ybf_ref, ycf_ref),
        (1, qkb_ref, vb_ref, xb_ref, sb_ref, ybb_ref, ycb_ref),
    )
    for step in range(nch):
        for dr, qk_ref, v_ref, x_ref, s_ref, yb_ref, yc_ref in dirs:
            fwd = dr == 0
            c = step if fwd else nch - 1 - step
            rs = slice(c * CHUNK, (c + 1) * CHUNK)
            causal = (row >= col) if fwd else (row <= col)
            causal_x = (row_x >= col_x) if fwd else (row_x <= col_x)
            tri16 = jnp.where(causal, 1.0, 0.0).astype(BF16)
            last = CHUNK - 1 if fwd else 0
            sm = s_ref[0, rs, :]
            yb_ref[0, rs, :] = _gla_chunk(dr, qk_ref[0, rs, :], v_ref[0, rs, :], sm.astype(BF16),
                                          wup_ref, gb_ref, tri16, causal, gla_s, last)
            yc_ref[0, rs, :] = _ssd_chunk(dr, x_ref[0, rs, :], sm, e3_ref, dtb_ref, na_ref, tri16,
                                          causal_x, blk_mask, lane, ones_part, ssd_s, last)


def _scan(u, wup, gb, e3, dtb, na, *, lc):
    b, tt, _ = u.shape
    tb = SCAN_TB
    nt = tt // tb
    nctx = lc // tb

    def fmap(cb):
        return lambda bi, i: (bi, i, cb)

    def bmap(cb):
        return lambda bi, i: (bi, _bwd_block(i, nctx, nt), cb)

    def tok_specs(mk, dr):
        return [
            pl.BlockSpec((1, tb, 2 * GLA_DK), mk(C_Q // (2 * GLA_DK))),
            pl.BlockSpec((1, tb, GLA_DV), mk(C_V // GLA_DV)),
            pl.BlockSpec((1, tb, SSD_XBC), mk(C_XBC // SSD_XBC)),
            pl.BlockSpec((1, tb, SM_W), mk(C_SM // SM_W + dr)),
        ]

    def const_spec(a):
        nd = a.ndim
        return pl.BlockSpec(a.shape, lambda bi, i: (0,) * nd)

    consts = (wup, gb, e3, dtb, na)
    return pl.pallas_call(
        _scan_kernel,
        grid=(b, nt),
        in_specs=tok_specs(fmap, 0) + tok_specs(bmap, 1) + [const_spec(a) for a in consts],
        out_specs=[
            pl.BlockSpec((1, tb, GLA_DV), fmap(0)),
            pl.BlockSpec((1, tb, SSD_INNER), fmap(0)),
            pl.BlockSpec((1, tb, GLA_DV), bmap(0)),
            pl.BlockSpec((1, tb, SSD_INNER), bmap(0)),
        ],
        out_shape=[
            jax.ShapeDtypeStruct((b, tt, GLA_DV), F32),
            jax.ShapeDtypeStruct((b, tt, SSD_INNER), F32),
            jax.ShapeDtypeStruct((b, tt, GLA_DV), F32),
            jax.ShapeDtypeStruct((b, tt, SSD_INNER), F32),
        ],
        scratch_shapes=[
            pltpu.VMEM((2, GLA_HEADS, GLA_DVH, GLA_DKH), F32),
            pltpu.VMEM((2, SSD_G, SSD_N, SSD_GW), F32),
        ],
        compiler_params=pltpu.CompilerParams(
            dimension_semantics=("arbitrary", "arbitrary"), vmem_limit_bytes=VMEM_LIMIT),
        name="gla_ssd_scan",
    )(u, u, u, u, u, u, u, u, *consts)


FIN_TM = 128


def _group_rms(x, width):
    parts = []
    for s in range(x.shape[1] // width):
        xs = x[:, s * width:(s + 1) * width]
        parts.append(xs * lax.rsqrt(jnp.mean(xs * xs, axis=-1, keepdims=True) + EPS))
    return jnp.concatenate(parts, axis=1)


def _fin_kernel(yaf_ref, yab_ref, ybf_ref, ybb_ref, ycf_ref, ycb_ref, lg_ref, gg_ref, z_ref, mg_ref,
                xs_ref, x_ref, ml_ref, mc_ref, gng_ref, sng_ref, sd_ref, pog_ref,
                wpa_ref, wpb_ref, wpc_ref, wo_ref, o_ref, *, nctx):
    i = pl.program_id(1)
    d = D_MODEL
    ya = yaf_ref[0] + yab_ref[0]
    pa = _dot((ya * _silu(lg_ref[0])).astype(BF16), wpa_ref[...])
    yb = ybf_ref[0] + ybb_ref[0]
    ob = (_group_rms(yb, GLA_DVH) * gng_ref[...]) * _silu(gg_ref[0])
    pb = _dot(ob.astype(BF16), wpb_ref[...])
    yc = (ycf_ref[0] + ycb_ref[0]) + sd_ref[...] * xs_ref[0]
    oc = _group_rms(yc * _silu(z_ref[0]), SSD_GW) * sng_ref[...]
    pc = _dot(oc.astype(BF16), wpc_ref[...])
    gates = jax.nn.sigmoid(mg_ref[0])
    merged = gates[:, 0:d] * pa + gates[:, d:2 * d] * pb + gates[:, 2 * d:3 * d] * pc
    out = _dot(merged.astype(BF16), wo_ref[...])
    gate = jnp.where(i < nctx, mc_ref[:, 2 * d:3 * d], ml_ref[0][:, 2 * d:3 * d])
    o_ref[0] = x_ref[0] + gate * (_group_rms(out, d) * pog_ref[...])


def _finish(ya_f, ya_b, yb_f, yb_b, yc_f, yc_b, u, xall, mod_lat, mod_ctx,
            gng, sng, sd_x, post_g, w_pa, w_pb, w_pc, w_out, *, lc):
    b, tt, d = xall.shape
    tm = FIN_TM

    def tok(width, cb=0):
        return pl.BlockSpec((1, tm, width), lambda bi, i: (bi, i, cb))

    def const_spec(a):
        nd = a.ndim
        return pl.BlockSpec(a.shape, lambda bi, i: (0,) * nd)

    consts = (gng, sng, sd_x, post_g, w_pa, w_pb, w_pc, w_out)
    return pl.pallas_call(
        functools.partial(_fin_kernel, nctx=lc // tm),
        grid=(b, tt // tm),
        in_specs=[
            tok(d), tok(d), tok(GLA_DV), tok(GLA_DV), tok(SSD_INNER), tok(SSD_INNER),
            tok(d, C_LG // d), tok(GLA_DV, C_GG // GLA_DV), tok(SSD_INNER, C_Z // SSD_INNER),
            tok(3 * d, C_MRG // (3 * d)), tok(SSD_INNER, C_XBC // SSD_INNER), tok(d),
            pl.BlockSpec((1, 1, 3 * d), lambda bi, i: (bi, 0, 0)),
            pl.BlockSpec((1, 3 * d), lambda bi, i: (0, 0)),
        ] + [const_spec(a) for a in consts],
        out_specs=tok(d),
        out_shape=jax.ShapeDtypeStruct((b, tt, d), F32),
        compiler_params=pltpu.CompilerParams(
            dimension_semantics=("arbitrary", "arbitrary"), vmem_limit_bytes=VMEM_LIMIT),
        name="finish",
    )(ya_f, ya_b, yb_f, yb_b, yc_f, yc_b, u, u, u, u, u, xall, mod_lat, mod_ctx, *consts)


def _reorder_in_weights(w_in):
    d = w_in.shape[0]
    o = 0
    seg = {}
    for name, wd in (("xa", D_MODEL), ("lg", D_MODEL), ("q", GLA_DK), ("k", GLA_DK), ("v", GLA_DV),
                     ("gg", GLA_DV), ("low", 2 * GLA_RANK), ("z", SSD_INNER), ("xbc", SSD_XBC),
                     ("dt", 2 * SSD_HEADS), ("mrg", 3 * D_MODEL)):
        seg[name] = w_in[:, o:o + wd]
        o += wd
    pad = jnp.zeros((d, SM_W - GLA_RANK - SSD_HEADS), w_in.dtype)
    cols = [seg["xbc"], seg["mrg"], seg["z"], seg["xa"], seg["v"], seg["lg"], seg["gg"],
            seg["q"], seg["k"]]
    for dr in range(2):
        cols += [seg["dt"][:, dr * SSD_HEADS:(dr + 1) * SSD_HEADS],
                 seg["low"][:, dr * GLA_RANK:(dr + 1) * GLA_RANK], pad]
    return jnp.concatenate(cols, axis=1).astype(BF16)


def _conv_tables(conv_a_w, conv_a_b, conv_c_w, conv_c_b):
    cw = jnp.zeros((CONV_W, N_TOT), F32)
    cw = cw.at[:, C_XBC:C_XBC + SSD_XBC].set(conv_c_w).at[:, C_XA:C_XA + D_MODEL].set(conv_a_w)
    cb = jnp.zeros((1, N_TOT), F32)
    cb = cb.at[0, C_XBC:C_XBC + SSD_XBC].set(conv_c_b).at[0, C_XA:C_XA + D_MODEL].set(conv_a_b)
    return cw, cb


def _scan_tables(gla_alpha_up, ssd_dt_bias, ssd_a_log):
    wup = jnp.zeros((2, SM_W, GLA_DK), F32).at[:, SM_LOW:SM_LOW + GLA_RANK, :].set(gla_alpha_up)
    dtb = jnp.zeros((2, SM_W), F32).at[:, SM_DT:SM_DT + SSD_HEADS].set(ssd_dt_bias)
    a_log = jnp.zeros((2, SM_W), F32).at[:, SM_DT:SM_DT + SSD_HEADS].set(ssd_a_log)
    valid = np.zeros((2, SM_W), np.float32)
    valid[:, SM_DT:SM_DT + SSD_HEADS] = 1.0
    e3 = np.zeros((SM_W, SSD_INNER), np.float32)
    for k in range(3 * SSD_HEADS):
        h = k % SSD_HEADS
        e3[k, h * SSD_P:(h + 1) * SSD_P] = 1.0
    return wup.astype(BF16), jnp.asarray(e3, BF16), dtb, a_log, jnp.asarray(valid)


def _to_col_major(h, rows):
    b, length, dm = h.shape
    return h.reshape(b, rows, GRID_W, dm).transpose(0, 2, 1, 3).reshape(b, length, dm)


def _from_col_major(h, rows):
    b, length, dm = h.shape
    return h.reshape(b, GRID_W, rows, dm).transpose(0, 2, 1, 3).reshape(b, length, dm)


def _neg_exp_kernel(a_ref, v_ref, o_ref):
    o_ref[...] = -jnp.exp(a_ref[...]) * v_ref[...]


def kernel(x, c, ctx, c_ctx, ada_w, ada_b, pre_g, post_g, w_in, conv_a_w, conv_a_b, lru_wr, lru_br,
           lru_wi, lru_bi, lru_lam, gla_alpha_up, gla_alpha_b, gla_norm_g, conv_c_w, conv_c_b,
           ssd_a_log, ssd_dt_bias, ssd_d, ssd_norm_g, w_pa, w_pb, w_pc, w_out):
    b, seq, d = x.shape
    lc = ctx.shape[1]
    depth = w_in.shape[0]
    rows = seq // GRID_W
    assert d == D_MODEL and lc % SCAN_TB == 0 and seq % SCAN_TB == 0 and b % 8 == 0

    pad_rows = (-(b + 1)) % 8
    cc = jnp.concatenate([c, c_ctx[None, :], jnp.zeros((pad_rows, d), F32)], axis=0)
    mods = _adaln(cc, ada_w, ada_b)

    x_lat, x_ctx = x, ctx
    for l in range(depth):
        col_major = l % 2 == 1
        line = rows if col_major else GRID_W
        mod_lat = mods[l, :b].reshape(b, 1, 3 * d)
        mod_ctx = mods[l, b:b + 1]
        lat = _to_col_major(x_lat, rows) if col_major else x_lat
        xall = jnp.concatenate([x_ctx, lat], axis=1)

        cw, cb = _conv_tables(conv_a_w[l], conv_a_b[l], conv_c_w[l], conv_c_b[l])
        u = _in_proj(xall, mod_lat, mod_ctx, pre_g[l].reshape(1, d), _reorder_in_weights(w_in[l]),
                     cw, cb, lc=lc, line=line)

        ya_f, ya_b = _lru(u, lru_wr[l].astype(BF16), lru_wi[l].astype(BF16), lru_br[l], lru_bi[l],
                          lru_lam[l], lc=lc)

        wup, e3, dtb, a_log, valid = _scan_tables(gla_alpha_up[l], ssd_dt_bias[l], ssd_a_log[l])
        na = pl.pallas_call(_neg_exp_kernel, out_shape=jax.ShapeDtypeStruct(a_log.shape, F32),
                            name="ssd_neg_a")(a_log, valid)
        yb_f, yc_f, yb_b, yc_b = _scan(u, wup, gla_alpha_b[l], e3, dtb, na, lc=lc)

        xnew = _finish(ya_f, ya_b, yb_f, yb_b, yc_f, yc_b, u, xall, mod_lat, mod_ctx,
                       jnp.tile(gla_norm_g[l], GLA_HEADS).reshape(1, GLA_DV),
                       ssd_norm_g[l].reshape(1, SSD_INNER),
                       jnp.repeat(ssd_d[l], SSD_P).reshape(1, SSD_INNER),
                       post_g[l].reshape(1, d),
                       w_pa[l].astype(BF16), w_pb[l].astype(BF16), w_pc[l].astype(BF16),
                       w_out[l].astype(BF16), lc=lc)
        x_ctx = xnew[:, :lc]
        lat = xnew[:, lc:]
        x_lat = _from_col_major(lat, rows) if col_major else lat
    return x_lat
```

```python
import functools

import jax
import jax.numpy as jnp
import numpy as np
from jax import lax
from jax.experimental import pallas as pl
from jax.experimental.pallas import tpu as pltpu

F32 = jnp.float32
BF16 = jnp.bfloat16

D_MODEL = 1024
GRID_W = 64
CONV_W = 4
EPS = 1e-6
LRU_BW = 128
LRU_C = 8.0
GLA_HEADS = 4
GLA_DK = 512
GLA_DV = 1024
GLA_DKH = 128
GLA_DVH = 256
GLA_RANK = 16
GLA_TAU = 16.0
SSD_INNER = 2048
SSD_P = 64
SSD_HEADS = 32
SSD_N = 128
SSD_G = 4
SSD_GW = SSD_INNER // SSD_G
SSD_XBC = SSD_INNER + 2 * SSD_G * SSD_N
CHUNK = 64

TN = 512
C_XBC, C_MRG, C_Z, C_XA, C_V, C_LG, C_GG, C_Q, C_K, C_SM = (
    0, 3072, 6144, 8192, 9216, 10240, 11264, 12288, 12800, 13312)
N_TOT = C_SM + TN

VMEM_LIMIT = 56 * 1024 * 1024


def _sigmoid(x):
    return 0.5 * jnp.tanh(0.5 * x) + 0.5


def _silu(x):
    return x * _sigmoid(x)


def _softplus(x):
    return jnp.maximum(x, 0.0) + jnp.log1p(jnp.exp(-jnp.abs(x)))


def _dot(a, b):
    return jnp.dot(a, b, preferred_element_type=F32)


def _dot_nt(a, b):
    return lax.dot_general(a, b, (((1,), (1,)), ((), ())), preferred_element_type=F32)


def _dot_tn(a, b):
    return lax.dot_general(a, b, (((0,), (0,)), ((), ())), preferred_element_type=F32)


def _adaln_kernel(c_ref, w_ref, b_ref, o_ref):
    c = c_ref[...]
    o_ref[0] = _dot(_silu(c).astype(BF16), w_ref[0]) + b_ref[0]


def _adaln(cc, ada_w, ada_b):
    depth, d, n3 = ada_w.shape
    rows = cc.shape[0]
    tn = 1024
    return pl.pallas_call(
        _adaln_kernel,
        grid=(depth, n3 // tn),
        in_specs=[
            pl.BlockSpec((rows, d), lambda l, n: (0, 0)),
            pl.BlockSpec((1, d, tn), lambda l, n: (l, 0, n)),
            pl.BlockSpec((1, 1, tn), lambda l, n: (l, 0, n)),
        ],
        out_specs=pl.BlockSpec((1, rows, tn), lambda l, n: (l, 0, n)),
        out_shape=jax.ShapeDtypeStruct((depth, rows, n3), F32),
        name="adaln",
    )(cc, ada_w.astype(BF16), ada_b.reshape(depth, 1, n3))


CONV_STRIP = 64
CONV_LANES = 256

def _in_kernel(xc_ref, xl_ref, ml_ref, mc_ref, pg_ref, w_ref, cw_ref, cb_ref, u_ref, h_s, *, line):
    j = pl.program_id(1)
    lc = xc_ref.shape[1]
    tt = lc + xl_ref.shape[1]
    d = xc_ref.shape[2]

    @pl.when(j == 0)
    def _():
        def norm_mod(x, mod):
            y = x * lax.rsqrt(jnp.mean(x * x, axis=-1, keepdims=True) + EPS) * pg_ref[...]
            return (y * (1.0 + mod[:, d:2 * d]) + mod[:, 0:d]).astype(BF16)
        h_s[0:lc, :] = norm_mod(xc_ref[0], mc_ref[...])
        h_s[lc:tt, :] = norm_mod(xl_ref[0], ml_ref[0])

    u_ref[0] = _dot(h_s[...], w_ref[...])

    def conv(act):
        def strip(r0, rows, ll):
            pos = lax.broadcasted_iota(jnp.int32, (rows, 1), 0) % ll
            for c0 in range(0, TN, CONV_LANES):
                cs = slice(c0, c0 + CONV_LANES)
                a = u_ref[0, pl.ds(r0, rows), cs]
                cw = cw_ref[:, cs]
                out = cb_ref[:, cs] + cw[1:2, :] * a
                out = out + cw[0:1, :] * jnp.where(pos >= 1, pltpu.roll(a, 1, 0), 0.0)
                out = out + cw[2:3, :] * jnp.where(pos + 1 < ll, pltpu.roll(a, rows - 1, 0), 0.0)
                out = out + cw[3:4, :] * jnp.where(pos + 2 < ll, pltpu.roll(a, rows - 2, 0), 0.0)
                u_ref[0, pl.ds(r0, rows), cs] = act(out)

        strip(0, lc, lc)
        rows = max(line, CONV_STRIP)

        def body(s, carry):
            strip(pl.multiple_of(lc + s * rows, rows), rows, line)
            return carry
        lax.fori_loop(0, (tt - lc) // rows, body, 0)

    @pl.when(j < SSD_XBC // TN)
    def _():
        conv(_silu)

    @pl.when((j >= C_XA // TN) & (j < (C_XA + D_MODEL) // TN))
    def _():
        conv(lambda v: v)


def _in_proj(x_ctx, x_lat, mod_lat, mod_ctx, pre_g, w_in_p, cw, cb, *, line):
    b, lc, d = x_ctx.shape
    seq = x_lat.shape[1]
    tt = lc + seq
    assert max(line, CONV_STRIP) % line == 0 and seq % max(line, CONV_STRIP) == 0
    return pl.pallas_call(
        functools.partial(_in_kernel, line=line),
        grid=(b, N_TOT // TN),
        in_specs=[
            pl.BlockSpec((1, lc, d), lambda i, j: (i, 0, 0)),
            pl.BlockSpec((1, seq, d), lambda i, j: (i, 0, 0)),
            pl.BlockSpec((1, 1, 3 * d), lambda i, j: (i, 0, 0)),
            pl.BlockSpec((1, 3 * d), lambda i, j: (0, 0)),
            pl.BlockSpec((1, d), lambda i, j: (0, 0)),
            pl.BlockSpec((d, TN), lambda i, j: (0, j)),
            pl.BlockSpec((CONV_W, TN), lambda i, j: (0, j)),
            pl.BlockSpec((1, TN), lambda i, j: (0, j)),
        ],
        out_specs=pl.BlockSpec((1, tt, TN), lambda i, j: (i, 0, j)),
        out_shape=jax.ShapeDtypeStruct((b, tt, N_TOT), F32),
        scratch_shapes=[pltpu.VMEM((tt, d), BF16)],
        compiler_params=pltpu.CompilerParams(
            dimension_semantics=("arbitrary", "arbitrary"), vmem_limit_bytes=VMEM_LIMIT),
        name="in_proj",
    )(x_ctx, x_lat, mod_lat, mod_ctx, pre_g, w_in_p, cw, cb)


LRU_TC = 64
LRU_WT = 512
LRU_PITCH = LRU_TC + 8


def _lru_kernel(xf_ref, xb_ref, wr_ref, wi_ref, br_ref, bi_ref, lam_ref, yf_ref, yb_ref,
                a_s, b_s, o_s, h_s):
    i = pl.program_id(1)
    nb, tc, wt = xf_ref.shape

    @pl.when(i == 0)
    def _():
        h_s[...] = jnp.zeros_like(h_s)

    def gates(bi, carry):
        r0 = pl.multiple_of(bi * LRU_PITCH, 8)
        for dr, x_ref in enumerate((xf_ref, xb_ref)):
            for n in range(wt // LRU_BW):
                sl = slice(n * LRU_BW, (n + 1) * LRU_BW)
                xs = x_ref[bi, :, sl]
                xs16 = xs.astype(BF16)
                r = _sigmoid(_dot(xs16, wr_ref[dr, n]) + br_ref[dr:dr + 1, sl])
                g = _sigmoid(_dot(xs16, wi_ref[dr, n]) + bi_ref[dr:dr + 1, sl])
                nla = (LRU_C * _softplus(-lam_ref[dr:dr + 1, sl])) * r
                a = jnp.exp(-nla)
                a_s[dr, n, pl.ds(r0, tc), :] = a
                b_s[dr, n, pl.ds(r0, tc), :] = jnp.sqrt(jnp.tanh(nla) * (a * a + 1.0)) * (g * xs)
        return carry
    lax.fori_loop(0, nb, gates, 0, unroll=2)

    nl = wt // LRU_BW
    hf = [h_s[0, n] for n in range(nl)]
    hb = [h_s[1, n] for n in range(nl)]
    for t in range(tc):
        rows_f = pl.ds(t, nb, stride=LRU_PITCH)
        rows_b = pl.ds(tc - 1 - t, nb, stride=LRU_PITCH)
        for n in range(nl):
            hf[n] = a_s[0, n, rows_f, :] * hf[n] + b_s[0, n, rows_f, :]
            hb[n] = a_s[1, n, rows_b, :] * hb[n] + b_s[1, n, rows_b, :]
            o_s[0, n, rows_f, :] = hf[n]
            o_s[1, n, rows_b, :] = hb[n]
    for n in range(nl):
        h_s[0, n] = hf[n]
        h_s[1, n] = hb[n]

    def emit(bi, carry):
        r0 = pl.multiple_of(bi * LRU_PITCH, 8)
        for n in range(nl):
            sl = slice(n * LRU_BW, (n + 1) * LRU_BW)
            yf_ref[bi, :, sl] = o_s[0, n, pl.ds(r0, tc), :]
            yb_ref[bi, :, sl] = o_s[1, n, pl.ds(r0, tc), :]
        return carry
    lax.fori_loop(0, nb, emit, 0)


def _bwd_block(i, nctx, ntot):
    return jnp.where(i < nctx, nctx - 1 - i, ntot - 1 + nctx - i)


def _lru(u, wr, wi, br, bi, lam, *, lc):
    b, tt, _ = u.shape
    w = D_MODEL
    nt = tt // LRU_TC
    nctx = lc // LRU_TC
    c0 = C_XA // LRU_WT
    nblk = LRU_WT // LRU_BW
    y_spec_f = pl.BlockSpec((b, LRU_TC, LRU_WT), lambda j, i: (0, i, j))
    y_spec_b = pl.BlockSpec((b, LRU_TC, LRU_WT), lambda j, i: (0, _bwd_block(i, nctx, nt), j))
    return pl.pallas_call(
        _lru_kernel,
        grid=(w // LRU_WT, nt),
        in_specs=[
            pl.BlockSpec((b, LRU_TC, LRU_WT), lambda j, i: (0, i, c0 + j)),
            pl.BlockSpec((b, LRU_TC, LRU_WT), lambda j, i: (0, _bwd_block(i, nctx, nt), c0 + j)),
            pl.BlockSpec((2, nblk, LRU_BW, LRU_BW), lambda j, i: (0, j, 0, 0)),
            pl.BlockSpec((2, nblk, LRU_BW, LRU_BW), lambda j, i: (0, j, 0, 0)),
            pl.BlockSpec((2, LRU_WT), lambda j, i: (0, j)),
            pl.BlockSpec((2, LRU_WT), lambda j, i: (0, j)),
            pl.BlockSpec((2, LRU_WT), lambda j, i: (0, j)),
        ],
        out_specs=[y_spec_f, y_spec_b],
        out_shape=[jax.ShapeDtypeStruct((b, tt, w), F32)] * 2,
        scratch_shapes=[
            pltpu.VMEM((2, nblk, b * LRU_PITCH, LRU_BW), F32),
            pltpu.VMEM((2, nblk, b * LRU_PITCH, LRU_BW), F32),
            pltpu.VMEM((2, nblk, b * LRU_PITCH, LRU_BW), F32),
            pltpu.VMEM((2, nblk, b, LRU_BW), F32),
        ],
        compiler_params=pltpu.CompilerParams(
            dimension_semantics=("arbitrary", "arbitrary"), vmem_limit_bytes=VMEM_LIMIT),
        name="lru_scan",
    )(u, u, wr, wi, br, bi, lam)


SCAN_TB = 128
SM_W = 128
SM_DT = 0
SM_LOW = SSD_HEADS


def _split3(x):
    hi = x.astype(BF16).astype(F32)
    r = x - hi
    mid = r.astype(BF16).astype(F32)
    return hi, mid, r - mid


def _cumsum_rows(tri16, x):
    n = x.shape[1]
    p = _dot(tri16, jnp.concatenate([s.astype(BF16) for s in _split3(x)], axis=1))
    return (p[:, 0:n] + p[:, n:2 * n]) + p[:, 2 * n:3 * n]


def _pack3(x, lane):
    hi, mid, lo = _split3(x)
    h = SSD_HEADS
    return jnp.where(lane < h, hi,
                     jnp.where(lane < 2 * h, pltpu.roll(mid, h, 1),
                               jnp.where(lane < 3 * h, pltpu.roll(lo, 2 * h, 1), 0.0)))


def _gla_chunk(dr, qk, v, sm16, wup_ref, gb_ref, tri16, causal, gla_s, last):
    z = _dot(sm16, wup_ref[dr]) + gb_ref[dr:dr + 1, :]
    logg = -_softplus(-z) * (1.0 / GLA_TAU)
    bc = _cumsum_rows(tri16, logg)
    btot = bc[last:last + 1, :]
    qe = (qk[:, 0:GLA_DK] * (GLA_DKH ** -0.5)) * jnp.exp(bc)
    k = qk[:, GLA_DK:2 * GLA_DK]
    ke = k * jnp.exp(-bc)
    kd = k * jnp.exp(btot - bc)
    etot = jnp.exp(btot)
    outs = []
    for h in range(GLA_HEADS):
        ks = slice(h * GLA_DKH, (h + 1) * GLA_DKH)
        vs = slice(h * GLA_DVH, (h + 1) * GLA_DVH)
        qe16 = qe[:, ks].astype(BF16)
        v16 = v[:, vs].astype(BF16)
        att = jnp.where(causal, _dot_nt(qe16, ke[:, ks].astype(BF16)), 0.0)
        st = gla_s[dr, h]
        outs.append(_dot(att.astype(BF16), v16) + _dot_nt(qe16, st.astype(BF16)))
        gla_s[dr, h] = st * etot[:, ks] + _dot_tn(v16, kd[:, ks].astype(BF16))
    return jnp.concatenate(outs, axis=1)


def _ssd_chunk(dr, xbc, sm, e3_ref, dtb_ref, na_ref, tri16, causal_x, blk_mask, lane, ones_part, ssd_s, last):
    dt = _softplus(sm + dtb_ref[dr:dr + 1, :])
    cum = _cumsum_rows(tri16, dt * na_ref[dr:dr + 1, :])
    clast = cum[last:last + 1, :]
    packed = _pack3(jnp.concatenate([cum, dt, jnp.exp(cum), jnp.exp(clast - cum)], axis=0), lane)
    pt = packed[0:CHUNK].T
    pt2 = (-jnp.concatenate([pt, pt], axis=1)).astype(BF16)
    rm = jnp.concatenate([pt2] * (SSD_INNER // 128), axis=1) * e3_ref[...]
    a16 = jnp.concatenate([packed.astype(BF16), ones_part], axis=1)
    big = _dot(a16, jnp.concatenate([e3_ref[...], rm], axis=0))
    lmat = jnp.exp(jnp.where(causal_x, big[0:CHUNK], -jnp.inf))
    dt_x = big[CHUNK:2 * CHUNK]
    ecum_x = big[2 * CHUNK:3 * CHUNK]
    dec_x = big[3 * CHUNK:4 * CHUNK]
    xs = xbc[:, 0:SSD_INNER]
    xdt = xs * dt_x
    xdec16 = (xdt * dec_x).astype(BF16)
    xdt16 = xdt.astype(BF16)
    ys = []
    for g in range(SSD_G):
        gs = slice(g * SSD_GW, (g + 1) * SSD_GW)
        bm16 = xbc[:, SSD_INNER + g * SSD_N:SSD_INNER + (g + 1) * SSD_N].astype(BF16)
        cm16 = xbc[:, SSD_INNER + (SSD_G + g) * SSD_N:SSD_INNER + (SSD_G + g + 1) * SSD_N].astype(BF16)
        b_tiled = jnp.concatenate([bm16] * (SSD_GW // CHUNK), axis=0)
        m16 = (_dot_nt(cm16, b_tiled) * lmat[:, gs]).astype(BF16)
        parts = []
        for pr in range(SSD_GW // 128):
            ls = slice(g * SSD_GW + pr * 128, g * SSD_GW + (pr + 1) * 128)
            x2 = xdt16[:, ls]
            bd = jnp.where(blk_mask, jnp.concatenate([x2, x2], axis=0), jnp.zeros((), BF16))
            parts.append(_dot(m16[:, pr * 128:(pr + 1) * 128], bd))
        st = ssd_s[dr, g]
        ys.append(jnp.concatenate(parts, axis=1) + _dot(cm16, st.astype(BF16)) * ecum_x[:, gs])
        ssd_s[dr, g] = st * ecum_x[last:last + 1, gs] + _dot_tn(bm16, xdec16[:, gs])
    return jnp.concatenate(ys, axis=1)


def _scan_kernel(qkf_ref, vf_ref, xf_ref, sf_ref, qkb_ref, vb_ref, xb_ref, sb_ref,
                 wup_ref, gb_ref, e3_ref, dtb_ref, na_ref,
                 ybf_ref, ycf_ref, ybb_ref, ycb_ref, gla_s, ssd_s):
    i = pl.program_id(1)

    @pl.when(i == 0)
    def _():
        gla_s[...] = jnp.zeros_like(gla_s)
        ssd_s[...] = jnp.zeros_like(ssd_s)

    tb = qkf_ref.shape[1]
    nch = tb // CHUNK
    row = lax.broadcasted_iota(jnp.int32, (CHUNK, CHUNK), 0)
    col = lax.broadcasted_iota(jnp.int32, (CHUNK, CHUNK), 1)
    row_x = lax.broadcasted_iota(jnp.int32, (CHUNK, SSD_INNER), 0)
    col_x = lax.broadcasted_iota(jnp.int32, (CHUNK, SSD_INNER), 1) & (CHUNK - 1)
    r2 = lax.broadcasted_iota(jnp.int32, (2 * CHUNK, 128), 0)
    c2 = lax.broadcasted_iota(jnp.int32, (2 * CHUNK, 128), 1)
    blk_mask = (r2 < CHUNK) == (c2 < CHUNK)
    lane = lax.broadcasted_iota(jnp.int32, (4 * CHUNK, SM_W), 1)
    prow = lax.broadcasted_iota(jnp.int32, (4 * CHUNK, SM_W), 0)
    ones_part = jnp.where((prow < CHUNK) & (lane < 3 * SSD_HEADS), 1.0, 0.0).astype(BF16)

    dirs = (
        (0, qkf_ref, vf_ref, xf_ref, sf_ref, ybf_ref, ycf_ref),
        (1, qkb_ref, vb_ref, xb_ref, sb_ref, ybb_ref, ycb_ref),
    )
    for step in range(nch):
        for dr, qk_ref, v_ref, x_ref, s_ref, yb_ref, yc_ref in dirs:
            fwd = dr == 0
            c = step if fwd else nch - 1 - step
            rs = slice(c * CHUNK, (c + 1) * CHUNK)
            causal = (row >= col) if fwd else (row <= col)
            causal_x = (row_x >= col_x) if fwd else (row_x <= col_x)
            tri16 = jnp.where(causal, 1.0, 0.0).astype(BF16)
            last = CHUNK - 1 if fwd else 0
            sm = s_ref[0, rs, :]
            yb_ref[0, rs, :] = _gla_chunk(dr, qk_ref[0, rs, :], v_ref[0, rs, :], sm.astype(BF16),
                                          wup_ref, gb_ref, tri16, causal, gla_s, last)
            yc_ref[0, rs, :] = _ssd_chunk(dr, x_ref[0, rs, :], sm, e3_ref, dtb_ref, na_ref, tri16,
                                          causal_x, blk_mask, lane, ones_part, ssd_s, last)


def _scan(u, wup, gb, e3, dtb, na, *, lc):
    b, tt, _ = u.shape
    tb = SCAN_TB
    nt = tt // tb
    nctx = lc // tb

    def fmap(cb):
        return lambda bi, i: (bi, i, cb)

    def bmap(cb):
        return lambda bi, i: (bi, _bwd_block(i, nctx, nt), cb)

    def tok_specs(mk, dr):
        return [
            pl.BlockSpec((1, tb, 2 * GLA_DK), mk(C_Q // (2 * GLA_DK))),
            pl.BlockSpec((1, tb, GLA_DV), mk(C_V // GLA_DV)),
            pl.BlockSpec((1, tb, SSD_XBC), mk(C_XBC // SSD_XBC)),
            pl.BlockSpec((1, tb, SM_W), mk(C_SM // SM_W + dr)),
        ]

    def const_spec(a):
        nd = a.ndim
        return pl.BlockSpec(a.shape, lambda bi, i: (0,) * nd)

    consts = (wup, gb, e3, dtb, na)
    return pl.pallas_call(
        _scan_kernel,
        grid=(b, nt),
        in_specs=tok_specs(fmap, 0) + tok_specs(bmap, 1) + [const_spec(a) for a in consts],
        out_specs=[
            pl.BlockSpec((1, tb, GLA_DV), fmap(0)),
            pl.BlockSpec((1, tb, SSD_INNER), fmap(0)),
            pl.BlockSpec((1, tb, GLA_DV), bmap(0)),
            pl.BlockSpec((1, tb, SSD_INNER), bmap(0)),
        ],
        out_shape=[
            jax.ShapeDtypeStruct((b, tt, GLA_DV), F32),
            jax.ShapeDtypeStruct((b, tt, SSD_INNER), F32),
            jax.ShapeDtypeStruct((b, tt, GLA_DV), F32),
            jax.ShapeDtypeStruct((b, tt, SSD_INNER), F32),
        ],
        scratch_shapes=[
            pltpu.VMEM((2, GLA_HEADS, GLA_DVH, GLA_DKH), F32),
            pltpu.VMEM((2, SSD_G, SSD_N, SSD_GW), F32),
        ],
        compiler_params=pltpu.CompilerParams(
            dimension_semantics=("arbitrary", "arbitrary"), vmem_limit_bytes=VMEM_LIMIT),
        name="gla_ssd_scan",
    )(u, u, u, u, u, u, u, u, *consts)


FIN_TM = 128


def _group_rms(x, width):
    parts = []
    for s in range(x.shape[1] // width):
        xs = x[:, s * width:(s + 1) * width]
        parts.append(xs * lax.rsqrt(jnp.mean(xs * xs, axis=-1, keepdims=True) + EPS))
    return jnp.concatenate(parts, axis=1)


def _fin_kernel(yaf_ref, yab_ref, ybf_ref, ybb_ref, ycf_ref, ycb_ref, lg_ref, gg_ref, z_ref, mg_ref,
                xs_ref, xc_ref, xl_ref, ml_ref, mc_ref, gng_ref, sng_ref, sd_ref, pog_ref,
                wpa_ref, wpb_ref, wpc_ref, wo_ref, oc_ref, ol_ref, *, nctx):
    i = pl.program_id(1)
    d = D_MODEL
    ya = yaf_ref[0] + yab_ref[0]
    pa = _dot((ya * _silu(lg_ref[0])).astype(BF16), wpa_ref[...])
    yb = ybf_ref[0] + ybb_ref[0]
    ob = (_group_rms(yb, GLA_DVH) * gng_ref[...]) * _silu(gg_ref[0])
    pb = _dot(ob.astype(BF16), wpb_ref[...])
    yc = (ycf_ref[0] + ycb_ref[0]) + sd_ref[...] * xs_ref[0]
    oc = _group_rms(yc * _silu(z_ref[0]), SSD_GW) * sng_ref[...]
    pc = _dot(oc.astype(BF16), wpc_ref[...])
    gates = _sigmoid(mg_ref[0])
    merged = gates[:, 0:d] * pa + gates[:, d:2 * d] * pb + gates[:, 2 * d:3 * d] * pc
    out = _dot(merged.astype(BF16), wo_ref[...])
    normed = _group_rms(out, d) * pog_ref[...]

    @pl.when(i < nctx)
    def _():
        oc_ref[0] = xc_ref[0] + mc_ref[:, 2 * d:3 * d] * normed

    @pl.when(i >= nctx)
    def _():
        ol_ref[0] = xl_ref[0] + ml_ref[0][:, 2 * d:3 * d] * normed


def _finish(ya_f, ya_b, yb_f, yb_b, yc_f, yc_b, u, x_ctx, x_lat, mod_lat, mod_ctx,
            gng, sng, sd_x, post_g, w_pa, w_pb, w_pc, w_out):
    b, lc, d = x_ctx.shape
    seq = x_lat.shape[1]
    tm = FIN_TM
    nctx = lc // tm
    ctx_spec = pl.BlockSpec((1, tm, d), lambda bi, i: (bi, jnp.minimum(i, nctx - 1), 0))
    lat_spec = pl.BlockSpec((1, tm, d), lambda bi, i: (bi, jnp.maximum(i - nctx, 0), 0))

    def tok(width, cb=0):
        return pl.BlockSpec((1, tm, width), lambda bi, i: (bi, i, cb))

    def const_spec(a):
        nd = a.ndim
        return pl.BlockSpec(a.shape, lambda bi, i: (0,) * nd)

    consts = (gng, sng, sd_x, post_g, w_pa, w_pb, w_pc, w_out)
    return pl.pallas_call(
        functools.partial(_fin_kernel, nctx=nctx),
        grid=(b, (lc + seq) // tm),
        in_specs=[
            tok(d), tok(d), tok(GLA_DV), tok(GLA_DV), tok(SSD_INNER), tok(SSD_INNER),
            tok(d, C_LG // d), tok(GLA_DV, C_GG // GLA_DV), tok(SSD_INNER, C_Z // SSD_INNER),
            tok(3 * d, C_MRG // (3 * d)), tok(SSD_INNER, C_XBC // SSD_INNER), ctx_spec, lat_spec,
            pl.BlockSpec((1, 1, 3 * d), lambda bi, i: (bi, 0, 0)),
            pl.BlockSpec((1, 3 * d), lambda bi, i: (0, 0)),
        ] + [const_spec(a) for a in consts],
        out_specs=[ctx_spec, lat_spec],
        out_shape=[jax.ShapeDtypeStruct((b, lc, d), F32), jax.ShapeDtypeStruct((b, seq, d), F32)],
        compiler_params=pltpu.CompilerParams(
            dimension_semantics=("arbitrary", "arbitrary"), vmem_limit_bytes=VMEM_LIMIT),
        name="finish",
    )(ya_f, ya_b, yb_f, yb_b, yc_f, yc_b, u, u, u, u, u, x_ctx, x_lat, mod_lat, mod_ctx, *consts)


def _reorder_in_weights(w_in):
    d = w_in.shape[0]
    o = 0
    seg = {}
    for name, wd in (("xa", D_MODEL), ("lg", D_MODEL), ("q", GLA_DK), ("k", GLA_DK), ("v", GLA_DV),
                     ("gg", GLA_DV), ("low", 2 * GLA_RANK), ("z", SSD_INNER), ("xbc", SSD_XBC),
                     ("dt", 2 * SSD_HEADS), ("mrg", 3 * D_MODEL)):
        seg[name] = w_in[:, o:o + wd]
        o += wd
    pad = jnp.zeros((d, SM_W - GLA_RANK - SSD_HEADS), w_in.dtype)
    cols = [seg["xbc"], seg["mrg"], seg["z"], seg["xa"], seg["v"], seg["lg"], seg["gg"],
            seg["q"], seg["k"]]
    for dr in range(2):
        cols += [seg["dt"][:, dr * SSD_HEADS:(dr + 1) * SSD_HEADS],
                 seg["low"][:, dr * GLA_RANK:(dr + 1) * GLA_RANK], pad]
    cols.append(jnp.zeros((d, N_TOT - C_SM - 2 * SM_W), w_in.dtype))
    return jnp.concatenate(cols, axis=1).astype(BF16)


def _conv_tables(conv_a_w, conv_a_b, conv_c_w, conv_c_b):
    cw = jnp.zeros((CONV_W, N_TOT), F32)
    cw = cw.at[:, C_XBC:C_XBC + SSD_XBC].set(conv_c_w).at[:, C_XA:C_XA + D_MODEL].set(conv_a_w)
    cb = jnp.zeros((1, N_TOT), F32)
    cb = cb.at[0, C_XBC:C_XBC + SSD_XBC].set(conv_c_b).at[0, C_XA:C_XA + D_MODEL].set(conv_a_b)
    return cw, cb


def _scan_tables(gla_alpha_up, ssd_dt_bias, ssd_a_log):
    wup = jnp.zeros((2, SM_W, GLA_DK), F32).at[:, SM_LOW:SM_LOW + GLA_RANK, :].set(gla_alpha_up)
    dtb = jnp.zeros((2, SM_W), F32).at[:, SM_DT:SM_DT + SSD_HEADS].set(ssd_dt_bias)
    a_log = jnp.zeros((2, SM_W), F32).at[:, SM_DT:SM_DT + SSD_HEADS].set(ssd_a_log)
    valid = np.zeros((2, SM_W), np.float32)
    valid[:, SM_DT:SM_DT + SSD_HEADS] = 1.0
    e3 = np.zeros((SM_W, SSD_INNER), np.float32)
    for k in range(3 * SSD_HEADS):
        h = k % SSD_HEADS
        e3[k, h * SSD_P:(h + 1) * SSD_P] = 1.0
    return wup.astype(BF16), jnp.asarray(e3, BF16), dtb, a_log, jnp.asarray(valid)


def _to_col_major(h, rows):
    b, length, dm = h.shape
    return h.reshape(b, rows, GRID_W, dm).transpose(0, 2, 1, 3).reshape(b, length, dm)


def _from_col_major(h, rows):
    b, length, dm = h.shape
    return h.reshape(b, GRID_W, rows, dm).transpose(0, 2, 1, 3).reshape(b, length, dm)


def _neg_exp_kernel(a_ref, v_ref, o_ref):
    o_ref[...] = -jnp.exp(a_ref[...]) * v_ref[...]


def kernel(x, c, ctx, c_ctx, ada_w, ada_b, pre_g, post_g, w_in, conv_a_w, conv_a_b, lru_wr, lru_br,
           lru_wi, lru_bi, lru_lam, gla_alpha_up, gla_alpha_b, gla_norm_g, conv_c_w, conv_c_b,
           ssd_a_log, ssd_dt_bias, ssd_d, ssd_norm_g, w_pa, w_pb, w_pc, w_out):
    b, seq, d = x.shape
    lc = ctx.shape[1]
    depth = w_in.shape[0]
    rows = seq // GRID_W
    assert d == D_MODEL and lc % SCAN_TB == 0 and seq % SCAN_TB == 0 and b % 8 == 0

    pad_rows = (-(b + 1)) % 8
    cc = jnp.concatenate([c, c_ctx[None, :], jnp.zeros((pad_rows, d), F32)], axis=0)
    mods = _adaln(cc, ada_w, ada_b)

    x_lat, x_ctx = x, ctx
    for l in range(depth):
        col_major = l % 2 == 1
        line = rows if col_major else GRID_W
        mod_lat = mods[l, :b].reshape(b, 1, 3 * d)
        mod_ctx = mods[l, b:b + 1]
        lat = _to_col_major(x_lat, rows) if col_major else x_lat

        cw, cb = _conv_tables(conv_a_w[l], conv_a_b[l], conv_c_w[l], conv_c_b[l])
        u = _in_proj(x_ctx, lat, mod_lat, mod_ctx, pre_g[l].reshape(1, d), _reorder_in_weights(w_in[l]),
                     cw, cb, line=line)

        ya_f, ya_b = _lru(u, lru_wr[l].astype(BF16), lru_wi[l].astype(BF16), lru_br[l], lru_bi[l],
                          lru_lam[l], lc=lc)

        wup, e3, dtb, a_log, valid = _scan_tables(gla_alpha_up[l], ssd_dt_bias[l], ssd_a_log[l])
        na = pl.pallas_call(_neg_exp_kernel, out_shape=jax.ShapeDtypeStruct(a_log.shape, F32),
                            name="ssd_neg_a")(a_log, valid)
        yb_f, yc_f, yb_b, yc_b = _scan(u, wup, gla_alpha_b[l], e3, dtb, na, lc=lc)

        x_ctx, lat = _finish(ya_f, ya_b, yb_f, yb_b, yc_f, yc_b, u, x_ctx, lat, mod_lat, mod_ctx,
                             jnp.tile(gla_norm_g[l], GLA_HEADS).reshape(1, GLA_DV),
                             ssd_norm_g[l].reshape(1, SSD_INNER),
                             jnp.repeat(ssd_d[l], SSD_P).reshape(1, SSD_INNER),
                             post_g[l].reshape(1, d),
                             w_pa[l].astype(BF16), w_pb[l].astype(BF16), w_pc[l].astype(BF16),
                             w_out[l].astype(BF16))
        x_lat = _from_col_major(lat, rows) if col_major else lat
    return x_lat
```

```python
import functools

import jax
import jax.numpy as jnp
import numpy as np
from jax import lax
from jax.experimental import pallas as pl
from jax.experimental.pallas import tpu as pltpu

F32 = jnp.float32
BF16 = jnp.bfloat16

D_MODEL = 1024
GRID_W = 64
CONV_W = 4
EPS = 1e-6
LRU_BW = 128
LRU_C = 8.0
GLA_HEADS = 4
GLA_DK = 512
GLA_DV = 1024
GLA_DKH = 128
GLA_DVH = 256
GLA_RANK = 16
GLA_TAU = 16.0
SSD_INNER = 2048
SSD_P = 64
SSD_HEADS = 32
SSD_N = 128
SSD_G = 4
SSD_GW = SSD_INNER // SSD_G
SSD_XBC = SSD_INNER + 2 * SSD_G * SSD_N
CHUNK = 64

TN = 512
C_XBC, C_MRG, C_Z, C_XA, C_V, C_LG, C_GG, C_Q, C_K, C_SM = (
    0, 3072, 6144, 8192, 9216, 10240, 11264, 12288, 12800, 13312)
N_TOT = C_SM + TN

VMEM_LIMIT = 56 * 1024 * 1024


def _sigmoid(x):
    return 0.5 * jnp.tanh(0.5 * x) + 0.5


def _silu(x):
    return x * _sigmoid(x)


def _softplus(x):
    return jnp.maximum(x, 0.0) + jnp.log1p(jnp.exp(-jnp.abs(x)))


def _dot(a, b):
    return jnp.dot(a, b, preferred_element_type=F32)


def _dot_nt(a, b):
    return lax.dot_general(a, b, (((1,), (1,)), ((), ())), preferred_element_type=F32)


def _dot_tn(a, b):
    return lax.dot_general(a, b, (((0,), (0,)), ((), ())), preferred_element_type=F32)


def _adaln_kernel(c_ref, w_ref, b_ref, o_ref):
    c = c_ref[...]
    o_ref[0] = _dot(_silu(c).astype(BF16), w_ref[0]) + b_ref[0]


def _adaln(cc, ada_w, ada_b):
    depth, d, n3 = ada_w.shape
    rows = cc.shape[0]
    tn = 1024
    return pl.pallas_call(
        _adaln_kernel,
        grid=(depth, n3 // tn),
        in_specs=[
            pl.BlockSpec((rows, d), lambda l, n: (0, 0)),
            pl.BlockSpec((1, d, tn), lambda l, n: (l, 0, n)),
            pl.BlockSpec((1, 1, tn), lambda l, n: (l, 0, n)),
        ],
        out_specs=pl.BlockSpec((1, rows, tn), lambda l, n: (l, 0, n)),
        out_shape=jax.ShapeDtypeStruct((depth, rows, n3), F32),
        name="adaln",
    )(cc, ada_w.astype(BF16), ada_b.reshape(depth, 1, n3))


CONV_STRIP = 64
CONV_LANES = 256

def _in_kernel(xc_ref, xl_ref, ml_ref, mc_ref, pg_ref, w_ref, cw_ref, cb_ref, u_ref, h_s, *, line):
    j = pl.program_id(1)
    lc = xc_ref.shape[1]
    tt = lc + xl_ref.shape[1]
    d = xc_ref.shape[2]

    @pl.when(j == 0)
    def _():
        def norm_mod(x, mod):
            y = x * lax.rsqrt(jnp.mean(x * x, axis=-1, keepdims=True) + EPS) * pg_ref[...]
            return (y * (1.0 + mod[:, d:2 * d]) + mod[:, 0:d]).astype(BF16)
        h_s[0:lc, :] = norm_mod(xc_ref[0], mc_ref[...])
        h_s[lc:tt, :] = norm_mod(xl_ref[0], ml_ref[0])

    u_ref[0] = _dot(h_s[...], w_ref[...])

    def conv(act):
        def strip(r0, rows, ll):
            pos = lax.broadcasted_iota(jnp.int32, (rows, 1), 0) % ll
            for c0 in range(0, TN, CONV_LANES):
                cs = slice(c0, c0 + CONV_LANES)
                a = u_ref[0, pl.ds(r0, rows), cs]
                cw = cw_ref[:, cs]
                out = cb_ref[:, cs] + cw[1:2, :] * a
                out = out + cw[0:1, :] * jnp.where(pos >= 1, pltpu.roll(a, 1, 0), 0.0)
                out = out + cw[2:3, :] * jnp.where(pos + 1 < ll, pltpu.roll(a, rows - 1, 0), 0.0)
                out = out + cw[3:4, :] * jnp.where(pos + 2 < ll, pltpu.roll(a, rows - 2, 0), 0.0)
                u_ref[0, pl.ds(r0, rows), cs] = act(out)

        strip(0, lc, lc)
        rows = max(line, CONV_STRIP)

        def body(s, carry):
            strip(pl.multiple_of(lc + s * rows, rows), rows, line)
            return carry
        lax.fori_loop(0, (tt - lc) // rows, body, 0)

    @pl.when(j < SSD_XBC // TN)
    def _():
        conv(_silu)

    @pl.when((j >= C_XA // TN) & (j < (C_XA + D_MODEL) // TN))
    def _():
        conv(lambda v: v)


def _in_proj(x_ctx, x_lat, mod_lat, mod_ctx, pre_g, w_in_p, cw, cb, *, line):
    b, lc, d = x_ctx.shape
    seq = x_lat.shape[1]
    tt = lc + seq
    assert max(line, CONV_STRIP) % line == 0 and seq % max(line, CONV_STRIP) == 0
    return pl.pallas_call(
        functools.partial(_in_kernel, line=line),
        grid=(b, N_TOT // TN),
        in_specs=[
            pl.BlockSpec((1, lc, d), lambda i, j: (i, 0, 0)),
            pl.BlockSpec((1, seq, d), lambda i, j: (i, 0, 0)),
            pl.BlockSpec((1, 1, 3 * d), lambda i, j: (i, 0, 0)),
            pl.BlockSpec((1, 3 * d), lambda i, j: (0, 0)),
            pl.BlockSpec((1, d), lambda i, j: (0, 0)),
            pl.BlockSpec((d, TN), lambda i, j: (0, j)),
            pl.BlockSpec((CONV_W, TN), lambda i, j: (0, j)),
            pl.BlockSpec((1, TN), lambda i, j: (0, j)),
        ],
        out_specs=pl.BlockSpec((1, tt, TN), lambda i, j: (i, 0, j)),
        out_shape=jax.ShapeDtypeStruct((b, tt, N_TOT), F32),
        scratch_shapes=[pltpu.VMEM((tt, d), BF16)],
        compiler_params=pltpu.CompilerParams(
            dimension_semantics=("arbitrary", "arbitrary"), vmem_limit_bytes=VMEM_LIMIT),
        name="in_proj",
    )(x_ctx, x_lat, mod_lat, mod_ctx, pre_g, w_in_p, cw, cb)


LRU_TC = 64
LRU_WT = 512
LRU_PITCH = LRU_TC + 8


def _lru_kernel(xf_ref, xb_ref, wr_ref, wi_ref, br_ref, bi_ref, lam_ref, yf_ref, yb_ref,
                a_s, b_s, o_s, h_s):
    i = pl.program_id(1)
    nb, tc, wt = xf_ref.shape

    @pl.when(i == 0)
    def _():
        h_s[...] = jnp.zeros_like(h_s)

    def gates(bi, carry):
        r0 = pl.multiple_of(bi * LRU_PITCH, 8)
        for dr, x_ref in enumerate((xf_ref, xb_ref)):
            for n in range(wt // LRU_BW):
                sl = slice(n * LRU_BW, (n + 1) * LRU_BW)
                xs = x_ref[bi, :, sl]
                xs16 = xs.astype(BF16)
                r = _sigmoid(_dot(xs16, wr_ref[dr, n]) + br_ref[dr:dr + 1, sl])
                g = _sigmoid(_dot(xs16, wi_ref[dr, n]) + bi_ref[dr:dr + 1, sl])
                nla = (LRU_C * _softplus(-lam_ref[dr:dr + 1, sl])) * r
                a = jnp.exp(-nla)
                a_s[dr, n, pl.ds(r0, tc), :] = a
                b_s[dr, n, pl.ds(r0, tc), :] = jnp.sqrt(jnp.tanh(nla) * (a * a + 1.0)) * (g * xs)
        return carry
    lax.fori_loop(0, nb, gates, 0, unroll=2)

    nl = wt // LRU_BW
    hf = [h_s[0, n] for n in range(nl)]
    hb = [h_s[1, n] for n in range(nl)]
    for t in range(tc):
        rows_f = pl.ds(t, nb, stride=LRU_PITCH)
        rows_b = pl.ds(tc - 1 - t, nb, stride=LRU_PITCH)
        for n in range(nl):
            hf[n] = a_s[0, n, rows_f, :] * hf[n] + b_s[0, n, rows_f, :]
            hb[n] = a_s[1, n, rows_b, :] * hb[n] + b_s[1, n, rows_b, :]
            o_s[0, n, rows_f, :] = hf[n]
            o_s[1, n, rows_b, :] = hb[n]
    for n in range(nl):
        h_s[0, n] = hf[n]
        h_s[1, n] = hb[n]

    def emit(bi, carry):
        r0 = pl.multiple_of(bi * LRU_PITCH, 8)
        for n in range(nl):
            sl = slice(n * LRU_BW, (n + 1) * LRU_BW)
            yf_ref[bi, :, sl] = o_s[0, n, pl.ds(r0, tc), :]
            yb_ref[bi, :, sl] = o_s[1, n, pl.ds(r0, tc), :]
        return carry
    lax.fori_loop(0, nb, emit, 0)


def _bwd_block(i, nctx, ntot):
    return jnp.where(i < nctx, nctx - 1 - i, ntot - 1 + nctx - i)


def _lru(u, wr, wi, br, bi, lam, *, lc):
    b, tt, _ = u.shape
    w = D_MODEL
    nt = tt // LRU_TC
    nctx = lc // LRU_TC
    c0 = C_XA // LRU_WT
    nblk = LRU_WT // LRU_BW
    y_spec_f = pl.BlockSpec((b, LRU_TC, LRU_WT), lambda j, i: (0, i, j))
    y_spec_b = pl.BlockSpec((b, LRU_TC, LRU_WT), lambda j, i: (0, _bwd_block(i, nctx, nt), j))
    return pl.pallas_call(
        _lru_kernel,
        grid=(w // LRU_WT, nt),
        in_specs=[
            pl.BlockSpec((b, LRU_TC, LRU_WT), lambda j, i: (0, i, c0 + j)),
            pl.BlockSpec((b, LRU_TC, LRU_WT), lambda j, i: (0, _bwd_block(i, nctx, nt), c0 + j)),
            pl.BlockSpec((2, nblk, LRU_BW, LRU_BW), lambda j, i: (0, j, 0, 0)),
            pl.BlockSpec((2, nblk, LRU_BW, LRU_BW), lambda j, i: (0, j, 0, 0)),
            pl.BlockSpec((2, LRU_WT), lambda j, i: (0, j)),
            pl.BlockSpec((2, LRU_WT), lambda j, i: (0, j)),
            pl.BlockSpec((2, LRU_WT), lambda j, i: (0, j)),
        ],
        out_specs=[y_spec_f, y_spec_b],
        out_shape=[jax.ShapeDtypeStruct((b, tt, w), F32)] * 2,
        scratch_shapes=[
            pltpu.VMEM((2, nblk, b * LRU_PITCH, LRU_BW), F32),
            pltpu.VMEM((2, nblk, b * LRU_PITCH, LRU_BW), F32),
            pltpu.VMEM((2, nblk, b * LRU_PITCH, LRU_BW), F32),
            pltpu.VMEM((2, nblk, b, LRU_BW), F32),
        ],
        compiler_params=pltpu.CompilerParams(
            dimension_semantics=("arbitrary", "arbitrary"), vmem_limit_bytes=VMEM_LIMIT),
        name="lru_scan",
    )(u, u, wr, wi, br, bi, lam)


SCAN_TB = 256
SM_W = 128
SM_DT = 0
SM_LOW = SSD_HEADS
BD_HEADS = 4
BD_W = BD_HEADS * SSD_P


def _split3(x):
    hi = x.astype(BF16).astype(F32)
    r = x - hi
    mid = r.astype(BF16).astype(F32)
    return hi, mid, r - mid


def _cumsum_rows(tri16, x):
    n = x.shape[1]
    p = _dot(tri16, jnp.concatenate([s.astype(BF16) for s in _split3(x)], axis=1))
    return (p[:, 0:n] + p[:, n:2 * n]) + p[:, 2 * n:3 * n]


def _pack3(x, lane):
    hi, mid, lo = _split3(x)
    h = SSD_HEADS
    return jnp.where(lane < h, hi,
                     jnp.where(lane < 2 * h, pltpu.roll(mid, h, 1),
                               jnp.where(lane < 3 * h, pltpu.roll(lo, 2 * h, 1), 0.0)))


def _scan_kernel(qkf_ref, vf_ref, xf_ref, sf_ref, qkb_ref, vb_ref, xb_ref, sb_ref,
                 wup_ref, gb_ref, e3_ref, dtb_ref, na_ref,
                 ybf_ref, ycf_ref, ybb_ref, ycb_ref, gla_s, ssd_s):
    i = pl.program_id(1)

    @pl.when(i == 0)
    def _():
        gla_s[...] = jnp.zeros_like(gla_s)
        ssd_s[...] = jnp.zeros_like(ssd_s)

    tb = qkf_ref.shape[1]
    nch = tb // CHUNK
    row = lax.broadcasted_iota(jnp.int32, (CHUNK, CHUNK), 0)
    col = lax.broadcasted_iota(jnp.int32, (CHUNK, CHUNK), 1)
    row_x = lax.broadcasted_iota(jnp.int32, (CHUNK, SSD_INNER), 0)
    col_x = lax.broadcasted_iota(jnp.int32, (CHUNK, SSD_INNER), 1) & (CHUNK - 1)
    r2 = lax.broadcasted_iota(jnp.int32, (BD_W, BD_W), 0)
    c2 = lax.broadcasted_iota(jnp.int32, (BD_W, BD_W), 1)
    blk_mask = (r2 // CHUNK) == (c2 // SSD_P)
    lane = lax.broadcasted_iota(jnp.int32, (4 * CHUNK, SM_W), 1)
    prow = lax.broadcasted_iota(jnp.int32, (4 * CHUNK, SM_W), 0)
    ones_part = jnp.where((prow < CHUNK) & (lane < 3 * SSD_HEADS), 1.0, 0.0).astype(BF16)
    heads = range(GLA_HEADS)
    groups = range(SSD_G)

    def ksl(h):
        return slice(h * GLA_DKH, (h + 1) * GLA_DKH)

    def vsl(h):
        return slice(h * GLA_DVH, (h + 1) * GLA_DVH)

    def gsl(g):
        return slice(g * SSD_GW, (g + 1) * SSD_GW)

    dirs = (
        (0, qkf_ref, vf_ref, xf_ref, sf_ref, ybf_ref, ycf_ref),
        (1, qkb_ref, vb_ref, xb_ref, sb_ref, ybb_ref, ycb_ref),
    )
    for step in range(nch):
        insts = []
        for dr, qk_ref, v_ref, x_ref, s_ref, yb_ref, yc_ref in dirs:
            fwd = dr == 0
            c = step if fwd else nch - 1 - step
            causal = (row >= col) if fwd else (row <= col)
            insts.append(dict(
                dr=dr, rs=slice(c * CHUNK, (c + 1) * CHUNK), last=CHUNK - 1 if fwd else 0,
                causal=causal, causal_x=(row_x >= col_x) if fwd else (row_x <= col_x),
                tri16=jnp.where(causal, 1.0, 0.0).astype(BF16),
                qk_ref=qk_ref, v_ref=v_ref, x_ref=x_ref, s_ref=s_ref, yb_ref=yb_ref, yc_ref=yc_ref))

        for it in insts:
            dr, rs, x_ref = it["dr"], it["rs"], it["x_ref"]
            it["sm"] = it["s_ref"][0, rs, :]
            it["z"] = _dot(it["sm"].astype(BF16), wup_ref[dr])
            it["bm16"] = [x_ref[0, rs, SSD_INNER + g * SSD_N:SSD_INNER + (g + 1) * SSD_N].astype(BF16)
                          for g in groups]
            it["cm16"] = [x_ref[0, rs, SSD_INNER + (SSD_G + g) * SSD_N:
                                SSD_INNER + (SSD_G + g + 1) * SSD_N].astype(BF16) for g in groups]
            it["cb"] = [_dot_nt(it["cm16"][g], jnp.concatenate([it["bm16"][g]] * (SSD_GW // CHUNK), axis=0))
                        for g in groups]

        for it in insts:
            dr, rs, last = it["dr"], it["rs"], it["last"]
            logg = -_softplus(-(it["z"] + gb_ref[dr:dr + 1, :])) * (1.0 / GLA_TAU)
            bc = _cumsum_rows(it["tri16"], logg)
            btot = bc[last:last + 1, :]
            qk = it["qk_ref"][0, rs, :]
            k = qk[:, GLA_DK:2 * GLA_DK]
            it["qe16"] = ((qk[:, 0:GLA_DK] * (GLA_DKH ** -0.5)) * jnp.exp(bc)).astype(BF16)
            it["ke16"] = (k * jnp.exp(-bc)).astype(BF16)
            it["kd16"] = (k * jnp.exp(btot - bc)).astype(BF16)
            it["etot"] = jnp.exp(btot)
            it["v16"] = it["v_ref"][0, rs, :].astype(BF16)

            dt = _softplus(it["sm"] + dtb_ref[dr:dr + 1, :])
            cum = _cumsum_rows(it["tri16"], dt * na_ref[dr:dr + 1, :])
            clast = cum[last:last + 1, :]
            packed = _pack3(jnp.concatenate([cum, dt, jnp.exp(cum), jnp.exp(clast - cum)], axis=0), lane)
            pt = packed[0:CHUNK].T
            pt2 = (-jnp.concatenate([pt, pt], axis=1)).astype(BF16)
            rm = jnp.concatenate([pt2] * (SSD_INNER // 128), axis=1) * e3_ref[...]
            it["a16"] = jnp.concatenate([packed.astype(BF16), ones_part], axis=1)
            it["rhs"] = jnp.concatenate([e3_ref[...], rm], axis=0)

        for it in insts:
            it["big"] = _dot(it["a16"], it["rhs"])
            it["att"] = [_dot_nt(it["qe16"][:, ksl(h)], it["ke16"][:, ksl(h)]) for h in heads]

        for it in insts:
            big = it["big"]
            lmat = jnp.exp(jnp.where(it["causal_x"], big[0:CHUNK], -jnp.inf))
            it["m16"] = [(it["cb"][g] * lmat[:, gsl(g)]).astype(BF16) for g in groups]
            it["ecum_x"] = big[2 * CHUNK:3 * CHUNK]
            xdt = it["x_ref"][0, it["rs"], 0:SSD_INNER] * big[CHUNK:2 * CHUNK]
            it["xdt16"] = xdt.astype(BF16)
            it["xdec16"] = (xdt * big[3 * CHUNK:4 * CHUNK]).astype(BF16)
            it["att16"] = [jnp.where(it["causal"], a, 0.0).astype(BF16) for a in it["att"]]

        for it in insts:
            dr = it["dr"]
            it["o_in"] = [_dot(it["att16"][h], it["v16"][:, vsl(h)]) for h in heads]
            it["o_st"] = [_dot_nt(it["qe16"][:, ksl(h)], gla_s[dr, h].astype(BF16)) for h in heads]
            it["u_gla"] = [_dot_tn(it["v16"][:, vsl(h)], it["kd16"][:, ksl(h)]) for h in heads]
            y_in = []
            for g in groups:
                parts = []
                for pr in range(SSD_GW // BD_W):
                    ls = slice(g * SSD_GW + pr * BD_W, g * SSD_GW + (pr + 1) * BD_W)
                    xh = it["xdt16"][:, ls]
                    bd = jnp.where(blk_mask, jnp.concatenate([xh] * BD_HEADS, axis=0), jnp.zeros((), BF16))
                    parts.append(_dot(it["m16"][g][:, pr * BD_W:(pr + 1) * BD_W], bd))
                y_in.append(jnp.concatenate(parts, axis=1))
            it["y_in"] = y_in
            it["y_st"] = [_dot(it["cm16"][g], ssd_s[dr, g].astype(BF16)) for g in groups]
            it["u_ssd"] = [_dot_tn(it["bm16"][g], it["xdec16"][:, gsl(g)]) for g in groups]

        for it in insts:
            dr, rs, last = it["dr"], it["rs"], it["last"]
            for h in heads:
                it["yb_ref"][0, rs, vsl(h)] = it["o_in"][h] + it["o_st"][h]
                gla_s[dr, h] = gla_s[dr, h] * it["etot"][:, ksl(h)] + it["u_gla"][h]
            for g in groups:
                ecum = it["ecum_x"][:, gsl(g)]
                it["yc_ref"][0, rs, gsl(g)] = it["y_in"][g] + it["y_st"][g] * ecum
                ssd_s[dr, g] = ssd_s[dr, g] * ecum[last:last + 1, :] + it["u_ssd"][g]


def _scan(u, wup, gb, e3, dtb, na, *, lc):
    b, tt, _ = u.shape
    tb = SCAN_TB
    nt = tt // tb
    nctx = lc // tb

    def fmap(cb):
        return lambda bi, i: (bi, i, cb)

    def bmap(cb):
        return lambda bi, i: (bi, _bwd_block(i, nctx, nt), cb)

    def tok_specs(mk, dr):
        return [
            pl.BlockSpec((1, tb, 2 * GLA_DK), mk(C_Q // (2 * GLA_DK))),
            pl.BlockSpec((1, tb, GLA_DV), mk(C_V // GLA_DV)),
            pl.BlockSpec((1, tb, SSD_XBC), mk(C_XBC // SSD_XBC)),
            pl.BlockSpec((1, tb, SM_W), mk(C_SM // SM_W + dr)),
        ]

    def const_spec(a):
        nd = a.ndim
        return pl.BlockSpec(a.shape, lambda bi, i: (0,) * nd)

    consts = (wup, gb, e3, dtb, na)
    return pl.pallas_call(
        _scan_kernel,
        grid=(b, nt),
        in_specs=tok_specs(fmap, 0) + tok_specs(bmap, 1) + [const_spec(a) for a in consts],
        out_specs=[
            pl.BlockSpec((1, tb, GLA_DV), fmap(0)),
            pl.BlockSpec((1, tb, SSD_INNER), fmap(0)),
            pl.BlockSpec((1, tb, GLA_DV), bmap(0)),
            pl.BlockSpec((1, tb, SSD_INNER), bmap(0)),
        ],
        out_shape=[
            jax.ShapeDtypeStruct((b, tt, GLA_DV), F32),
            jax.ShapeDtypeStruct((b, tt, SSD_INNER), F32),
            jax.ShapeDtypeStruct((b, tt, GLA_DV), F32),
            jax.ShapeDtypeStruct((b, tt, SSD_INNER), F32),
        ],
        scratch_shapes=[
            pltpu.VMEM((2, GLA_HEADS, GLA_DVH, GLA_DKH), F32),
            pltpu.VMEM((2, SSD_G, SSD_N, SSD_GW), F32),
        ],
        compiler_params=pltpu.CompilerParams(
            dimension_semantics=("arbitrary", "arbitrary"), vmem_limit_bytes=VMEM_LIMIT),
        name="gla_ssd_scan",
    )(u, u, u, u, u, u, u, u, *consts)


FIN_TM = 128


def _group_rms(x, width):
    parts = []
    for s in range(x.shape[1] // width):
        xs = x[:, s * width:(s + 1) * width]
        parts.append(xs * lax.rsqrt(jnp.mean(xs * xs, axis=-1, keepdims=True) + EPS))
    return jnp.concatenate(parts, axis=1)


def _fin_kernel(yaf_ref, yab_ref, ybf_ref, ybb_ref, ycf_ref, ycb_ref, lg_ref, gg_ref, z_ref, mg_ref,
                xs_ref, xc_ref, xl_ref, ml_ref, mc_ref, gng_ref, sng_ref, sd_ref, pog_ref,
                wpa_ref, wpb_ref, wpc_ref, wo_ref, oc_ref, ol_ref, *, nctx):
    i = pl.program_id(1)
    d = D_MODEL
    ya = yaf_ref[0] + yab_ref[0]
    pa = _dot((ya * _silu(lg_ref[0])).astype(BF16), wpa_ref[...])
    yb = ybf_ref[0] + ybb_ref[0]
    ob = (_group_rms(yb, GLA_DVH) * gng_ref[...]) * _silu(gg_ref[0])
    pb = _dot(ob.astype(BF16), wpb_ref[...])
    yc = (ycf_ref[0] + ycb_ref[0]) + sd_ref[...] * xs_ref[0]
    oc = _group_rms(yc * _silu(z_ref[0]), SSD_GW) * sng_ref[...]
    pc = _dot(oc.astype(BF16), wpc_ref[...])
    gates = _sigmoid(mg_ref[0])
    merged = gates[:, 0:d] * pa + gates[:, d:2 * d] * pb + gates[:, 2 * d:3 * d] * pc
    out = _dot(merged.astype(BF16), wo_ref[...])
    normed = _group_rms(out, d) * pog_ref[...]

    @pl.when(i < nctx)
    def _():
        oc_ref[0] = xc_ref[0] + mc_ref[:, 2 * d:3 * d] * normed

    @pl.when(i >= nctx)
    def _():
        ol_ref[0] = xl_ref[0] + ml_ref[0][:, 2 * d:3 * d] * normed


def _finish(ya_f, ya_b, yb_f, yb_b, yc_f, yc_b, u, x_ctx, x_lat, mod_lat, mod_ctx,
            gng, sng, sd_x, post_g, w_pa, w_pb, w_pc, w_out):
    b, lc, d = x_ctx.shape
    seq = x_lat.shape[1]
    tm = FIN_TM
    nctx = lc // tm
    ctx_spec = pl.BlockSpec((1, tm, d), lambda bi, i: (bi, jnp.minimum(i, nctx - 1), 0))
    lat_spec = pl.BlockSpec((1, tm, d), lambda bi, i: (bi, jnp.maximum(i - nctx, 0), 0))

    def tok(width, cb=0):
        return pl.BlockSpec((1, tm, width), lambda bi, i: (bi, i, cb))

    def const_spec(a):
        nd = a.ndim
        return pl.BlockSpec(a.shape, lambda bi, i: (0,) * nd)

    consts = (gng, sng, sd_x, post_g, w_pa, w_pb, w_pc, w_out)
    return pl.pallas_call(
        functools.partial(_fin_kernel, nctx=nctx),
        grid=(b, (lc + seq) // tm),
        in_specs=[
            tok(d), tok(d), tok(GLA_DV), tok(GLA_DV), tok(SSD_INNER), tok(SSD_INNER),
            tok(d, C_LG // d), tok(GLA_DV, C_GG // GLA_DV), tok(SSD_INNER, C_Z // SSD_INNER),
            tok(3 * d, C_MRG // (3 * d)), tok(SSD_INNER, C_XBC // SSD_INNER), ctx_spec, lat_spec,
            pl.BlockSpec((1, 1, 3 * d), lambda bi, i: (bi, 0, 0)),
            pl.BlockSpec((1, 3 * d), lambda bi, i: (0, 0)),
        ] + [const_spec(a) for a in consts],
        out_specs=[ctx_spec, lat_spec],
        out_shape=[jax.ShapeDtypeStruct((b, lc, d), F32), jax.ShapeDtypeStruct((b, seq, d), F32)],
        compiler_params=pltpu.CompilerParams(
            dimension_semantics=("arbitrary", "arbitrary"), vmem_limit_bytes=VMEM_LIMIT),
        name="finish",
    )(ya_f, ya_b, yb_f, yb_b, yc_f, yc_b, u, u, u, u, u, x_ctx, x_lat, mod_lat, mod_ctx, *consts)


def _reorder_in_weights(w_in):
    d = w_in.shape[0]
    o = 0
    seg = {}
    for name, wd in (("xa", D_MODEL), ("lg", D_MODEL), ("q", GLA_DK), ("k", GLA_DK), ("v", GLA_DV),
                     ("gg", GLA_DV), ("low", 2 * GLA_RANK), ("z", SSD_INNER), ("xbc", SSD_XBC),
                     ("dt", 2 * SSD_HEADS), ("mrg", 3 * D_MODEL)):
        seg[name] = w_in[:, o:o + wd]
        o += wd
    pad = jnp.zeros((d, SM_W - GLA_RANK - SSD_HEADS), w_in.dtype)
    cols = [seg["xbc"], seg["mrg"], seg["z"], seg["xa"], seg["v"], seg["lg"], seg["gg"],
            seg["q"], seg["k"]]
    for dr in range(2):
        cols += [seg["dt"][:, dr * SSD_HEADS:(dr + 1) * SSD_HEADS],
                 seg["low"][:, dr * GLA_RANK:(dr + 1) * GLA_RANK], pad]
    cols.append(jnp.zeros((d, N_TOT - C_SM - 2 * SM_W), w_in.dtype))
    return jnp.concatenate(cols, axis=1).astype(BF16)


def _conv_tables(conv_a_w, conv_a_b, conv_c_w, conv_c_b):
    cw = jnp.zeros((CONV_W, N_TOT), F32)
    cw = cw.at[:, C_XBC:C_XBC + SSD_XBC].set(conv_c_w).at[:, C_XA:C_XA + D_MODEL].set(conv_a_w)
    cb = jnp.zeros((1, N_TOT), F32)
    cb = cb.at[0, C_XBC:C_XBC + SSD_XBC].set(conv_c_b).at[0, C_XA:C_XA + D_MODEL].set(conv_a_b)
    return cw, cb


def _scan_tables(gla_alpha_up, ssd_dt_bias, ssd_a_log):
    wup = jnp.zeros((2, SM_W, GLA_DK), F32).at[:, SM_LOW:SM_LOW + GLA_RANK, :].set(gla_alpha_up)
    dtb = jnp.zeros((2, SM_W), F32).at[:, SM_DT:SM_DT + SSD_HEADS].set(ssd_dt_bias)
    a_log = jnp.zeros((2, SM_W), F32).at[:, SM_DT:SM_DT + SSD_HEADS].set(ssd_a_log)
    valid = np.zeros((2, SM_W), np.float32)
    valid[:, SM_DT:SM_DT + SSD_HEADS] = 1.0
    e3 = np.zeros((SM_W, SSD_INNER), np.float32)
    for k in range(3 * SSD_HEADS):
        h = k % SSD_HEADS
        e3[k, h * SSD_P:(h + 1) * SSD_P] = 1.0
    return wup.astype(BF16), jnp.asarray(e3, BF16), dtb, a_log, jnp.asarray(valid)


def _to_col_major(h, rows):
    b, length, dm = h.shape
    return h.reshape(b, rows, GRID_W, dm).transpose(0, 2, 1, 3).reshape(b, length, dm)


def _from_col_major(h, rows):
    b, length, dm = h.shape
    return h.reshape(b, GRID_W, rows, dm).transpose(0, 2, 1, 3).reshape(b, length, dm)


def _neg_exp_kernel(a_ref, v_ref, o_ref):
    o_ref[...] = -jnp.exp(a_ref[...]) * v_ref[...]


def kernel(x, c, ctx, c_ctx, ada_w, ada_b, pre_g, post_g, w_in, conv_a_w, conv_a_b, lru_wr, lru_br,
           lru_wi, lru_bi, lru_lam, gla_alpha_up, gla_alpha_b, gla_norm_g, conv_c_w, conv_c_b,
           ssd_a_log, ssd_dt_bias, ssd_d, ssd_norm_g, w_pa, w_pb, w_pc, w_out):
    b, seq, d = x.shape
    lc = ctx.shape[1]
    depth = w_in.shape[0]
    rows = seq // GRID_W
    assert d == D_MODEL and lc % SCAN_TB == 0 and seq % SCAN_TB == 0 and b % 8 == 0

    pad_rows = (-(b + 1)) % 8
    cc = jnp.concatenate([c, c_ctx[None, :], jnp.zeros((pad_rows, d), F32)], axis=0)
    mods = _adaln(cc, ada_w, ada_b)

    x_lat, x_ctx = x, ctx
    for l in range(depth):
        col_major = l % 2 == 1
        line = rows if col_major else GRID_W
        mod_lat = mods[l, :b].reshape(b, 1, 3 * d)
        mod_ctx = mods[l, b:b + 1]
        lat = _to_col_major(x_lat, rows) if col_major else x_lat

        cw, cb = _conv_tables(conv_a_w[l], conv_a_b[l], conv_c_w[l], conv_c_b[l])
        u = _in_proj(x_ctx, lat, mod_lat, mod_ctx, pre_g[l].reshape(1, d), _reorder_in_weights(w_in[l]),
                     cw, cb, line=line)

        ya_f, ya_b = _lru(u, lru_wr[l].astype(BF16), lru_wi[l].astype(BF16), lru_br[l], lru_bi[l],
                          lru_lam[l], lc=lc)

        wup, e3, dtb, a_log, valid = _scan_tables(gla_alpha_up[l], ssd_dt_bias[l], ssd_a_log[l])
        na = pl.pallas_call(_neg_exp_kernel, out_shape=jax.ShapeDtypeStruct(a_log.shape, F32),
                            name="ssd_neg_a")(a_log, valid)
        yb_f, yc_f, yb_b, yc_b = _scan(u, wup, gla_alpha_b[l], e3, dtb, na, lc=lc)

        x_ctx, lat = _finish(ya_f, ya_b, yb_f, yb_b, yc_f, yc_b, u, x_ctx, lat, mod_lat, mod_ctx,
                             jnp.tile(gla_norm_g[l], GLA_HEADS).reshape(1, GLA_DV),
                             ssd_norm_g[l].reshape(1, SSD_INNER),
                             jnp.repeat(ssd_d[l], SSD_P).reshape(1, SSD_INNER),
                             post_g[l].reshape(1, d),
                             w_pa[l].astype(BF16), w_pb[l].astype(BF16), w_pc[l].astype(BF16),
                             w_out[l].astype(BF16))
        x_lat = _from_col_major(lat, rows) if col_major else lat
    return x_lat
```

```python
import functools

import jax
import jax.numpy as jnp
import numpy as np
from jax import lax
from jax.experimental import pallas as pl
from jax.experimental.pallas import tpu as pltpu

F32 = jnp.float32
BF16 = jnp.bfloat16

D_MODEL = 1024
GRID_W = 64
CONV_W = 4
EPS = 1e-6
LRU_BW = 128
LRU_C = 8.0
GLA_HEADS = 4
GLA_DK = 512
GLA_DV = 1024
GLA_DKH = 128
GLA_DVH = 256
GLA_RANK = 16
GLA_TAU = 16.0
SSD_INNER = 2048
SSD_P = 64
SSD_HEADS = 32
SSD_N = 128
SSD_G = 4
SSD_GW = SSD_INNER // SSD_G
SSD_XBC = SSD_INNER + 2 * SSD_G * SSD_N
CHUNK = 64

TN = 512
C_XBC, C_MRG, C_Z, C_XA, C_V, C_LG, C_GG, C_Q, C_K, C_SM = (
    0, 3072, 6144, 8192, 9216, 10240, 11264, 12288, 12800, 13312)
N_TOT = C_SM + TN

VMEM_LIMIT = 56 * 1024 * 1024


def _sigmoid(x):
    return 0.5 * jnp.tanh(0.5 * x) + 0.5


def _silu(x):
    return x * _sigmoid(x)


def _softplus(x):
    return jnp.maximum(x, 0.0) + jnp.log1p(jnp.exp(-jnp.abs(x)))


def _dot(a, b):
    return jnp.dot(a, b, preferred_element_type=F32)


def _dot_nt(a, b):
    return lax.dot_general(a, b, (((1,), (1,)), ((), ())), preferred_element_type=F32)


def _dot_tn(a, b):
    return lax.dot_general(a, b, (((0,), (0,)), ((), ())), preferred_element_type=F32)


def _adaln_kernel(c_ref, w_ref, b_ref, o_ref):
    c = c_ref[...]
    o_ref[0] = _dot(_silu(c).astype(BF16), w_ref[0]) + b_ref[0]


def _adaln(cc, ada_w, ada_b):
    depth, d, n3 = ada_w.shape
    rows = cc.shape[0]
    tn = 1024
    return pl.pallas_call(
        _adaln_kernel,
        grid=(depth, n3 // tn),
        in_specs=[
            pl.BlockSpec((rows, d), lambda l, n: (0, 0)),
            pl.BlockSpec((1, d, tn), lambda l, n: (l, 0, n)),
            pl.BlockSpec((1, 1, tn), lambda l, n: (l, 0, n)),
        ],
        out_specs=pl.BlockSpec((1, rows, tn), lambda l, n: (l, 0, n)),
        out_shape=jax.ShapeDtypeStruct((depth, rows, n3), F32),
        name="adaln",
    )(cc, ada_w.astype(BF16), ada_b.reshape(depth, 1, n3))


CONV_PAD = 8
CONV_SLABS = TN // 128

def _in_kernel(xc_ref, xl_ref, ml_ref, mc_ref, pg_ref, w_ref, cw_ref, cb_ref, u_ref, h_s, conv_s, *, line):
    j = pl.program_id(1)
    lc = xc_ref.shape[1]
    tt = lc + xl_ref.shape[1]
    d = xc_ref.shape[2]
    nlines = (tt - lc) // line
    pitch = line + CONV_PAD
    lat0 = CONV_PAD + lc

    @pl.when(j == 0)
    def _():
        def norm_mod(x, mod):
            y = x * lax.rsqrt(jnp.mean(x * x, axis=-1, keepdims=True) + EPS) * pg_ref[...]
            return (y * (1.0 + mod[:, d:2 * d]) + mod[:, 0:d]).astype(BF16)
        h_s[0:lc, :] = norm_mod(xc_ref[0], mc_ref[...])
        h_s[lc:tt, :] = norm_mod(xl_ref[0], ml_ref[0])
        zeros = jnp.zeros((CONV_SLABS, CONV_PAD, 128), F32)
        conv_s[:, 0:CONV_PAD, :] = zeros
        for i in range(nlines + 1):
            r = lat0 + i * pitch
            conv_s[:, r:r + CONV_PAD, :] = zeros

    u = _dot(h_s[...], w_ref[...])

    def conv(act):
        for n in range(CONV_SLABS):
            ls = slice(n * 128, (n + 1) * 128)
            conv_s[n, CONV_PAD:lat0, :] = u[0:lc, ls]
            for i in range(nlines):
                r = lat0 + i * pitch + CONV_PAD
                conv_s[n, r:r + line, :] = u[lc + i * line:lc + (i + 1) * line, ls]

        def taps(n, r, rows):
            ls = slice(n * 128, (n + 1) * 128)
            cw = cw_ref[:, ls]
            out = cb_ref[:, ls] + cw[1:2, :] * conv_s[n, pl.ds(r, rows), :]
            out = out + cw[0:1, :] * conv_s[n, pl.ds(r - 1, rows), :]
            out = out + cw[2:3, :] * conv_s[n, pl.ds(r + 1, rows), :]
            out = out + cw[3:4, :] * conv_s[n, pl.ds(r + 2, rows), :]
            return act(out)

        for n in range(CONV_SLABS):
            u_ref[0, 0:lc, n * 128:(n + 1) * 128] = taps(n, CONV_PAD, lc)

        def body(i, carry):
            r = pl.multiple_of(lat0 + i * pitch + CONV_PAD, 8)
            o = pl.multiple_of(lc + i * line, 8)
            for n in range(CONV_SLABS):
                u_ref[0, pl.ds(o, line), n * 128:(n + 1) * 128] = taps(n, r, line)
            return carry
        lax.fori_loop(0, nlines, body, 0, unroll=2)

    is_xbc = j < SSD_XBC // TN
    is_xa = (j >= C_XA // TN) & (j < (C_XA + D_MODEL) // TN)

    @pl.when(is_xbc)
    def _():
        conv(_silu)

    @pl.when(is_xa)
    def _():
        conv(lambda v: v)

    @pl.when(jnp.logical_not(is_xbc | is_xa))
    def _():
        u_ref[0] = u


def _in_proj(x_ctx, x_lat, mod_lat, mod_ctx, pre_g, w_in_p, cw, cb, *, line):
    b, lc, d = x_ctx.shape
    seq = x_lat.shape[1]
    tt = lc + seq
    assert line % 8 == 0 and seq % line == 0 and lc % 8 == 0
    conv_rows = CONV_PAD + lc + (seq // line) * (line + CONV_PAD) + CONV_PAD
    return pl.pallas_call(
        functools.partial(_in_kernel, line=line),
        grid=(b, N_TOT // TN),
        in_specs=[
            pl.BlockSpec((1, lc, d), lambda i, j: (i, 0, 0)),
            pl.BlockSpec((1, seq, d), lambda i, j: (i, 0, 0)),
            pl.BlockSpec((1, 1, 3 * d), lambda i, j: (i, 0, 0)),
            pl.BlockSpec((1, 3 * d), lambda i, j: (0, 0)),
            pl.BlockSpec((1, d), lambda i, j: (0, 0)),
            pl.BlockSpec((d, TN), lambda i, j: (0, j)),
            pl.BlockSpec((CONV_W, TN), lambda i, j: (0, j)),
            pl.BlockSpec((1, TN), lambda i, j: (0, j)),
        ],
        out_specs=pl.BlockSpec((1, tt, TN), lambda i, j: (i, 0, j)),
        out_shape=jax.ShapeDtypeStruct((b, tt, N_TOT), F32),
        scratch_shapes=[pltpu.VMEM((tt, d), BF16), pltpu.VMEM((CONV_SLABS, conv_rows, 128), F32)],
        compiler_params=pltpu.CompilerParams(
            dimension_semantics=("arbitrary", "arbitrary"), vmem_limit_bytes=VMEM_LIMIT),
        name="in_proj",
    )(x_ctx, x_lat, mod_lat, mod_ctx, pre_g, w_in_p, cw, cb)


LRU_TC = 64
LRU_WT = 512
LRU_PITCH = LRU_TC + 8


def _lru_kernel(xf_ref, xb_ref, wr_ref, wi_ref, br_ref, bi_ref, lam_ref, yf_ref, yb_ref,
                a_s, b_s, o_s, h_s):
    i = pl.program_id(1)
    nb, tc, wt = xf_ref.shape

    @pl.when(i == 0)
    def _():
        h_s[...] = jnp.zeros_like(h_s)

    def gates(bi, carry):
        r0 = pl.multiple_of(bi * LRU_PITCH, 8)
        for dr, x_ref in enumerate((xf_ref, xb_ref)):
            for n in range(wt // LRU_BW):
                sl = slice(n * LRU_BW, (n + 1) * LRU_BW)
                xs = x_ref[bi, :, sl]
                xs16 = xs.astype(BF16)
                r = _sigmoid(_dot(xs16, wr_ref[dr, n]) + br_ref[dr:dr + 1, sl])
                g = _sigmoid(_dot(xs16, wi_ref[dr, n]) + bi_ref[dr:dr + 1, sl])
                nla = (LRU_C * _softplus(-lam_ref[dr:dr + 1, sl])) * r
                a = jnp.exp(-nla)
                a_s[dr, n, pl.ds(r0, tc), :] = a
                b_s[dr, n, pl.ds(r0, tc), :] = jnp.sqrt(jnp.tanh(nla) * (a * a + 1.0)) * (g * xs)
        return carry
    lax.fori_loop(0, nb, gates, 0, unroll=2)

    nl = wt // LRU_BW
    hf = [h_s[0, n] for n in range(nl)]
    hb = [h_s[1, n] for n in range(nl)]
    for t in range(tc):
        rows_f = pl.ds(t, nb, stride=LRU_PITCH)
        rows_b = pl.ds(tc - 1 - t, nb, stride=LRU_PITCH)
        for n in range(nl):
            hf[n] = a_s[0, n, rows_f, :] * hf[n] + b_s[0, n, rows_f, :]
            hb[n] = a_s[1, n, rows_b, :] * hb[n] + b_s[1, n, rows_b, :]
            o_s[0, n, rows_f, :] = hf[n]
            o_s[1, n, rows_b, :] = hb[n]
    for n in range(nl):
        h_s[0, n] = hf[n]
        h_s[1, n] = hb[n]

    def emit(bi, carry):
        r0 = pl.multiple_of(bi * LRU_PITCH, 8)
        for n in range(nl):
            sl = slice(n * LRU_BW, (n + 1) * LRU_BW)
            yf_ref[bi, :, sl] = o_s[0, n, pl.ds(r0, tc), :]
            yb_ref[bi, :, sl] = o_s[1, n, pl.ds(r0, tc), :]
        return carry
    lax.fori_loop(0, nb, emit, 0)


def _bwd_block(i, nctx, ntot):
    return jnp.where(i < nctx, nctx - 1 - i, ntot - 1 + nctx - i)


def _lru(u, wr, wi, br, bi, lam, *, lc):
    b, tt, _ = u.shape
    w = D_MODEL
    nt = tt // LRU_TC
    nctx = lc // LRU_TC
    c0 = C_XA // LRU_WT
    nblk = LRU_WT // LRU_BW
    y_spec_f = pl.BlockSpec((b, LRU_TC, LRU_WT), lambda j, i: (0, i, j))
    y_spec_b = pl.BlockSpec((b, LRU_TC, LRU_WT), lambda j, i: (0, _bwd_block(i, nctx, nt), j))
    return pl.pallas_call(
        _lru_kernel,
        grid=(w // LRU_WT, nt),
        in_specs=[
            pl.BlockSpec((b, LRU_TC, LRU_WT), lambda j, i: (0, i, c0 + j)),
            pl.BlockSpec((b, LRU_TC, LRU_WT), lambda j, i: (0, _bwd_block(i, nctx, nt), c0 + j)),
            pl.BlockSpec((2, nblk, LRU_BW, LRU_BW), lambda j, i: (0, j, 0, 0)),
            pl.BlockSpec((2, nblk, LRU_BW, LRU_BW), lambda j, i: (0, j, 0, 0)),
            pl.BlockSpec((2, LRU_WT), lambda j, i: (0, j)),
            pl.BlockSpec((2, LRU_WT), lambda j, i: (0, j)),
            pl.BlockSpec((2, LRU_WT), lambda j, i: (0, j)),
        ],
        out_specs=[y_spec_f, y_spec_b],
        out_shape=[jax.ShapeDtypeStruct((b, tt, w), F32)] * 2,
        scratch_shapes=[
            pltpu.VMEM((2, nblk, b * LRU_PITCH, LRU_BW), F32),
            pltpu.VMEM((2, nblk, b * LRU_PITCH, LRU_BW), F32),
            pltpu.VMEM((2, nblk, b * LRU_PITCH, LRU_BW), F32),
            pltpu.VMEM((2, nblk, b, LRU_BW), F32),
        ],
        compiler_params=pltpu.CompilerParams(
            dimension_semantics=("arbitrary", "arbitrary"), vmem_limit_bytes=VMEM_LIMIT),
        name="lru_scan",
    )(u, u, wr, wi, br, bi, lam)


SCAN_TB = 256
SM_W = 128
SM_DT = 0
SM_LOW = SSD_HEADS
BD_HEADS = 4
BD_W = BD_HEADS * SSD_P
LOG_FLOOR = -1e30


def _split3(x):
    hi = x.astype(BF16).astype(F32)
    r = x - hi
    mid = r.astype(BF16).astype(F32)
    return hi, mid, r - mid


def _cumsum_rows(tri16, x):
    n = x.shape[1]
    p = _dot(tri16, jnp.concatenate([s.astype(BF16) for s in _split3(x)], axis=1))
    return (p[:, 0:n] + p[:, n:2 * n]) + p[:, 2 * n:3 * n]


def _pack3(x, lane):
    hi, mid, lo = _split3(x)
    h = SSD_HEADS
    return jnp.where(lane < h, hi,
                     jnp.where(lane < 2 * h, pltpu.roll(mid, h, 1),
                               jnp.where(lane < 3 * h, pltpu.roll(lo, 2 * h, 1), 0.0)))


def _scan_kernel(qkf_ref, vf_ref, xf_ref, sf_ref, qkb_ref, vb_ref, xb_ref, sb_ref,
                 wup_ref, gb_ref, e3_ref, dtb_ref, na_ref,
                 ybf_ref, ycf_ref, ybb_ref, ycb_ref, gla_s, ssd_s):
    i = pl.program_id(1)

    @pl.when(i == 0)
    def _():
        gla_s[...] = jnp.zeros_like(gla_s)
        ssd_s[...] = jnp.zeros_like(ssd_s)

    tb = qkf_ref.shape[1]
    nch = tb // CHUNK
    row = lax.broadcasted_iota(jnp.int32, (CHUNK, CHUNK), 0)
    col = lax.broadcasted_iota(jnp.int32, (CHUNK, CHUNK), 1)
    row_x = lax.broadcasted_iota(jnp.int32, (CHUNK, SSD_INNER), 0)
    col_x = lax.broadcasted_iota(jnp.int32, (CHUNK, SSD_INNER), 1) & (CHUNK - 1)
    r2 = lax.broadcasted_iota(jnp.int32, (BD_W, BD_W), 0)
    c2 = lax.broadcasted_iota(jnp.int32, (BD_W, BD_W), 1)
    blk_mask = (r2 // CHUNK) == (c2 // SSD_P)
    lane = lax.broadcasted_iota(jnp.int32, (4 * CHUNK, SM_W), 1)
    prow = lax.broadcasted_iota(jnp.int32, (3 * CHUNK, SM_W), 0)
    plane = lax.broadcasted_iota(jnp.int32, (3 * CHUNK, SM_W), 1)
    ones_part = jnp.where((prow < CHUNK) & (plane < 3 * SSD_HEADS), 1.0, 0.0).astype(BF16)
    heads = range(GLA_HEADS)
    groups = range(SSD_G)

    def ksl(h):
        return slice(h * GLA_DKH, (h + 1) * GLA_DKH)

    def vsl(h):
        return slice(h * GLA_DVH, (h + 1) * GLA_DVH)

    def gsl(g):
        return slice(g * SSD_GW, (g + 1) * SSD_GW)

    dirs = (
        (0, qkf_ref, vf_ref, xf_ref, sf_ref, ybf_ref, ycf_ref),
        (1, qkb_ref, vb_ref, xb_ref, sb_ref, ybb_ref, ycb_ref),
    )
    for step in range(nch):
        insts = []
        for dr, qk_ref, v_ref, x_ref, s_ref, yb_ref, yc_ref in dirs:
            fwd = dr == 0
            c = step if fwd else nch - 1 - step
            causal = (row >= col) if fwd else (row <= col)
            insts.append(dict(
                dr=dr, rs=slice(c * CHUNK, (c + 1) * CHUNK), last=CHUNK - 1 if fwd else 0,
                causal=causal, causal_x=(row_x >= col_x) if fwd else (row_x <= col_x),
                tri16=jnp.where(causal, 1.0, 0.0).astype(BF16),
                qk_ref=qk_ref, v_ref=v_ref, x_ref=x_ref, s_ref=s_ref, yb_ref=yb_ref, yc_ref=yc_ref))

        for it in insts:
            dr, rs, x_ref = it["dr"], it["rs"], it["x_ref"]
            it["sm"] = it["s_ref"][0, rs, :]
            it["z"] = _dot(it["sm"].astype(BF16), wup_ref[dr])
            it["bm16"] = [x_ref[0, rs, SSD_INNER + g * SSD_N:SSD_INNER + (g + 1) * SSD_N].astype(BF16)
                          for g in groups]
            it["cm16"] = [x_ref[0, rs, SSD_INNER + (SSD_G + g) * SSD_N:
                                SSD_INNER + (SSD_G + g + 1) * SSD_N].astype(BF16) for g in groups]
            it["cb"] = [_dot_nt(it["cm16"][g], it["bm16"][g]) for g in groups]

        for it in insts:
            dr, rs, last = it["dr"], it["rs"], it["last"]
            logg = -_softplus(-(it["z"] + gb_ref[dr:dr + 1, :])) * (1.0 / GLA_TAU)
            bc = _cumsum_rows(it["tri16"], logg)
            btot = bc[last:last + 1, :]
            qk = it["qk_ref"][0, rs, :]
            k = qk[:, GLA_DK:2 * GLA_DK]
            it["qe16"] = ((qk[:, 0:GLA_DK] * (GLA_DKH ** -0.5)) * jnp.exp(bc)).astype(BF16)
            it["ke16"] = (k * jnp.exp(-bc)).astype(BF16)
            it["kd16"] = (k * jnp.exp(btot - bc)).astype(BF16)
            it["etot"] = jnp.exp(btot)
            it["v16"] = it["v_ref"][0, rs, :].astype(BF16)

            dt = _softplus(it["sm"] + dtb_ref[dr:dr + 1, :])
            cum = _cumsum_rows(it["tri16"], dt * na_ref[dr:dr + 1, :])
            clast = cum[last:last + 1, :]
            wrow = jnp.maximum(jnp.log(dt), LOG_FLOOR) - cum
            packed = _pack3(jnp.concatenate([cum, jnp.exp(cum), dt * jnp.exp(clast - cum), wrow], axis=0), lane)
            pt = packed[3 * CHUNK:4 * CHUNK].T
            pt2 = jnp.concatenate([pt, pt], axis=1).astype(BF16)
            rm = jnp.concatenate([pt2] * (SSD_INNER // 128), axis=1) * e3_ref[...]
            it["a16"] = jnp.concatenate([packed[0:3 * CHUNK].astype(BF16), ones_part], axis=1)
            it["rhs"] = jnp.concatenate([e3_ref[...], rm], axis=0)

        for it in insts:
            it["big"] = _dot(it["a16"], it["rhs"])
            it["att"] = [_dot_nt(it["qe16"][:, ksl(h)], it["ke16"][:, ksl(h)]) for h in heads]

        for it in insts:
            big = it["big"]
            lmat = jnp.exp(jnp.where(it["causal_x"], big[0:CHUNK], -jnp.inf))
            m16 = []
            for g in groups:
                cb2 = jnp.concatenate([it["cb"][g], it["cb"][g]], axis=1)
                m16.append(jnp.concatenate(
                    [cb2 * lmat[:, g * SSD_GW + q * 128:g * SSD_GW + (q + 1) * 128] for q in range(SSD_GW // 128)],
                    axis=1).astype(BF16))
            it["m16"] = m16
            it["ecum_x"] = big[CHUNK:2 * CHUNK]
            xs = it["x_ref"][0, it["rs"], 0:SSD_INNER]
            it["xs16"] = xs.astype(BF16)
            it["xdec16"] = (xs * big[2 * CHUNK:3 * CHUNK]).astype(BF16)
            it["att16"] = [jnp.where(it["causal"], a, 0.0).astype(BF16) for a in it["att"]]

        for it in insts:
            dr = it["dr"]
            it["o_in"] = [_dot(it["att16"][h], it["v16"][:, vsl(h)]) for h in heads]
            it["o_st"] = [_dot_nt(it["qe16"][:, ksl(h)], gla_s[dr, h].astype(BF16)) for h in heads]
            it["u_gla"] = [_dot_tn(it["v16"][:, vsl(h)], it["kd16"][:, ksl(h)]) for h in heads]
            y_in = []
            for g in groups:
                parts = []
                for pr in range(SSD_GW // BD_W):
                    ls = slice(g * SSD_GW + pr * BD_W, g * SSD_GW + (pr + 1) * BD_W)
                    xh = it["xs16"][:, ls]
                    bd = jnp.where(blk_mask, jnp.concatenate([xh] * BD_HEADS, axis=0), jnp.zeros((), BF16))
                    parts.append(_dot(it["m16"][g][:, pr * BD_W:(pr + 1) * BD_W], bd))
                y_in.append(jnp.concatenate(parts, axis=1))
            it["y_in"] = y_in
            it["y_st"] = [_dot(it["cm16"][g], ssd_s[dr, g].astype(BF16)) for g in groups]
            it["u_ssd"] = [_dot_tn(it["bm16"][g], it["xdec16"][:, gsl(g)]) for g in groups]

        for it in insts:
            dr, rs, last = it["dr"], it["rs"], it["last"]
            for h in heads:
                it["yb_ref"][0, rs, vsl(h)] = it["o_in"][h] + it["o_st"][h]
                gla_s[dr, h] = gla_s[dr, h] * it["etot"][:, ksl(h)] + it["u_gla"][h]
            for g in groups:
                ecum = it["ecum_x"][:, gsl(g)]
                it["yc_ref"][0, rs, gsl(g)] = it["y_in"][g] + it["y_st"][g] * ecum
                ssd_s[dr, g] = ssd_s[dr, g] * ecum[last:last + 1, :] + it["u_ssd"][g]


def _scan(u, wup, gb, e3, dtb, na, *, lc):
    b, tt, _ = u.shape
    tb = SCAN_TB
    nt = tt // tb
    nctx = lc // tb

    def fmap(cb):
        return lambda bi, i: (bi, i, cb)

    def bmap(cb):
        return lambda bi, i: (bi, _bwd_block(i, nctx, nt), cb)

    def tok_specs(mk, dr):
        return [
            pl.BlockSpec((1, tb, 2 * GLA_DK), mk(C_Q // (2 * GLA_DK))),
            pl.BlockSpec((1, tb, GLA_DV), mk(C_V // GLA_DV)),
            pl.BlockSpec((1, tb, SSD_XBC), mk(C_XBC // SSD_XBC)),
            pl.BlockSpec((1, tb, SM_W), mk(C_SM // SM_W + dr)),
        ]

    def const_spec(a):
        nd = a.ndim
        return pl.BlockSpec(a.shape, lambda bi, i: (0,) * nd)

    consts = (wup, gb, e3, dtb, na)
    return pl.pallas_call(
        _scan_kernel,
        grid=(b, nt),
        in_specs=tok_specs(fmap, 0) + tok_specs(bmap, 1) + [const_spec(a) for a in consts],
        out_specs=[
            pl.BlockSpec((1, tb, GLA_DV), fmap(0)),
            pl.BlockSpec((1, tb, SSD_INNER), fmap(0)),
            pl.BlockSpec((1, tb, GLA_DV), bmap(0)),
            pl.BlockSpec((1, tb, SSD_INNER), bmap(0)),
        ],
        out_shape=[
            jax.ShapeDtypeStruct((b, tt, GLA_DV), F32),
            jax.ShapeDtypeStruct((b, tt, SSD_INNER), F32),
            jax.ShapeDtypeStruct((b, tt, GLA_DV), F32),
            jax.ShapeDtypeStruct((b, tt, SSD_INNER), F32),
        ],
        scratch_shapes=[
            pltpu.VMEM((2, GLA_HEADS, GLA_DVH, GLA_DKH), F32),
            pltpu.VMEM((2, SSD_G, SSD_N, SSD_GW), F32),
        ],
        compiler_params=pltpu.CompilerParams(
            dimension_semantics=("arbitrary", "arbitrary"), vmem_limit_bytes=VMEM_LIMIT),
        name="gla_ssd_scan",
    )(u, u, u, u, u, u, u, u, *consts)


FIN_TM = 128


def _group_rms(x, width):
    parts = []
    for s in range(x.shape[1] // width):
        xs = x[:, s * width:(s + 1) * width]
        parts.append(xs * lax.rsqrt(jnp.mean(xs * xs, axis=-1, keepdims=True) + EPS))
    return jnp.concatenate(parts, axis=1)


def _fin_kernel(yaf_ref, yab_ref, ybf_ref, ybb_ref, ycf_ref, ycb_ref, lg_ref, gg_ref, z_ref, mg_ref,
                xs_ref, xc_ref, xl_ref, ml_ref, mc_ref, gng_ref, sng_ref, sd_ref, pog_ref,
                wpa_ref, wpb_ref, wpc_ref, wo_ref, oc_ref, ol_ref, *, nctx):
    i = pl.program_id(1)
    d = D_MODEL
    ya = yaf_ref[0] + yab_ref[0]
    pa = _dot((ya * _silu(lg_ref[0])).astype(BF16), wpa_ref[...])
    yb = ybf_ref[0] + ybb_ref[0]
    ob = (_group_rms(yb, GLA_DVH) * gng_ref[...]) * _silu(gg_ref[0])
    pb = _dot(ob.astype(BF16), wpb_ref[...])
    yc = (ycf_ref[0] + ycb_ref[0]) + sd_ref[...] * xs_ref[0]
    oc = _group_rms(yc * _silu(z_ref[0]), SSD_GW) * sng_ref[...]
    pc = _dot(oc.astype(BF16), wpc_ref[...])
    gates = _sigmoid(mg_ref[0])
    merged = gates[:, 0:d] * pa + gates[:, d:2 * d] * pb + gates[:, 2 * d:3 * d] * pc
    out = _dot(merged.astype(BF16), wo_ref[...])
    normed = _group_rms(out, d) * pog_ref[...]

    @pl.when(i < nctx)
    def _():
        oc_ref[0] = xc_ref[0] + mc_ref[:, 2 * d:3 * d] * normed

    @pl.when(i >= nctx)
    def _():
        ol_ref[0] = xl_ref[0] + ml_ref[0][:, 2 * d:3 * d] * normed


def _finish(ya_f, ya_b, yb_f, yb_b, yc_f, yc_b, u, x_ctx, x_lat, mod_lat, mod_ctx,
            gng, sng, sd_x, post_g, w_pa, w_pb, w_pc, w_out):
    b, lc, d = x_ctx.shape
    seq = x_lat.shape[1]
    tm = FIN_TM
    nctx = lc // tm
    ctx_spec = pl.BlockSpec((1, tm, d), lambda bi, i: (bi, jnp.minimum(i, nctx - 1), 0))
    lat_spec = pl.BlockSpec((1, tm, d), lambda bi, i: (bi, jnp.maximum(i - nctx, 0), 0))

    def tok(width, cb=0):
        return pl.BlockSpec((1, tm, width), lambda bi, i: (bi, i, cb))

    def const_spec(a):
        nd = a.ndim
        return pl.BlockSpec(a.shape, lambda bi, i: (0,) * nd)

    consts = (gng, sng, sd_x, post_g, w_pa, w_pb, w_pc, w_out)
    return pl.pallas_call(
        functools.partial(_fin_kernel, nctx=nctx),
        grid=(b, (lc + seq) // tm),
        in_specs=[
            tok(d), tok(d), tok(GLA_DV), tok(GLA_DV), tok(SSD_INNER), tok(SSD_INNER),
            tok(d, C_LG // d), tok(GLA_DV, C_GG // GLA_DV), tok(SSD_INNER, C_Z // SSD_INNER),
            tok(3 * d, C_MRG // (3 * d)), tok(SSD_INNER, C_XBC // SSD_INNER), ctx_spec, lat_spec,
            pl.BlockSpec((1, 1, 3 * d), lambda bi, i: (bi, 0, 0)),
            pl.BlockSpec((1, 3 * d), lambda bi, i: (0, 0)),
        ] + [const_spec(a) for a in consts],
        out_specs=[ctx_spec, lat_spec],
        out_shape=[jax.ShapeDtypeStruct((b, lc, d), F32), jax.ShapeDtypeStruct((b, seq, d), F32)],
        compiler_params=pltpu.CompilerParams(
            dimension_semantics=("arbitrary", "arbitrary"), vmem_limit_bytes=VMEM_LIMIT),
        name="finish",
    )(ya_f, ya_b, yb_f, yb_b, yc_f, yc_b, u, u, u, u, u, x_ctx, x_lat, mod_lat, mod_ctx, *consts)


def _reorder_in_weights(w_in):
    d = w_in.shape[0]
    o = 0
    seg = {}
    for name, wd in (("xa", D_MODEL), ("lg", D_MODEL), ("q", GLA_DK), ("k", GLA_DK), ("v", GLA_DV),
                     ("gg", GLA_DV), ("low", 2 * GLA_RANK), ("z", SSD_INNER), ("xbc", SSD_XBC),
                     ("dt", 2 * SSD_HEADS), ("mrg", 3 * D_MODEL)):
        seg[name] = w_in[:, o:o + wd]
        o += wd
    pad = jnp.zeros((d, SM_W - GLA_RANK - SSD_HEADS), w_in.dtype)
    cols = [seg["xbc"], seg["mrg"], seg["z"], seg["xa"], seg["v"], seg["lg"], seg["gg"],
            seg["q"], seg["k"]]
    for dr in range(2):
        cols += [seg["dt"][:, dr * SSD_HEADS:(dr + 1) * SSD_HEADS],
                 seg["low"][:, dr * GLA_RANK:(dr + 1) * GLA_RANK], pad]
    cols.append(jnp.zeros((d, N_TOT - C_SM - 2 * SM_W), w_in.dtype))
    return jnp.concatenate(cols, axis=1)


def _conv_tables(conv_a_w, conv_a_b, conv_c_w, conv_c_b):
    cw = jnp.zeros((CONV_W, N_TOT), F32)
    cw = cw.at[:, C_XBC:C_XBC + SSD_XBC].set(conv_c_w).at[:, C_XA:C_XA + D_MODEL].set(conv_a_w)
    cb = jnp.zeros((1, N_TOT), F32)
    cb = cb.at[0, C_XBC:C_XBC + SSD_XBC].set(conv_c_b).at[0, C_XA:C_XA + D_MODEL].set(conv_a_b)
    return cw, cb


def _scan_tables(gla_alpha_up, ssd_dt_bias, ssd_a_log):
    wup = jnp.zeros((2, SM_W, GLA_DK), F32).at[:, SM_LOW:SM_LOW + GLA_RANK, :].set(gla_alpha_up)
    dtb = jnp.zeros((2, SM_W), F32).at[:, SM_DT:SM_DT + SSD_HEADS].set(ssd_dt_bias)
    a_log = jnp.zeros((2, SM_W), F32).at[:, SM_DT:SM_DT + SSD_HEADS].set(ssd_a_log)
    valid = np.zeros((2, SM_W), np.float32)
    valid[:, SM_DT:SM_DT + SSD_HEADS] = 1.0
    e3 = np.zeros((SM_W, SSD_INNER), np.float32)
    for k in range(3 * SSD_HEADS):
        h = k % SSD_HEADS
        e3[k, h * SSD_P:(h + 1) * SSD_P] = 1.0
    return wup.astype(BF16), jnp.asarray(e3, BF16), dtb, a_log, jnp.asarray(valid)


def _to_col_major(h, rows):
    b, length, dm = h.shape
    return h.reshape(b, rows, GRID_W, dm).transpose(0, 2, 1, 3).reshape(b, length, dm)


def _from_col_major(h, rows):
    b, length, dm = h.shape
    return h.reshape(b, GRID_W, rows, dm).transpose(0, 2, 1, 3).reshape(b, length, dm)


def _neg_exp_kernel(a_ref, v_ref, o_ref):
    o_ref[...] = -jnp.exp(a_ref[...]) * v_ref[...]


def kernel(x, c, ctx, c_ctx, ada_w, ada_b, pre_g, post_g, w_in, conv_a_w, conv_a_b, lru_wr, lru_br,
           lru_wi, lru_bi, lru_lam, gla_alpha_up, gla_alpha_b, gla_norm_g, conv_c_w, conv_c_b,
           ssd_a_log, ssd_dt_bias, ssd_d, ssd_norm_g, w_pa, w_pb, w_pc, w_out):
    b, seq, d = x.shape
    lc = ctx.shape[1]
    depth = w_in.shape[0]
    rows = seq // GRID_W
    assert d == D_MODEL and lc % SCAN_TB == 0 and seq % SCAN_TB == 0 and b % 8 == 0

    pad_rows = (-(b + 1)) % 8
    cc = jnp.concatenate([c, c_ctx[None, :], jnp.zeros((pad_rows, d), F32)], axis=0)
    mods = _adaln(cc, ada_w, ada_b)

    w_in16 = w_in.astype(BF16)
    x_lat, x_ctx = x, ctx
    for l in range(depth):
        col_major = l % 2 == 1
        line = rows if col_major else GRID_W
        mod_lat = mods[l, :b].reshape(b, 1, 3 * d)
        mod_ctx = mods[l, b:b + 1]
        lat = _to_col_major(x_lat, rows) if col_major else x_lat

        cw, cb = _conv_tables(conv_a_w[l], conv_a_b[l], conv_c_w[l], conv_c_b[l])
        u = _in_proj(x_ctx, lat, mod_lat, mod_ctx, pre_g[l].reshape(1, d), _reorder_in_weights(w_in16[l]),
                     cw, cb, line=line)

        ya_f, ya_b = _lru(u, lru_wr[l].astype(BF16), lru_wi[l].astype(BF16), lru_br[l], lru_bi[l],
                          lru_lam[l], lc=lc)

        wup, e3, dtb, a_log, valid = _scan_tables(gla_alpha_up[l], ssd_dt_bias[l], ssd_a_log[l])
        na = pl.pallas_call(_neg_exp_kernel, out_shape=jax.ShapeDtypeStruct(a_log.shape, F32),
                            name="ssd_neg_a")(a_log, valid)
        yb_f, yc_f, yb_b, yc_b = _scan(u, wup, gla_alpha_b[l], e3, dtb, na, lc=lc)

        x_ctx, lat = _finish(ya_f, ya_b, yb_f, yb_b, yc_f, yc_b, u, x_ctx, lat, mod_lat, mod_ctx,
                             jnp.tile(gla_norm_g[l], GLA_HEADS).reshape(1, GLA_DV),
                             ssd_norm_g[l].reshape(1, SSD_INNER),
                             jnp.repeat(ssd_d[l], SSD_P).reshape(1, SSD_INNER),
                             post_g[l].reshape(1, d),
                             w_pa[l].astype(BF16), w_pb[l].astype(BF16), w_pc[l].astype(BF16),
                             w_out[l].astype(BF16))
        x_lat = _from_col_major(lat, rows) if col_major else lat
    return x_lat
```

```python
import functools

import jax
import jax.numpy as jnp
import numpy as np
from jax import lax
from jax.experimental import pallas as pl
from jax.experimental.pallas import tpu as pltpu

F32 = jnp.float32
BF16 = jnp.bfloat16

D_MODEL = 1024
GRID_W = 64
CONV_W = 4
EPS = 1e-6
LRU_BW = 128
LRU_C = 8.0
GLA_HEADS = 4
GLA_DK = 512
GLA_DV = 1024
GLA_DKH = 128
GLA_DVH = 256
GLA_RANK = 16
GLA_TAU = 16.0
SSD_INNER = 2048
SSD_P = 64
SSD_HEADS = 32
SSD_N = 128
SSD_G = 4
SSD_GW = SSD_INNER // SSD_G
SSD_XBC = SSD_INNER + 2 * SSD_G * SSD_N
CHUNK = 64

TN = 512
C_XBC, C_MRG, C_Z, C_XA, C_V, C_LG, C_GG, C_Q, C_K, C_SM = (
    0, 3072, 6144, 8192, 9216, 10240, 11264, 12288, 12800, 13312)
N_TOT = C_SM + TN

VMEM_LIMIT = 56 * 1024 * 1024


def _sigmoid(x):
    return 0.5 * jnp.tanh(0.5 * x) + 0.5


def _silu(x):
    return x * _sigmoid(x)


def _softplus(x):
    return jnp.maximum(x, 0.0) + jnp.log1p(jnp.exp(-jnp.abs(x)))


def _dot(a, b):
    return jnp.dot(a, b, preferred_element_type=F32)


def _dot_nt(a, b):
    return lax.dot_general(a, b, (((1,), (1,)), ((), ())), preferred_element_type=F32)


def _dot_tn(a, b):
    return lax.dot_general(a, b, (((0,), (0,)), ((), ())), preferred_element_type=F32)


def _adaln_kernel(c_ref, w_ref, b_ref, o_ref):
    c = c_ref[...]
    o_ref[0] = _dot(_silu(c).astype(BF16), w_ref[0]) + b_ref[0]


def _adaln(cc, ada_w, ada_b):
    depth, d, n3 = ada_w.shape
    rows = cc.shape[0]
    tn = 1024
    return pl.pallas_call(
        _adaln_kernel,
        grid=(depth, n3 // tn),
        in_specs=[
            pl.BlockSpec((rows, d), lambda l, n: (0, 0)),
            pl.BlockSpec((1, d, tn), lambda l, n: (l, 0, n)),
            pl.BlockSpec((1, 1, tn), lambda l, n: (l, 0, n)),
        ],
        out_specs=pl.BlockSpec((1, rows, tn), lambda l, n: (l, 0, n)),
        out_shape=jax.ShapeDtypeStruct((depth, rows, n3), F32),
        name="adaln",
    )(cc, ada_w.astype(BF16), ada_b.reshape(depth, 1, n3))


CONV_PAD = 8
CONV_SLABS = TN // 128

def _in_kernel(xc_ref, xl_ref, ml_ref, mc_ref, pg_ref, w_ref, cw_ref, cb_ref, u_ref, h_s, conv_s, *, line):
    j = pl.program_id(1)
    lc = xc_ref.shape[1]
    tt = lc + xl_ref.shape[1]
    d = xc_ref.shape[2]
    nlines = (tt - lc) // line
    pitch = line + CONV_PAD
    lat0 = CONV_PAD + lc

    @pl.when(j == 0)
    def _():
        def norm_mod(x, mod):
            y = x * lax.rsqrt(jnp.mean(x * x, axis=-1, keepdims=True) + EPS) * pg_ref[...]
            return (y * (1.0 + mod[:, d:2 * d]) + mod[:, 0:d]).astype(BF16)
        h_s[0:lc, :] = norm_mod(xc_ref[0], mc_ref[...])
        h_s[lc:tt, :] = norm_mod(xl_ref[0], ml_ref[0])
        zeros = jnp.zeros((CONV_SLABS, CONV_PAD, 128), F32)
        conv_s[:, 0:CONV_PAD, :] = zeros
        for i in range(nlines + 1):
            r = lat0 + i * pitch
            conv_s[:, r:r + CONV_PAD, :] = zeros

    u = _dot(h_s[...], w_ref[...])

    def conv(act):
        for n in range(CONV_SLABS):
            ls = slice(n * 128, (n + 1) * 128)
            conv_s[n, CONV_PAD:lat0, :] = u[0:lc, ls]
            for i in range(nlines):
                r = lat0 + i * pitch + CONV_PAD
                conv_s[n, r:r + line, :] = u[lc + i * line:lc + (i + 1) * line, ls]

        def taps(n, r, rows):
            ls = slice(n * 128, (n + 1) * 128)
            cw = cw_ref[:, ls]
            out = cb_ref[:, ls] + cw[1:2, :] * conv_s[n, pl.ds(r, rows), :]
            out = out + cw[0:1, :] * conv_s[n, pl.ds(r - 1, rows), :]
            out = out + cw[2:3, :] * conv_s[n, pl.ds(r + 1, rows), :]
            out = out + cw[3:4, :] * conv_s[n, pl.ds(r + 2, rows), :]
            return act(out)

        for n in range(CONV_SLABS):
            u_ref[0, 0, 0:lc, n * 128:(n + 1) * 128] = taps(n, CONV_PAD, lc)

        def body(i, carry):
            r = pl.multiple_of(lat0 + i * pitch + CONV_PAD, 8)
            o = pl.multiple_of(lc + i * line, 8)
            for n in range(CONV_SLABS):
                u_ref[0, 0, pl.ds(o, line), n * 128:(n + 1) * 128] = taps(n, r, line)
            return carry
        lax.fori_loop(0, nlines, body, 0, unroll=2)

    is_xbc = j < SSD_XBC // TN
    is_xa = (j >= C_XA // TN) & (j < (C_XA + D_MODEL) // TN)

    @pl.when(is_xbc)
    def _():
        conv(_silu)

    @pl.when(is_xa)
    def _():
        conv(lambda v: v)

    @pl.when(jnp.logical_not(is_xbc | is_xa))
    def _():
        u_ref[0, 0] = u


def _in_proj(x_ctx, x_lat, mod_lat, mod_ctx, pre_g, w_in_p, cw, cb, *, line):
    b, lc, d = x_ctx.shape
    seq = x_lat.shape[1]
    tt = lc + seq
    assert line % 8 == 0 and seq % line == 0 and lc % 8 == 0
    conv_rows = CONV_PAD + lc + (seq // line) * (line + CONV_PAD) + CONV_PAD
    return pl.pallas_call(
        functools.partial(_in_kernel, line=line),
        grid=(b, N_TOT // TN),
        in_specs=[
            pl.BlockSpec((1, lc, d), lambda i, j: (i, 0, 0)),
            pl.BlockSpec((1, seq, d), lambda i, j: (i, 0, 0)),
            pl.BlockSpec((1, 1, 3 * d), lambda i, j: (i, 0, 0)),
            pl.BlockSpec((1, 3 * d), lambda i, j: (0, 0)),
            pl.BlockSpec((1, d), lambda i, j: (0, 0)),
            pl.BlockSpec((d, TN), lambda i, j: (0, j)),
            pl.BlockSpec((CONV_W, TN), lambda i, j: (0, j)),
            pl.BlockSpec((1, TN), lambda i, j: (0, j)),
        ],
        out_specs=pl.BlockSpec((1, 1, tt, TN), lambda i, j: (i, j, 0, 0)),
        out_shape=jax.ShapeDtypeStruct((b, N_TOT // TN, tt, TN), F32),
        scratch_shapes=[pltpu.VMEM((tt, d), BF16), pltpu.VMEM((CONV_SLABS, conv_rows, 128), F32)],
        compiler_params=pltpu.CompilerParams(
            dimension_semantics=("arbitrary", "arbitrary"), vmem_limit_bytes=VMEM_LIMIT),
        name="in_proj",
    )(x_ctx, x_lat, mod_lat, mod_ctx, pre_g, w_in_p, cw, cb)


LRU_TC = 64
LRU_WT = 512
LRU_PITCH = LRU_TC + 8


def _lru_kernel(xf_ref, xb_ref, wr_ref, wi_ref, br_ref, bi_ref, lam_ref, yf_ref, yb_ref,
                a_s, b_s, o_s, h_s):
    i = pl.program_id(1)
    nb, _, tc, wt = xf_ref.shape

    @pl.when(i == 0)
    def _():
        h_s[...] = jnp.zeros_like(h_s)

    def gates(bi, carry):
        r0 = pl.multiple_of(bi * LRU_PITCH, 8)
        for dr, x_ref in enumerate((xf_ref, xb_ref)):
            for n in range(wt // LRU_BW):
                sl = slice(n * LRU_BW, (n + 1) * LRU_BW)
                xs = x_ref[bi, 0, :, sl]
                xs16 = xs.astype(BF16)
                r = _sigmoid(_dot(xs16, wr_ref[dr, n]) + br_ref[dr:dr + 1, sl])
                g = _sigmoid(_dot(xs16, wi_ref[dr, n]) + bi_ref[dr:dr + 1, sl])
                nla = (LRU_C * _softplus(-lam_ref[dr:dr + 1, sl])) * r
                a = jnp.exp(-nla)
                a_s[dr, n, pl.ds(r0, tc), :] = a
                b_s[dr, n, pl.ds(r0, tc), :] = jnp.sqrt(jnp.tanh(nla) * (a * a + 1.0)) * (g * xs)
        return carry
    lax.fori_loop(0, nb, gates, 0, unroll=2)

    nl = wt // LRU_BW
    hf = [h_s[0, n] for n in range(nl)]
    hb = [h_s[1, n] for n in range(nl)]
    for t in range(tc):
        rows_f = pl.ds(t, nb, stride=LRU_PITCH)
        rows_b = pl.ds(tc - 1 - t, nb, stride=LRU_PITCH)
        for n in range(nl):
            hf[n] = a_s[0, n, rows_f, :] * hf[n] + b_s[0, n, rows_f, :]
            hb[n] = a_s[1, n, rows_b, :] * hb[n] + b_s[1, n, rows_b, :]
            o_s[0, n, rows_f, :] = hf[n]
            o_s[1, n, rows_b, :] = hb[n]
    for n in range(nl):
        h_s[0, n] = hf[n]
        h_s[1, n] = hb[n]

    def emit(bi, carry):
        r0 = pl.multiple_of(bi * LRU_PITCH, 8)
        for n in range(nl):
            sl = slice(n * LRU_BW, (n + 1) * LRU_BW)
            yf_ref[bi, :, sl] = o_s[0, n, pl.ds(r0, tc), :]
            yb_ref[bi, :, sl] = o_s[1, n, pl.ds(r0, tc), :]
        return carry
    lax.fori_loop(0, nb, emit, 0)


def _bwd_block(i, nctx, ntot):
    return jnp.where(i < nctx, nctx - 1 - i, ntot - 1 + nctx - i)


def _lru(u, wr, wi, br, bi, lam, *, lc):
    b, _, tt, _ = u.shape
    w = D_MODEL
    nt = tt // LRU_TC
    nctx = lc // LRU_TC
    assert LRU_WT == TN
    c0 = C_XA // TN
    nblk = LRU_WT // LRU_BW
    y_spec_f = pl.BlockSpec((b, LRU_TC, LRU_WT), lambda j, i: (0, i, j))
    y_spec_b = pl.BlockSpec((b, LRU_TC, LRU_WT), lambda j, i: (0, _bwd_block(i, nctx, nt), j))
    return pl.pallas_call(
        _lru_kernel,
        grid=(w // LRU_WT, nt),
        in_specs=[
            pl.BlockSpec((b, 1, LRU_TC, LRU_WT), lambda j, i: (0, c0 + j, i, 0)),
            pl.BlockSpec((b, 1, LRU_TC, LRU_WT), lambda j, i: (0, c0 + j, _bwd_block(i, nctx, nt), 0)),
            pl.BlockSpec((2, nblk, LRU_BW, LRU_BW), lambda j, i: (0, j, 0, 0)),
            pl.BlockSpec((2, nblk, LRU_BW, LRU_BW), lambda j, i: (0, j, 0, 0)),
            pl.BlockSpec((2, LRU_WT), lambda j, i: (0, j)),
            pl.BlockSpec((2, LRU_WT), lambda j, i: (0, j)),
            pl.BlockSpec((2, LRU_WT), lambda j, i: (0, j)),
        ],
        out_specs=[y_spec_f, y_spec_b],
        out_shape=[jax.ShapeDtypeStruct((b, tt, w), F32)] * 2,
        scratch_shapes=[
            pltpu.VMEM((2, nblk, b * LRU_PITCH, LRU_BW), F32),
            pltpu.VMEM((2, nblk, b * LRU_PITCH, LRU_BW), F32),
            pltpu.VMEM((2, nblk, b * LRU_PITCH, LRU_BW), F32),
            pltpu.VMEM((2, nblk, b, LRU_BW), F32),
        ],
        compiler_params=pltpu.CompilerParams(
            dimension_semantics=("arbitrary", "arbitrary"), vmem_limit_bytes=VMEM_LIMIT),
        name="lru_scan",
    )(u, u, wr, wi, br, bi, lam)


SCAN_TB = 256
SM_W = 128
SM_DT = 0
SM_LOW = SSD_HEADS
BD_HEADS = 4
BD_W = BD_HEADS * SSD_P
LOG_FLOOR = -1e30


T_SSD_B = SSD_INNER // TN
T_SSD_C = T_SSD_B + 1


def _seg_spec(rows, c0, width, blk):
    nt = width // TN
    assert width % TN == 0 and (c0 // TN) % nt == 0
    return pl.BlockSpec((1, nt, rows, TN), lambda bi, i: (bi, c0 // width, blk(i), 0))


def _wide(ref, rs, ntiles=None):
    nt = ref.shape[1] if ntiles is None else ntiles
    return jnp.concatenate([ref[0, t, rs, :] for t in range(nt)], axis=1)


def _split3(x):
    hi = x.astype(BF16).astype(F32)
    r = x - hi
    mid = r.astype(BF16).astype(F32)
    return hi, mid, r - mid


def _cumsum_rows(tri16, x):
    n = x.shape[1]
    p = _dot(tri16, jnp.concatenate([s.astype(BF16) for s in _split3(x)], axis=1))
    return (p[:, 0:n] + p[:, n:2 * n]) + p[:, 2 * n:3 * n]


def _pack3(x, lane):
    hi, mid, lo = _split3(x)
    h = SSD_HEADS
    return jnp.where(lane < h, hi,
                     jnp.where(lane < 2 * h, pltpu.roll(mid, h, 1),
                               jnp.where(lane < 3 * h, pltpu.roll(lo, 2 * h, 1), 0.0)))


def _scan_kernel(qf_ref, kf_ref, vf_ref, xf_ref, sf_ref, qb_ref, kb_ref, vb_ref, xb_ref, sb_ref,
                 wup_ref, gb_ref, e3_ref, dtb_ref, na_ref,
                 ybf_ref, ycf_ref, ybb_ref, ycb_ref, gla_s, ssd_s):
    i = pl.program_id(1)

    @pl.when(i == 0)
    def _():
        gla_s[...] = jnp.zeros_like(gla_s)
        ssd_s[...] = jnp.zeros_like(ssd_s)

    tb = qf_ref.shape[2]
    nch = tb // CHUNK
    row = lax.broadcasted_iota(jnp.int32, (CHUNK, CHUNK), 0)
    col = lax.broadcasted_iota(jnp.int32, (CHUNK, CHUNK), 1)
    row_x = lax.broadcasted_iota(jnp.int32, (CHUNK, SSD_INNER), 0)
    col_x = lax.broadcasted_iota(jnp.int32, (CHUNK, SSD_INNER), 1) & (CHUNK - 1)
    r2 = lax.broadcasted_iota(jnp.int32, (BD_W, BD_W), 0)
    c2 = lax.broadcasted_iota(jnp.int32, (BD_W, BD_W), 1)
    blk_mask = (r2 // CHUNK) == (c2 // SSD_P)
    lane = lax.broadcasted_iota(jnp.int32, (4 * CHUNK, SM_W), 1)
    prow = lax.broadcasted_iota(jnp.int32, (3 * CHUNK, SM_W), 0)
    plane = lax.broadcasted_iota(jnp.int32, (3 * CHUNK, SM_W), 1)
    ones_part = jnp.where((prow < CHUNK) & (plane < 3 * SSD_HEADS), 1.0, 0.0).astype(BF16)
    heads = range(GLA_HEADS)
    groups = range(SSD_G)

    def ksl(h):
        return slice(h * GLA_DKH, (h + 1) * GLA_DKH)

    def vsl(h):
        return slice(h * GLA_DVH, (h + 1) * GLA_DVH)

    def gsl(g):
        return slice(g * SSD_GW, (g + 1) * SSD_GW)

    dirs = (
        (0, qf_ref, kf_ref, vf_ref, xf_ref, sf_ref, ybf_ref, ycf_ref),
        (1, qb_ref, kb_ref, vb_ref, xb_ref, sb_ref, ybb_ref, ycb_ref),
    )
    for step in range(nch):
        insts = []
        for dr, q_ref, k_ref, v_ref, x_ref, s_ref, yb_ref, yc_ref in dirs:
            fwd = dr == 0
            c = step if fwd else nch - 1 - step
            causal = (row >= col) if fwd else (row <= col)
            insts.append(dict(
                dr=dr, rs=slice(c * CHUNK, (c + 1) * CHUNK), last=CHUNK - 1 if fwd else 0,
                causal=causal, causal_x=(row_x >= col_x) if fwd else (row_x <= col_x),
                tri16=jnp.where(causal, 1.0, 0.0).astype(BF16),
                q_ref=q_ref, k_ref=k_ref, v_ref=v_ref, x_ref=x_ref, s_ref=s_ref, yb_ref=yb_ref, yc_ref=yc_ref))

        for it in insts:
            dr, rs, x_ref = it["dr"], it["rs"], it["x_ref"]
            it["sm"] = it["s_ref"][0, 0, rs, :]
            it["z"] = _dot(it["sm"].astype(BF16), wup_ref[dr])
            it["bm16"] = [x_ref[0, T_SSD_B, rs, g * SSD_N:(g + 1) * SSD_N].astype(BF16) for g in groups]
            it["cm16"] = [x_ref[0, T_SSD_C, rs, g * SSD_N:(g + 1) * SSD_N].astype(BF16) for g in groups]
            it["cb"] = [_dot_nt(it["cm16"][g], it["bm16"][g]) for g in groups]

        for it in insts:
            dr, rs, last = it["dr"], it["rs"], it["last"]
            logg = -_softplus(-(it["z"] + gb_ref[dr:dr + 1, :])) * (1.0 / GLA_TAU)
            bc = _cumsum_rows(it["tri16"], logg)
            btot = bc[last:last + 1, :]
            k = it["k_ref"][0, 0, rs, :]
            it["qe16"] = ((it["q_ref"][0, 0, rs, :] * (GLA_DKH ** -0.5)) * jnp.exp(bc)).astype(BF16)
            it["ke16"] = (k * jnp.exp(-bc)).astype(BF16)
            it["kd16"] = (k * jnp.exp(btot - bc)).astype(BF16)
            it["etot"] = jnp.exp(btot)
            it["v16"] = _wide(it["v_ref"], rs).astype(BF16)

            dt = _softplus(it["sm"] + dtb_ref[dr:dr + 1, :])
            cum = _cumsum_rows(it["tri16"], dt * na_ref[dr:dr + 1, :])
            clast = cum[last:last + 1, :]
            wrow = jnp.maximum(jnp.log(dt), LOG_FLOOR) - cum
            packed = _pack3(jnp.concatenate([cum, jnp.exp(cum), dt * jnp.exp(clast - cum), wrow], axis=0), lane)
            pt = packed[3 * CHUNK:4 * CHUNK].T
            pt2 = jnp.concatenate([pt, pt], axis=1).astype(BF16)
            rm = jnp.concatenate([pt2] * (SSD_INNER // 128), axis=1) * e3_ref[...]
            it["a16"] = jnp.concatenate([packed[0:3 * CHUNK].astype(BF16), ones_part], axis=1)
            it["rhs"] = jnp.concatenate([e3_ref[...], rm], axis=0)

        for it in insts:
            it["big"] = _dot(it["a16"], it["rhs"])
            it["att"] = [_dot_nt(it["qe16"][:, ksl(h)], it["ke16"][:, ksl(h)]) for h in heads]

        for it in insts:
            big = it["big"]
            lmat = jnp.exp(jnp.where(it["causal_x"], big[0:CHUNK], -jnp.inf))
            m16 = []
            for g in groups:
                cb2 = jnp.concatenate([it["cb"][g], it["cb"][g]], axis=1)
                m16.append(jnp.concatenate(
                    [cb2 * lmat[:, g * SSD_GW + q * 128:g * SSD_GW + (q + 1) * 128] for q in range(SSD_GW // 128)],
                    axis=1).astype(BF16))
            it["m16"] = m16
            it["ecum_x"] = big[CHUNK:2 * CHUNK]
            xs = _wide(it["x_ref"], it["rs"], SSD_INNER // TN)
            it["xs16"] = xs.astype(BF16)
            it["xdec16"] = (xs * big[2 * CHUNK:3 * CHUNK]).astype(BF16)
            it["att16"] = [jnp.where(it["causal"], a, 0.0).astype(BF16) for a in it["att"]]

        for it in insts:
            dr = it["dr"]
            it["o_in"] = [_dot(it["att16"][h], it["v16"][:, vsl(h)]) for h in heads]
            it["o_st"] = [_dot_nt(it["qe16"][:, ksl(h)], gla_s[dr, h].astype(BF16)) for h in heads]
            it["u_gla"] = [_dot_tn(it["v16"][:, vsl(h)], it["kd16"][:, ksl(h)]) for h in heads]
            y_in = []
            for g in groups:
                parts = []
                for pr in range(SSD_GW // BD_W):
                    ls = slice(g * SSD_GW + pr * BD_W, g * SSD_GW + (pr + 1) * BD_W)
                    xh = it["xs16"][:, ls]
                    bd = jnp.where(blk_mask, jnp.concatenate([xh] * BD_HEADS, axis=0), jnp.zeros((), BF16))
                    parts.append(_dot(it["m16"][g][:, pr * BD_W:(pr + 1) * BD_W], bd))
                y_in.append(jnp.concatenate(parts, axis=1))
            it["y_in"] = y_in
            it["y_st"] = [_dot(it["cm16"][g], ssd_s[dr, g].astype(BF16)) for g in groups]
            it["u_ssd"] = [_dot_tn(it["bm16"][g], it["xdec16"][:, gsl(g)]) for g in groups]

        for it in insts:
            dr, rs, last = it["dr"], it["rs"], it["last"]
            for h in heads:
                it["yb_ref"][0, rs, vsl(h)] = it["o_in"][h] + it["o_st"][h]
                gla_s[dr, h] = gla_s[dr, h] * it["etot"][:, ksl(h)] + it["u_gla"][h]
            for g in groups:
                ecum = it["ecum_x"][:, gsl(g)]
                it["yc_ref"][0, rs, gsl(g)] = it["y_in"][g] + it["y_st"][g] * ecum
                ssd_s[dr, g] = ssd_s[dr, g] * ecum[last:last + 1, :] + it["u_ssd"][g]


def _scan(u, wup, gb, e3, dtb, na, *, lc):
    b, _, tt, _ = u.shape
    tb = SCAN_TB
    nt = tt // tb
    nctx = lc // tb

    def fblk(i):
        return i

    def bblk(i):
        return _bwd_block(i, nctx, nt)

    def fmap(cb):
        return lambda bi, i: (bi, i, cb)

    def bmap(cb):
        return lambda bi, i: (bi, _bwd_block(i, nctx, nt), cb)

    def tok_specs(blk, dr):
        return [
            _seg_spec(tb, C_Q, GLA_DK, blk), _seg_spec(tb, C_K, GLA_DK, blk), _seg_spec(tb, C_V, GLA_DV, blk),
            _seg_spec(tb, C_XBC, SSD_XBC, blk),
            pl.BlockSpec((1, 1, tb, SM_W), lambda bi, i: (bi, C_SM // TN, blk(i), dr)),
        ]

    def const_spec(a):
        nd = a.ndim
        return pl.BlockSpec(a.shape, lambda bi, i: (0,) * nd)

    consts = (wup, gb, e3, dtb, na)
    return pl.pallas_call(
        _scan_kernel,
        grid=(b, nt),
        in_specs=tok_specs(fblk, 0) + tok_specs(bblk, 1) + [const_spec(a) for a in consts],
        out_specs=[
            pl.BlockSpec((1, tb, GLA_DV), fmap(0)),
            pl.BlockSpec((1, tb, SSD_INNER), fmap(0)),
            pl.BlockSpec((1, tb, GLA_DV), bmap(0)),
            pl.BlockSpec((1, tb, SSD_INNER), bmap(0)),
        ],
        out_shape=[
            jax.ShapeDtypeStruct((b, tt, GLA_DV), F32),
            jax.ShapeDtypeStruct((b, tt, SSD_INNER), F32),
            jax.ShapeDtypeStruct((b, tt, GLA_DV), F32),
            jax.ShapeDtypeStruct((b, tt, SSD_INNER), F32),
        ],
        scratch_shapes=[
            pltpu.VMEM((2, GLA_HEADS, GLA_DVH, GLA_DKH), F32),
            pltpu.VMEM((2, SSD_G, SSD_N, SSD_GW), F32),
        ],
        compiler_params=pltpu.CompilerParams(
            dimension_semantics=("arbitrary", "arbitrary"), vmem_limit_bytes=VMEM_LIMIT),
        name="gla_ssd_scan",
    )(u, u, u, u, u, u, u, u, u, u, *consts)


FIN_TM = 128


def _group_rms(x, width):
    parts = []
    for s in range(x.shape[1] // width):
        xs = x[:, s * width:(s + 1) * width]
        parts.append(xs * lax.rsqrt(jnp.mean(xs * xs, axis=-1, keepdims=True) + EPS))
    return jnp.concatenate(parts, axis=1)


def _fin_kernel(yaf_ref, yab_ref, ybf_ref, ybb_ref, ycf_ref, ycb_ref, lg_ref, gg_ref, z_ref, mg_ref,
                xs_ref, xc_ref, xl_ref, ml_ref, mc_ref, gng_ref, sng_ref, sd_ref, pog_ref,
                wpa_ref, wpb_ref, wpc_ref, wo_ref, oc_ref, ol_ref, *, nctx):
    i = pl.program_id(1)
    d = D_MODEL
    rows = slice(None)
    ya = yaf_ref[0] + yab_ref[0]
    pa = _dot((ya * _silu(_wide(lg_ref, rows))).astype(BF16), wpa_ref[...])
    yb = ybf_ref[0] + ybb_ref[0]
    ob = (_group_rms(yb, GLA_DVH) * gng_ref[...]) * _silu(_wide(gg_ref, rows))
    pb = _dot(ob.astype(BF16), wpb_ref[...])
    yc = (ycf_ref[0] + ycb_ref[0]) + sd_ref[...] * _wide(xs_ref, rows)
    oc = _group_rms(yc * _silu(_wide(z_ref, rows)), SSD_GW) * sng_ref[...]
    pc = _dot(oc.astype(BF16), wpc_ref[...])
    gates = _sigmoid(_wide(mg_ref, rows))
    merged = gates[:, 0:d] * pa + gates[:, d:2 * d] * pb + gates[:, 2 * d:3 * d] * pc
    out = _dot(merged.astype(BF16), wo_ref[...])
    normed = _group_rms(out, d) * pog_ref[...]

    @pl.when(i < nctx)
    def _():
        oc_ref[0] = xc_ref[0] + mc_ref[:, 2 * d:3 * d] * normed

    @pl.when(i >= nctx)
    def _():
        ol_ref[0] = xl_ref[0] + ml_ref[0][:, 2 * d:3 * d] * normed


def _finish(ya_f, ya_b, yb_f, yb_b, yc_f, yc_b, u, x_ctx, x_lat, mod_lat, mod_ctx,
            gng, sng, sd_x, post_g, w_pa, w_pb, w_pc, w_out):
    b, lc, d = x_ctx.shape
    seq = x_lat.shape[1]
    tm = FIN_TM
    nctx = lc // tm
    ctx_spec = pl.BlockSpec((1, tm, d), lambda bi, i: (bi, jnp.minimum(i, nctx - 1), 0))
    lat_spec = pl.BlockSpec((1, tm, d), lambda bi, i: (bi, jnp.maximum(i - nctx, 0), 0))

    def tok(width):
        return pl.BlockSpec((1, tm, width), lambda bi, i: (bi, i, 0))

    def blk(i):
        return i

    def const_spec(a):
        nd = a.ndim
        return pl.BlockSpec(a.shape, lambda bi, i: (0,) * nd)

    consts = (gng, sng, sd_x, post_g, w_pa, w_pb, w_pc, w_out)
    return pl.pallas_call(
        functools.partial(_fin_kernel, nctx=nctx),
        grid=(b, (lc + seq) // tm),
        in_specs=[
            tok(d), tok(d), tok(GLA_DV), tok(GLA_DV), tok(SSD_INNER), tok(SSD_INNER),
            _seg_spec(tm, C_LG, d, blk), _seg_spec(tm, C_GG, GLA_DV, blk), _seg_spec(tm, C_Z, SSD_INNER, blk),
            _seg_spec(tm, C_MRG, 3 * d, blk), _seg_spec(tm, C_XBC, SSD_INNER, blk), ctx_spec, lat_spec,
            pl.BlockSpec((1, 1, 3 * d), lambda bi, i: (bi, 0, 0)),
            pl.BlockSpec((1, 3 * d), lambda bi, i: (0, 0)),
        ] + [const_spec(a) for a in consts],
        out_specs=[ctx_spec, lat_spec],
        out_shape=[jax.ShapeDtypeStruct((b, lc, d), F32), jax.ShapeDtypeStruct((b, seq, d), F32)],
        compiler_params=pltpu.CompilerParams(
            dimension_semantics=("arbitrary", "arbitrary"), vmem_limit_bytes=VMEM_LIMIT),
        name="finish",
    )(ya_f, ya_b, yb_f, yb_b, yc_f, yc_b, u, u, u, u, u, x_ctx, x_lat, mod_lat, mod_ctx, *consts)


def _reorder_in_weights(w_in):
    d = w_in.shape[0]
    o = 0
    seg = {}
    for name, wd in (("xa", D_MODEL), ("lg", D_MODEL), ("q", GLA_DK), ("k", GLA_DK), ("v", GLA_DV),
                     ("gg", GLA_DV), ("low", 2 * GLA_RANK), ("z", SSD_INNER), ("xbc", SSD_XBC),
                     ("dt", 2 * SSD_HEADS), ("mrg", 3 * D_MODEL)):
        seg[name] = w_in[:, o:o + wd]
        o += wd
    pad = jnp.zeros((d, SM_W - GLA_RANK - SSD_HEADS), w_in.dtype)
    cols = [seg["xbc"], seg["mrg"], seg["z"], seg["xa"], seg["v"], seg["lg"], seg["gg"],
            seg["q"], seg["k"]]
    for dr in range(2):
        cols += [seg["dt"][:, dr * SSD_HEADS:(dr + 1) * SSD_HEADS],
                 seg["low"][:, dr * GLA_RANK:(dr + 1) * GLA_RANK], pad]
    cols.append(jnp.zeros((d, N_TOT - C_SM - 2 * SM_W), w_in.dtype))
    return jnp.concatenate(cols, axis=1)


def _conv_tables(conv_a_w, conv_a_b, conv_c_w, conv_c_b):
    cw = jnp.zeros((CONV_W, N_TOT), F32)
    cw = cw.at[:, C_XBC:C_XBC + SSD_XBC].set(conv_c_w).at[:, C_XA:C_XA + D_MODEL].set(conv_a_w)
    cb = jnp.zeros((1, N_TOT), F32)
    cb = cb.at[0, C_XBC:C_XBC + SSD_XBC].set(conv_c_b).at[0, C_XA:C_XA + D_MODEL].set(conv_a_b)
    return cw, cb


def _scan_tables(gla_alpha_up, ssd_dt_bias, ssd_a_log):
    wup = jnp.zeros((2, SM_W, GLA_DK), F32).at[:, SM_LOW:SM_LOW + GLA_RANK, :].set(gla_alpha_up)
    dtb = jnp.zeros((2, SM_W), F32).at[:, SM_DT:SM_DT + SSD_HEADS].set(ssd_dt_bias)
    a_log = jnp.zeros((2, SM_W), F32).at[:, SM_DT:SM_DT + SSD_HEADS].set(ssd_a_log)
    valid = np.zeros((2, SM_W), np.float32)
    valid[:, SM_DT:SM_DT + SSD_HEADS] = 1.0
    e3 = np.zeros((SM_W, SSD_INNER), np.float32)
    for k in range(3 * SSD_HEADS):
        h = k % SSD_HEADS
        e3[k, h * SSD_P:(h + 1) * SSD_P] = 1.0
    return wup.astype(BF16), jnp.asarray(e3, BF16), dtb, a_log, jnp.asarray(valid)


def _to_col_major(h, rows):
    b, length, dm = h.shape
    return h.reshape(b, rows, GRID_W, dm).transpose(0, 2, 1, 3).reshape(b, length, dm)


def _from_col_major(h, rows):
    b, length, dm = h.shape
    return h.reshape(b, GRID_W, rows, dm).transpose(0, 2, 1, 3).reshape(b, length, dm)


def _neg_exp_kernel(a_ref, v_ref, o_ref):
    o_ref[...] = -jnp.exp(a_ref[...]) * v_ref[...]


def kernel(x, c, ctx, c_ctx, ada_w, ada_b, pre_g, post_g, w_in, conv_a_w, conv_a_b, lru_wr, lru_br,
           lru_wi, lru_bi, lru_lam, gla_alpha_up, gla_alpha_b, gla_norm_g, conv_c_w, conv_c_b,
           ssd_a_log, ssd_dt_bias, ssd_d, ssd_norm_g, w_pa, w_pb, w_pc, w_out):
    b, seq, d = x.shape
    lc = ctx.shape[1]
    depth = w_in.shape[0]
    rows = seq // GRID_W
    assert d == D_MODEL and lc % SCAN_TB == 0 and seq % SCAN_TB == 0 and b % 8 == 0

    pad_rows = (-(b + 1)) % 8
    cc = jnp.concatenate([c, c_ctx[None, :], jnp.zeros((pad_rows, d), F32)], axis=0)
    mods = _adaln(cc, ada_w, ada_b)

    w_in16 = w_in.astype(BF16)
    x_lat, x_ctx = x, ctx
    for l in range(depth):
        col_major = l % 2 == 1
        line = rows if col_major else GRID_W
        mod_lat = mods[l, :b].reshape(b, 1, 3 * d)
        mod_ctx = mods[l, b:b + 1]
        lat = _to_col_major(x_lat, rows) if col_major else x_lat

        cw, cb = _conv_tables(conv_a_w[l], conv_a_b[l], conv_c_w[l], conv_c_b[l])
        u = _in_proj(x_ctx, lat, mod_lat, mod_ctx, pre_g[l].reshape(1, d), _reorder_in_weights(w_in16[l]),
                     cw, cb, line=line)

        ya_f, ya_b = _lru(u, lru_wr[l].astype(BF16), lru_wi[l].astype(BF16), lru_br[l], lru_bi[l],
                          lru_lam[l], lc=lc)

        wup, e3, dtb, a_log, valid = _scan_tables(gla_alpha_up[l], ssd_dt_bias[l], ssd_a_log[l])
        na = pl.pallas_call(_neg_exp_kernel, out_shape=jax.ShapeDtypeStruct(a_log.shape, F32),
                            name="ssd_neg_a")(a_log, valid)
        yb_f, yc_f, yb_b, yc_b = _scan(u, wup, gla_alpha_b[l], e3, dtb, na, lc=lc)

        x_ctx, lat = _finish(ya_f, ya_b, yb_f, yb_b, yc_f, yc_b, u, x_ctx, lat, mod_lat, mod_ctx,
                             jnp.tile(gla_norm_g[l], GLA_HEADS).reshape(1, GLA_DV),
                             ssd_norm_g[l].reshape(1, SSD_INNER),
                             jnp.repeat(ssd_d[l], SSD_P).reshape(1, SSD_INNER),
                             post_g[l].reshape(1, d),
                             w_pa[l].astype(BF16), w_pb[l].astype(BF16), w_pc[l].astype(BF16),
                             w_out[l].astype(BF16))
        x_lat = _from_col_major(lat, rows) if col_major else lat
    return x_lat
```

```python
import functools

import jax
import jax.numpy as jnp
import numpy as np
from jax import lax
from jax.experimental import pallas as pl
from jax.experimental.pallas import tpu as pltpu

F32 = jnp.float32
BF16 = jnp.bfloat16

D_MODEL = 1024
GRID_W = 64
CONV_W = 4
EPS = 1e-6
LRU_BW = 128
LRU_C = 8.0
GLA_HEADS = 4
GLA_DK = 512
GLA_DV = 1024
GLA_DKH = 128
GLA_DVH = 256
GLA_RANK = 16
GLA_TAU = 16.0
SSD_INNER = 2048
SSD_P = 64
SSD_HEADS = 32
SSD_N = 128
SSD_G = 4
SSD_GW = SSD_INNER // SSD_G
SSD_XBC = SSD_INNER + 2 * SSD_G * SSD_N
CHUNK = 64

TN = 512
C_XBC, C_MRG, C_Z, C_XA, C_V, C_LG, C_GG, C_Q, C_K, C_SM = (
    0, 3072, 6144, 8192, 9216, 10240, 11264, 12288, 12800, 13312)
N_TOT = C_SM + TN

VMEM_LIMIT = 56 * 1024 * 1024


def _sigmoid(x):
    return 0.5 * jnp.tanh(0.5 * x) + 0.5


def _silu(x):
    return x * _sigmoid(x)


def _softplus(x):
    return jnp.maximum(x, 0.0) + jnp.log1p(jnp.exp(-jnp.abs(x)))


def _dot(a, b):
    return jnp.dot(a, b, preferred_element_type=F32)


def _dot_nt(a, b):
    return lax.dot_general(a, b, (((1,), (1,)), ((), ())), preferred_element_type=F32)


def _dot_tn(a, b):
    return lax.dot_general(a, b, (((0,), (0,)), ((), ())), preferred_element_type=F32)


def _adaln_kernel(c_ref, w_ref, b_ref, o_ref):
    c = c_ref[...]
    o_ref[0] = _dot(_silu(c).astype(BF16), w_ref[0]) + b_ref[0]


def _adaln(cc, ada_w, ada_b):
    depth, d, n3 = ada_w.shape
    rows = cc.shape[0]
    tn = 1024
    return pl.pallas_call(
        _adaln_kernel,
        grid=(depth, n3 // tn),
        in_specs=[
            pl.BlockSpec((rows, d), lambda l, n: (0, 0)),
            pl.BlockSpec((1, d, tn), lambda l, n: (l, 0, n)),
            pl.BlockSpec((1, 1, tn), lambda l, n: (l, 0, n)),
        ],
        out_specs=pl.BlockSpec((1, rows, tn), lambda l, n: (l, 0, n)),
        out_shape=jax.ShapeDtypeStruct((depth, rows, n3), F32),
        name="adaln",
    )(cc, ada_w.astype(BF16), ada_b.reshape(depth, 1, n3))


CONV_PAD = 8
CONV_SLABS = TN // 128
CONV_PARTS = 4

def _in_kernel(xc_ref, xl_ref, ml_ref, mc_ref, pg_ref, w_ref, cw_ref, cb_ref, u_ref, h_s, conv_s, *, line):
    j = pl.program_id(1)
    lc = xc_ref.shape[1]
    tt = lc + xl_ref.shape[1]
    d = xc_ref.shape[2]
    nlines = (tt - lc) // line
    pitch = line + CONV_PAD
    lat0 = CONV_PAD + lc

    @pl.when(j == 0)
    def _():
        def norm_mod(x, mod):
            y = x * lax.rsqrt(jnp.mean(x * x, axis=-1, keepdims=True) + EPS) * pg_ref[...]
            return (y * (1.0 + mod[:, d:2 * d]) + mod[:, 0:d]).astype(BF16)
        h_s[0:lc, :] = norm_mod(xc_ref[0], mc_ref[...])
        h_s[lc:tt, :] = norm_mod(xl_ref[0], ml_ref[0])
        zeros = jnp.zeros((CONV_SLABS, CONV_PAD, 128), F32)
        conv_s[:, 0:CONV_PAD, :] = zeros
        for i in range(nlines + 1):
            r = lat0 + i * pitch
            conv_s[:, r:r + CONV_PAD, :] = zeros

    def conv(act):
        def taps(n, r, rows):
            ls = slice(n * 128, (n + 1) * 128)
            cw = cw_ref[:, ls]
            out = cb_ref[:, ls] + cw[1:2, :] * conv_s[n, r:r + rows, :]
            out = out + cw[0:1, :] * conv_s[n, r - 1:r - 1 + rows, :]
            out = out + cw[2:3, :] * conv_s[n, r + 1:r + 1 + rows, :]
            out = out + cw[3:4, :] * conv_s[n, r + 2:r + 2 + rows, :]
            return act(out)

        lines = [(0, CONV_PAD, lc)] + [(lc + i * line, lat0 + i * pitch + CONV_PAD, line) for i in range(nlines)]
        bounds = [0] + [lc + ((p * nlines) // CONV_PARTS) * line for p in range(1, CONV_PARTS)] + [tt]
        for p in range(CONV_PARTS):
            t0, t1 = bounds[p], bounds[p + 1]
            up = _dot(h_s[t0:t1, :], w_ref[...])
            part = [ln for ln in lines if t0 <= ln[0] < t1]
            for n in range(CONV_SLABS):
                for t, r, rows in part:
                    conv_s[n, r:r + rows, :] = up[t - t0:t - t0 + rows, n * 128:(n + 1) * 128]
            for t, r, rows in part:
                for n in range(CONV_SLABS):
                    u_ref[0, 0, t:t + rows, n * 128:(n + 1) * 128] = taps(n, r, rows)

    is_xbc = j < SSD_XBC // TN
    is_xa = (j >= C_XA // TN) & (j < (C_XA + D_MODEL) // TN)

    @pl.when(is_xbc)
    def _():
        conv(_silu)

    @pl.when(is_xa)
    def _():
        conv(lambda v: v)

    @pl.when(jnp.logical_not(is_xbc | is_xa))
    def _():
        u_ref[0, 0] = _dot(h_s[...], w_ref[...])


def _in_proj(x_ctx, x_lat, mod_lat, mod_ctx, pre_g, w_in_p, cw, cb, *, line):
    b, lc, d = x_ctx.shape
    seq = x_lat.shape[1]
    tt = lc + seq
    assert line % 8 == 0 and seq % line == 0 and lc % 8 == 0
    conv_rows = CONV_PAD + lc + (seq // line) * (line + CONV_PAD) + CONV_PAD
    return pl.pallas_call(
        functools.partial(_in_kernel, line=line),
        grid=(b, N_TOT // TN),
        in_specs=[
            pl.BlockSpec((1, lc, d), lambda i, j: (i, 0, 0)),
            pl.BlockSpec((1, seq, d), lambda i, j: (i, 0, 0)),
            pl.BlockSpec((1, 1, 3 * d), lambda i, j: (i, 0, 0)),
            pl.BlockSpec((1, 3 * d), lambda i, j: (0, 0)),
            pl.BlockSpec((1, d), lambda i, j: (0, 0)),
            pl.BlockSpec((d, TN), lambda i, j: (0, j)),
            pl.BlockSpec((CONV_W, TN), lambda i, j: (0, j)),
            pl.BlockSpec((1, TN), lambda i, j: (0, j)),
        ],
        out_specs=pl.BlockSpec((1, 1, tt, TN), lambda i, j: (i, j, 0, 0)),
        out_shape=jax.ShapeDtypeStruct((b, N_TOT // TN, tt, TN), F32),
        scratch_shapes=[pltpu.VMEM((tt, d), BF16), pltpu.VMEM((CONV_SLABS, conv_rows, 128), F32)],
        compiler_params=pltpu.CompilerParams(
            dimension_semantics=("arbitrary", "arbitrary"), vmem_limit_bytes=VMEM_LIMIT),
        name="in_proj",
    )(x_ctx, x_lat, mod_lat, mod_ctx, pre_g, w_in_p, cw, cb)


LRU_TC = 64
LRU_WT = 512
LRU_PITCH = LRU_TC + 8


def _lru_kernel(xf_ref, xb_ref, wr_ref, wi_ref, br_ref, bi_ref, lam_ref, yf_ref, yb_ref,
                a_s, b_s, o_s, h_s):
    i = pl.program_id(1)
    nb, _, tc, wt = xf_ref.shape

    @pl.when(i == 0)
    def _():
        h_s[...] = jnp.zeros_like(h_s)

    def gates(bi, carry):
        r0 = pl.multiple_of(bi * LRU_PITCH, 8)
        for dr, x_ref in enumerate((xf_ref, xb_ref)):
            for n in range(wt // LRU_BW):
                sl = slice(n * LRU_BW, (n + 1) * LRU_BW)
                xs = x_ref[bi, 0, :, sl]
                xs16 = xs.astype(BF16)
                r = _sigmoid(_dot(xs16, wr_ref[dr, n]) + br_ref[dr:dr + 1, sl])
                g = _sigmoid(_dot(xs16, wi_ref[dr, n]) + bi_ref[dr:dr + 1, sl])
                nla = (LRU_C * _softplus(-lam_ref[dr:dr + 1, sl])) * r
                a = jnp.exp(-nla)
                a_s[dr, n, pl.ds(r0, tc), :] = a
                b_s[dr, n, pl.ds(r0, tc), :] = jnp.sqrt(jnp.tanh(nla) * (a * a + 1.0)) * (g * xs)
        return carry
    lax.fori_loop(0, nb, gates, 0, unroll=2)

    nl = wt // LRU_BW
    hf = [h_s[0, n] for n in range(nl)]
    hb = [h_s[1, n] for n in range(nl)]
    for t in range(tc):
        rows_f = pl.ds(t, nb, stride=LRU_PITCH)
        rows_b = pl.ds(tc - 1 - t, nb, stride=LRU_PITCH)
        for n in range(nl):
            hf[n] = a_s[0, n, rows_f, :] * hf[n] + b_s[0, n, rows_f, :]
            hb[n] = a_s[1, n, rows_b, :] * hb[n] + b_s[1, n, rows_b, :]
            o_s[0, n, rows_f, :] = hf[n]
            o_s[1, n, rows_b, :] = hb[n]
    for n in range(nl):
        h_s[0, n] = hf[n]
        h_s[1, n] = hb[n]

    def emit(bi, carry):
        r0 = pl.multiple_of(bi * LRU_PITCH, 8)
        for n in range(nl):
            sl = slice(n * LRU_BW, (n + 1) * LRU_BW)
            yf_ref[bi, :, sl] = o_s[0, n, pl.ds(r0, tc), :]
            yb_ref[bi, :, sl] = o_s[1, n, pl.ds(r0, tc), :]
        return carry
    lax.fori_loop(0, nb, emit, 0)


def _bwd_block(i, nctx, ntot):
    return jnp.where(i < nctx, nctx - 1 - i, ntot - 1 + nctx - i)


def _lru(u, wr, wi, br, bi, lam, *, lc):
    b, _, tt, _ = u.shape
    w = D_MODEL
    nt = tt // LRU_TC
    nctx = lc // LRU_TC
    assert LRU_WT == TN
    c0 = C_XA // TN
    nblk = LRU_WT // LRU_BW
    y_spec_f = pl.BlockSpec((b, LRU_TC, LRU_WT), lambda j, i: (0, i, j))
    y_spec_b = pl.BlockSpec((b, LRU_TC, LRU_WT), lambda j, i: (0, _bwd_block(i, nctx, nt), j))
    return pl.pallas_call(
        _lru_kernel,
        grid=(w // LRU_WT, nt),
        in_specs=[
            pl.BlockSpec((b, 1, LRU_TC, LRU_WT), lambda j, i: (0, c0 + j, i, 0)),
            pl.BlockSpec((b, 1, LRU_TC, LRU_WT), lambda j, i: (0, c0 + j, _bwd_block(i, nctx, nt), 0)),
            pl.BlockSpec((2, nblk, LRU_BW, LRU_BW), lambda j, i: (0, j, 0, 0)),
            pl.BlockSpec((2, nblk, LRU_BW, LRU_BW), lambda j, i: (0, j, 0, 0)),
            pl.BlockSpec((2, LRU_WT), lambda j, i: (0, j)),
            pl.BlockSpec((2, LRU_WT), lambda j, i: (0, j)),
            pl.BlockSpec((2, LRU_WT), lambda j, i: (0, j)),
        ],
        out_specs=[y_spec_f, y_spec_b],
        out_shape=[jax.ShapeDtypeStruct((b, tt, w), F32)] * 2,
        scratch_shapes=[
            pltpu.VMEM((2, nblk, b * LRU_PITCH, LRU_BW), F32),
            pltpu.VMEM((2, nblk, b * LRU_PITCH, LRU_BW), F32),
            pltpu.VMEM((2, nblk, b * LRU_PITCH, LRU_BW), F32),
            pltpu.VMEM((2, nblk, b, LRU_BW), F32),
        ],
        compiler_params=pltpu.CompilerParams(
            dimension_semantics=("arbitrary", "arbitrary"), vmem_limit_bytes=VMEM_LIMIT),
        name="lru_scan",
    )(u, u, wr, wi, br, bi, lam)


SCAN_TB = 256
SM_W = 128
SM_DT = 0
SM_LOW = SSD_HEADS
BD_HEADS = 4
BD_W = BD_HEADS * SSD_P
LOG_FLOOR = -1e30


T_SSD_B = SSD_INNER // TN
T_SSD_C = T_SSD_B + 1


def _seg_spec(rows, c0, width, blk):
    nt = width // TN
    assert width % TN == 0 and (c0 // TN) % nt == 0
    return pl.BlockSpec((1, nt, rows, TN), lambda bi, i: (bi, c0 // width, blk(i), 0))


def _wide(ref, rs, ntiles=None):
    nt = ref.shape[1] if ntiles is None else ntiles
    return jnp.concatenate([ref[0, t, rs, :] for t in range(nt)], axis=1)


def _split3(x):
    hi = x.astype(BF16).astype(F32)
    r = x - hi
    mid = r.astype(BF16).astype(F32)
    return hi, mid, r - mid


def _cumsum_rows(tri16, x):
    n = x.shape[1]
    p = _dot(tri16, jnp.concatenate([s.astype(BF16) for s in _split3(x)], axis=1))
    return (p[:, 0:n] + p[:, n:2 * n]) + p[:, 2 * n:3 * n]


def _pack3(x, lane):
    hi, mid, lo = _split3(x)
    h = SSD_HEADS
    return jnp.where(lane < h, hi,
                     jnp.where(lane < 2 * h, pltpu.roll(mid, h, 1),
                               jnp.where(lane < 3 * h, pltpu.roll(lo, 2 * h, 1), 0.0)))


def _scan_kernel(qf_ref, kf_ref, vf_ref, xf_ref, sf_ref, qb_ref, kb_ref, vb_ref, xb_ref, sb_ref,
                 wup_ref, gb_ref, e3_ref, dtb_ref, na_ref,
                 ybf_ref, ycf_ref, ybb_ref, ycb_ref, gla_s, ssd_s):
    i = pl.program_id(1)

    @pl.when(i == 0)
    def _():
        gla_s[...] = jnp.zeros_like(gla_s)
        ssd_s[...] = jnp.zeros_like(ssd_s)

    tb = qf_ref.shape[2]
    nch = tb // CHUNK
    row = lax.broadcasted_iota(jnp.int32, (CHUNK, CHUNK), 0)
    col = lax.broadcasted_iota(jnp.int32, (CHUNK, CHUNK), 1)
    row_x = lax.broadcasted_iota(jnp.int32, (CHUNK, SSD_INNER), 0)
    col_x = lax.broadcasted_iota(jnp.int32, (CHUNK, SSD_INNER), 1) & (CHUNK - 1)
    r2 = lax.broadcasted_iota(jnp.int32, (BD_W, BD_W), 0)
    c2 = lax.broadcasted_iota(jnp.int32, (BD_W, BD_W), 1)
    blk_mask = (r2 // CHUNK) == (c2 // SSD_P)
    lane = lax.broadcasted_iota(jnp.int32, (4 * CHUNK, SM_W), 1)
    prow = lax.broadcasted_iota(jnp.int32, (3 * CHUNK, SM_W), 0)
    plane = lax.broadcasted_iota(jnp.int32, (3 * CHUNK, SM_W), 1)
    ones_part = jnp.where((prow < CHUNK) & (plane < 3 * SSD_HEADS), 1.0, 0.0).astype(BF16)
    heads = range(GLA_HEADS)
    groups = range(SSD_G)

    def ksl(h):
        return slice(h * GLA_DKH, (h + 1) * GLA_DKH)

    def vsl(h):
        return slice(h * GLA_DVH, (h + 1) * GLA_DVH)

    def gsl(g):
        return slice(g * SSD_GW, (g + 1) * SSD_GW)

    dirs = (
        (0, qf_ref, kf_ref, vf_ref, xf_ref, sf_ref, ybf_ref, ycf_ref),
        (1, qb_ref, kb_ref, vb_ref, xb_ref, sb_ref, ybb_ref, ycb_ref),
    )
    for step in range(nch):
        insts = []
        for dr, q_ref, k_ref, v_ref, x_ref, s_ref, yb_ref, yc_ref in dirs:
            fwd = dr == 0
            c = step if fwd else nch - 1 - step
            causal = (row >= col) if fwd else (row <= col)
            insts.append(dict(
                dr=dr, rs=slice(c * CHUNK, (c + 1) * CHUNK), last=CHUNK - 1 if fwd else 0,
                causal=causal, causal_x=(row_x >= col_x) if fwd else (row_x <= col_x),
                tri16=jnp.where(causal, 1.0, 0.0).astype(BF16),
                q_ref=q_ref, k_ref=k_ref, v_ref=v_ref, x_ref=x_ref, s_ref=s_ref, yb_ref=yb_ref, yc_ref=yc_ref))

        for it in insts:
            dr, rs, x_ref = it["dr"], it["rs"], it["x_ref"]
            it["sm"] = it["s_ref"][0, 0, rs, :]
            it["z"] = _dot(it["sm"].astype(BF16), wup_ref[dr])
            it["bm16"] = [x_ref[0, T_SSD_B, rs, g * SSD_N:(g + 1) * SSD_N].astype(BF16) for g in groups]
            it["cm16"] = [x_ref[0, T_SSD_C, rs, g * SSD_N:(g + 1) * SSD_N].astype(BF16) for g in groups]
            it["cb"] = [_dot_nt(it["cm16"][g], it["bm16"][g]) for g in groups]

        for it in insts:
            dr, rs, last = it["dr"], it["rs"], it["last"]
            logg = -_softplus(-(it["z"] + gb_ref[dr:dr + 1, :])) * (1.0 / GLA_TAU)
            bc = _cumsum_rows(it["tri16"], logg)
            btot = bc[last:last + 1, :]
            k = it["k_ref"][0, 0, rs, :]
            it["qe16"] = ((it["q_ref"][0, 0, rs, :] * (GLA_DKH ** -0.5)) * jnp.exp(bc)).astype(BF16)
            it["ke16"] = (k * jnp.exp(-bc)).astype(BF16)
            it["kd16"] = (k * jnp.exp(btot - bc)).astype(BF16)
            it["etot"] = jnp.exp(btot)
            it["v16"] = _wide(it["v_ref"], rs).astype(BF16)

            dt = _softplus(it["sm"] + dtb_ref[dr:dr + 1, :])
            cum = _cumsum_rows(it["tri16"], dt * na_ref[dr:dr + 1, :])
            clast = cum[last:last + 1, :]
            wrow = jnp.maximum(jnp.log(dt), LOG_FLOOR) - cum
            packed = _pack3(jnp.concatenate([cum, jnp.exp(cum), dt * jnp.exp(clast - cum), wrow], axis=0), lane)
            pt = packed[3 * CHUNK:4 * CHUNK].T
            pt2 = jnp.concatenate([pt, pt], axis=1).astype(BF16)
            rm = jnp.concatenate([pt2] * (SSD_INNER // 128), axis=1) * e3_ref[...]
            it["a16"] = jnp.concatenate([packed[0:3 * CHUNK].astype(BF16), ones_part], axis=1)
            it["rhs"] = jnp.concatenate([e3_ref[...], rm], axis=0)

        for it in insts:
            it["big"] = _dot(it["a16"], it["rhs"])
            it["att"] = [_dot_nt(it["qe16"][:, ksl(h)], it["ke16"][:, ksl(h)]) for h in heads]

        for it in insts:
            big = it["big"]
            lmat = jnp.exp(jnp.where(it["causal_x"], big[0:CHUNK], -jnp.inf))
            m16 = []
            for g in groups:
                cb2 = jnp.concatenate([it["cb"][g], it["cb"][g]], axis=1)
                m16.append(jnp.concatenate(
                    [cb2 * lmat[:, g * SSD_GW + q * 128:g * SSD_GW + (q + 1) * 128] for q in range(SSD_GW // 128)],
                    axis=1).astype(BF16))
            it["m16"] = m16
            it["ecum_x"] = big[CHUNK:2 * CHUNK]
            xs = _wide(it["x_ref"], it["rs"], SSD_INNER // TN)
            it["xs16"] = xs.astype(BF16)
            it["xdec16"] = (xs * big[2 * CHUNK:3 * CHUNK]).astype(BF16)
            it["att16"] = [jnp.where(it["causal"], a, 0.0).astype(BF16) for a in it["att"]]

        for it in insts:
            dr = it["dr"]
            it["o_in"] = [_dot(it["att16"][h], it["v16"][:, vsl(h)]) for h in heads]
            it["o_st"] = [_dot_nt(it["qe16"][:, ksl(h)], gla_s[dr, h].astype(BF16)) for h in heads]
            it["u_gla"] = [_dot_tn(it["v16"][:, vsl(h)], it["kd16"][:, ksl(h)]) for h in heads]
            y_in = []
            for g in groups:
                parts = []
                for pr in range(SSD_GW // BD_W):
                    ls = slice(g * SSD_GW + pr * BD_W, g * SSD_GW + (pr + 1) * BD_W)
                    xh = it["xs16"][:, ls]
                    bd = jnp.where(blk_mask, jnp.concatenate([xh] * BD_HEADS, axis=0), jnp.zeros((), BF16))
                    parts.append(_dot(it["m16"][g][:, pr * BD_W:(pr + 1) * BD_W], bd))
                y_in.append(jnp.concatenate(parts, axis=1))
            it["y_in"] = y_in
            it["y_st"] = [_dot(it["cm16"][g], ssd_s[dr, g].astype(BF16)) for g in groups]
            it["u_ssd"] = [_dot_tn(it["bm16"][g], it["xdec16"][:, gsl(g)]) for g in groups]

        for it in insts:
            dr, rs, last = it["dr"], it["rs"], it["last"]
            for h in heads:
                it["yb_ref"][0, rs, vsl(h)] = it["o_in"][h] + it["o_st"][h]
                gla_s[dr, h] = gla_s[dr, h] * it["etot"][:, ksl(h)] + it["u_gla"][h]
            for g in groups:
                ecum = it["ecum_x"][:, gsl(g)]
                it["yc_ref"][0, rs, gsl(g)] = it["y_in"][g] + it["y_st"][g] * ecum
                ssd_s[dr, g] = ssd_s[dr, g] * ecum[last:last + 1, :] + it["u_ssd"][g]


def _scan(u, wup, gb, e3, dtb, na, *, lc):
    b, _, tt, _ = u.shape
    tb = SCAN_TB
    nt = tt // tb
    nctx = lc // tb

    def fblk(i):
        return i

    def bblk(i):
        return _bwd_block(i, nctx, nt)

    def fmap(cb):
        return lambda bi, i: (bi, i, cb)

    def bmap(cb):
        return lambda bi, i: (bi, _bwd_block(i, nctx, nt), cb)

    def tok_specs(blk, dr):
        return [
            _seg_spec(tb, C_Q, GLA_DK, blk), _seg_spec(tb, C_K, GLA_DK, blk), _seg_spec(tb, C_V, GLA_DV, blk),
            _seg_spec(tb, C_XBC, SSD_XBC, blk),
            pl.BlockSpec((1, 1, tb, SM_W), lambda bi, i: (bi, C_SM // TN, blk(i), dr)),
        ]

    def const_spec(a):
        nd = a.ndim
        return pl.BlockSpec(a.shape, lambda bi, i: (0,) * nd)

    consts = (wup, gb, e3, dtb, na)
    return pl.pallas_call(
        _scan_kernel,
        grid=(b, nt),
        in_specs=tok_specs(fblk, 0) + tok_specs(bblk, 1) + [const_spec(a) for a in consts],
        out_specs=[
            pl.BlockSpec((1, tb, GLA_DV), fmap(0)),
            pl.BlockSpec((1, tb, SSD_INNER), fmap(0)),
            pl.BlockSpec((1, tb, GLA_DV), bmap(0)),
            pl.BlockSpec((1, tb, SSD_INNER), bmap(0)),
        ],
        out_shape=[
            jax.ShapeDtypeStruct((b, tt, GLA_DV), F32),
            jax.ShapeDtypeStruct((b, tt, SSD_INNER), F32),
            jax.ShapeDtypeStruct((b, tt, GLA_DV), F32),
            jax.ShapeDtypeStruct((b, tt, SSD_INNER), F32),
        ],
        scratch_shapes=[
            pltpu.VMEM((2, GLA_HEADS, GLA_DVH, GLA_DKH), F32),
            pltpu.VMEM((2, SSD_G, SSD_N, SSD_GW), F32),
        ],
        compiler_params=pltpu.CompilerParams(
            dimension_semantics=("arbitrary", "arbitrary"), vmem_limit_bytes=VMEM_LIMIT),
        name="gla_ssd_scan",
    )(u, u, u, u, u, u, u, u, u, u, *consts)


FIN_TM = 128


def _group_rms(x, width):
    parts = []
    for s in range(x.shape[1] // width):
        xs = x[:, s * width:(s + 1) * width]
        parts.append(xs * lax.rsqrt(jnp.mean(xs * xs, axis=-1, keepdims=True) + EPS))
    return jnp.concatenate(parts, axis=1)


def _fin_kernel(yaf_ref, yab_ref, ybf_ref, ybb_ref, ycf_ref, ycb_ref, lg_ref, gg_ref, z_ref, mg_ref,
                xs_ref, xc_ref, xl_ref, ml_ref, mc_ref, gng_ref, sng_ref, sd_ref, pog_ref,
                wpa_ref, wpb_ref, wpc_ref, wo_ref, oc_ref, ol_ref, *, nctx):
    i = pl.program_id(1)
    d = D_MODEL
    rows = slice(None)
    ya = yaf_ref[0] + yab_ref[0]
    pa = _dot((ya * _silu(_wide(lg_ref, rows))).astype(BF16), wpa_ref[...])
    yb = ybf_ref[0] + ybb_ref[0]
    ob = (_group_rms(yb, GLA_DVH) * gng_ref[...]) * _silu(_wide(gg_ref, rows))
    pb = _dot(ob.astype(BF16), wpb_ref[...])
    yc = (ycf_ref[0] + ycb_ref[0]) + sd_ref[...] * _wide(xs_ref, rows)
    oc = _group_rms(yc * _silu(_wide(z_ref, rows)), SSD_GW) * sng_ref[...]
    pc = _dot(oc.astype(BF16), wpc_ref[...])
    gates = _sigmoid(_wide(mg_ref, rows))
    merged = gates[:, 0:d] * pa + gates[:, d:2 * d] * pb + gates[:, 2 * d:3 * d] * pc
    out = _dot(merged.astype(BF16), wo_ref[...])
    normed = _group_rms(out, d) * pog_ref[...]

    @pl.when(i < nctx)
    def _():
        oc_ref[0] = xc_ref[0] + mc_ref[:, 2 * d:3 * d] * normed

    @pl.when(i >= nctx)
    def _():
        ol_ref[0] = xl_ref[0] + ml_ref[0][:, 2 * d:3 * d] * normed


def _finish(ya_f, ya_b, yb_f, yb_b, yc_f, yc_b, u, x_ctx, x_lat, mod_lat, mod_ctx,
            gng, sng, sd_x, post_g, w_pa, w_pb, w_pc, w_out):
    b, lc, d = x_ctx.shape
    seq = x_lat.shape[1]
    tm = FIN_TM
    nctx = lc // tm
    ctx_spec = pl.BlockSpec((1, tm, d), lambda bi, i: (bi, jnp.minimum(i, nctx - 1), 0))
    lat_spec = pl.BlockSpec((1, tm, d), lambda bi, i: (bi, jnp.maximum(i - nctx, 0), 0))

    def tok(width):
        return pl.BlockSpec((1, tm, width), lambda bi, i: (bi, i, 0))

    def blk(i):
        return i

    def const_spec(a):
        nd = a.ndim
        return pl.BlockSpec(a.shape, lambda bi, i: (0,) * nd)

    consts = (gng, sng, sd_x, post_g, w_pa, w_pb, w_pc, w_out)
    return pl.pallas_call(
        functools.partial(_fin_kernel, nctx=nctx),
        grid=(b, (lc + seq) // tm),
        in_specs=[
            tok(d), tok(d), tok(GLA_DV), tok(GLA_DV), tok(SSD_INNER), tok(SSD_INNER),
            _seg_spec(tm, C_LG, d, blk), _seg_spec(tm, C_GG, GLA_DV, blk), _seg_spec(tm, C_Z, SSD_INNER, blk),
            _seg_spec(tm, C_MRG, 3 * d, blk), _seg_spec(tm, C_XBC, SSD_INNER, blk), ctx_spec, lat_spec,
            pl.BlockSpec((1, 1, 3 * d), lambda bi, i: (bi, 0, 0)),
            pl.BlockSpec((1, 3 * d), lambda bi, i: (0, 0)),
        ] + [const_spec(a) for a in consts],
        out_specs=[ctx_spec, lat_spec],
        out_shape=[jax.ShapeDtypeStruct((b, lc, d), F32), jax.ShapeDtypeStruct((b, seq, d), F32)],
        compiler_params=pltpu.CompilerParams(
            dimension_semantics=("arbitrary", "arbitrary"), vmem_limit_bytes=VMEM_LIMIT),
        name="finish",
    )(ya_f, ya_b, yb_f, yb_b, yc_f, yc_b, u, u, u, u, u, x_ctx, x_lat, mod_lat, mod_ctx, *consts)


def _reorder_in_weights(w_in):
    d = w_in.shape[0]
    o = 0
    seg = {}
    for name, wd in (("xa", D_MODEL), ("lg", D_MODEL), ("q", GLA_DK), ("k", GLA_DK), ("v", GLA_DV),
                     ("gg", GLA_DV), ("low", 2 * GLA_RANK), ("z", SSD_INNER), ("xbc", SSD_XBC),
                     ("dt", 2 * SSD_HEADS), ("mrg", 3 * D_MODEL)):
        seg[name] = w_in[:, o:o + wd]
        o += wd
    pad = jnp.zeros((d, SM_W - GLA_RANK - SSD_HEADS), w_in.dtype)
    cols = [seg["xbc"], seg["mrg"], seg["z"], seg["xa"], seg["v"], seg["lg"], seg["gg"],
            seg["q"], seg["k"]]
    for dr in range(2):
        cols += [seg["dt"][:, dr * SSD_HEADS:(dr + 1) * SSD_HEADS],
                 seg["low"][:, dr * GLA_RANK:(dr + 1) * GLA_RANK], pad]
    cols.append(jnp.zeros((d, N_TOT - C_SM - 2 * SM_W), w_in.dtype))
    return jnp.concatenate(cols, axis=1)


def _conv_tables(conv_a_w, conv_a_b, conv_c_w, conv_c_b):
    cw = jnp.zeros((CONV_W, N_TOT), F32)
    cw = cw.at[:, C_XBC:C_XBC + SSD_XBC].set(conv_c_w).at[:, C_XA:C_XA + D_MODEL].set(conv_a_w)
    cb = jnp.zeros((1, N_TOT), F32)
    cb = cb.at[0, C_XBC:C_XBC + SSD_XBC].set(conv_c_b).at[0, C_XA:C_XA + D_MODEL].set(conv_a_b)
    return cw, cb


def _scan_tables(gla_alpha_up, ssd_dt_bias, ssd_a_log):
    wup = jnp.zeros((2, SM_W, GLA_DK), F32).at[:, SM_LOW:SM_LOW + GLA_RANK, :].set(gla_alpha_up)
    dtb = jnp.zeros((2, SM_W), F32).at[:, SM_DT:SM_DT + SSD_HEADS].set(ssd_dt_bias)
    a_log = jnp.zeros((2, SM_W), F32).at[:, SM_DT:SM_DT + SSD_HEADS].set(ssd_a_log)
    valid = np.zeros((2, SM_W), np.float32)
    valid[:, SM_DT:SM_DT + SSD_HEADS] = 1.0
    e3 = np.zeros((SM_W, SSD_INNER), np.float32)
    for k in range(3 * SSD_HEADS):
        h = k % SSD_HEADS
        e3[k, h * SSD_P:(h + 1) * SSD_P] = 1.0
    return wup.astype(BF16), jnp.asarray(e3, BF16), dtb, a_log, jnp.asarray(valid)


def _to_col_major(h, rows):
    b, length, dm = h.shape
    return h.reshape(b, rows, GRID_W, dm).transpose(0, 2, 1, 3).reshape(b, length, dm)


def _from_col_major(h, rows):
    b, length, dm = h.shape
    return h.reshape(b, GRID_W, rows, dm).transpose(0, 2, 1, 3).reshape(b, length, dm)


def _neg_exp_kernel(a_ref, v_ref, o_ref):
    o_ref[...] = -jnp.exp(a_ref[...]) * v_ref[...]


def kernel(x, c, ctx, c_ctx, ada_w, ada_b, pre_g, post_g, w_in, conv_a_w, conv_a_b, lru_wr, lru_br,
           lru_wi, lru_bi, lru_lam, gla_alpha_up, gla_alpha_b, gla_norm_g, conv_c_w, conv_c_b,
           ssd_a_log, ssd_dt_bias, ssd_d, ssd_norm_g, w_pa, w_pb, w_pc, w_out):
    b, seq, d = x.shape
    lc = ctx.shape[1]
    depth = w_in.shape[0]
    rows = seq // GRID_W
    assert d == D_MODEL and lc % SCAN_TB == 0 and seq % SCAN_TB == 0 and b % 8 == 0

    pad_rows = (-(b + 1)) % 8
    cc = jnp.concatenate([c, c_ctx[None, :], jnp.zeros((pad_rows, d), F32)], axis=0)
    mods = _adaln(cc, ada_w, ada_b)

    w_in16 = w_in.astype(BF16)
    x_lat, x_ctx = x, ctx
    for l in range(depth):
        col_major = l % 2 == 1
        line = rows if col_major else GRID_W
        mod_lat = mods[l, :b].reshape(b, 1, 3 * d)
        mod_ctx = mods[l, b:b + 1]
        lat = _to_col_major(x_lat, rows) if col_major else x_lat

        cw, cb = _conv_tables(conv_a_w[l], conv_a_b[l], conv_c_w[l], conv_c_b[l])
        u = _in_proj(x_ctx, lat, mod_lat, mod_ctx, pre_g[l].reshape(1, d), _reorder_in_weights(w_in16[l]),
                     cw, cb, line=line)

        ya_f, ya_b = _lru(u, lru_wr[l].astype(BF16), lru_wi[l].astype(BF16), lru_br[l], lru_bi[l],
                          lru_lam[l], lc=lc)

        wup, e3, dtb, a_log, valid = _scan_tables(gla_alpha_up[l], ssd_dt_bias[l], ssd_a_log[l])
        na = pl.pallas_call(_neg_exp_kernel, out_shape=jax.ShapeDtypeStruct(a_log.shape, F32),
                            name="ssd_neg_a")(a_log, valid)
        yb_f, yc_f, yb_b, yc_b = _scan(u, wup, gla_alpha_b[l], e3, dtb, na, lc=lc)

        x_ctx, lat = _finish(ya_f, ya_b, yb_f, yb_b, yc_f, yc_b, u, x_ctx, lat, mod_lat, mod_ctx,
                             jnp.tile(gla_norm_g[l], GLA_HEADS).reshape(1, GLA_DV),
                             ssd_norm_g[l].reshape(1, SSD_INNER),
                             jnp.repeat(ssd_d[l], SSD_P).reshape(1, SSD_INNER),
                             post_g[l].reshape(1, d),
                             w_pa[l].astype(BF16), w_pb[l].astype(BF16), w_pc[l].astype(BF16),
                             w_out[l].astype(BF16))
        x_lat = _from_col_major(lat, rows) if col_major else lat
    return x_lat
```

```python
import functools

import jax
import jax.numpy as jnp
import numpy as np
from jax import lax
from jax.experimental import pallas as pl
from jax.experimental.pallas import tpu as pltpu

F32 = jnp.float32
BF16 = jnp.bfloat16

D_MODEL = 1024
GRID_W = 64
CONV_W = 4
EPS = 1e-6
LRU_BW = 128
LRU_C = 8.0
GLA_HEADS = 4
GLA_DK = 512
GLA_DV = 1024
GLA_DKH = 128
GLA_DVH = 256
GLA_RANK = 16
GLA_TAU = 16.0
SSD_INNER = 2048
SSD_P = 64
SSD_HEADS = 32
SSD_N = 128
SSD_G = 4
SSD_GW = SSD_INNER // SSD_G
SSD_XBC = SSD_INNER + 2 * SSD_G * SSD_N
CHUNK = 64

TN = 512
C_XBC, C_MRG, C_Z, C_XA, C_V, C_LG, C_GG, C_Q, C_K, C_SM = (
    0, 3072, 6144, 8192, 9216, 10240, 11264, 12288, 12800, 13312)
N_TOT = C_SM + TN

VMEM_LIMIT = 56 * 1024 * 1024


def _sigmoid(x):
    return 0.5 * jnp.tanh(0.5 * x) + 0.5


def _silu(x):
    return x * _sigmoid(x)


def _softplus(x):
    return jnp.maximum(x, 0.0) + jnp.log1p(jnp.exp(-jnp.abs(x)))


def _dot(a, b):
    return jnp.dot(a, b, preferred_element_type=F32)


def _dot_nt(a, b):
    return lax.dot_general(a, b, (((1,), (1,)), ((), ())), preferred_element_type=F32)


def _dot_tn(a, b):
    return lax.dot_general(a, b, (((0,), (0,)), ((), ())), preferred_element_type=F32)


def _adaln_kernel(c_ref, w_ref, b_ref, o_ref):
    c = c_ref[...]
    o_ref[0] = _dot(_silu(c).astype(BF16), w_ref[0]) + b_ref[0]


def _adaln(cc, ada_w, ada_b):
    depth, d, n3 = ada_w.shape
    rows = cc.shape[0]
    tn = 1024
    return pl.pallas_call(
        _adaln_kernel,
        grid=(depth, n3 // tn),
        in_specs=[
            pl.BlockSpec((rows, d), lambda l, n: (0, 0)),
            pl.BlockSpec((1, d, tn), lambda l, n: (l, 0, n)),
            pl.BlockSpec((1, 1, tn), lambda l, n: (l, 0, n)),
        ],
        out_specs=pl.BlockSpec((1, rows, tn), lambda l, n: (l, 0, n)),
        out_shape=jax.ShapeDtypeStruct((depth, rows, n3), F32),
        name="adaln",
    )(cc, ada_w.astype(BF16), ada_b.reshape(depth, 1, n3))


CONV_PAD = 8
CONV_SLABS = TN // 128
CONV_PARTS = 4

def _in_kernel(xc_ref, xl_ref, ml_ref, mc_ref, pg_ref, w_ref, cw_ref, cb_ref, u_ref, h_s, conv_s, *, line):
    j = pl.program_id(1)
    lc = xc_ref.shape[1]
    tt = lc + xl_ref.shape[1]
    d = xc_ref.shape[2]
    nlines = (tt - lc) // line
    pitch = line + CONV_PAD
    lat0 = CONV_PAD + lc

    @pl.when(j == 0)
    def _():
        def norm_mod(x, mod):
            y = x * lax.rsqrt(jnp.mean(x * x, axis=-1, keepdims=True) + EPS) * pg_ref[...]
            return (y * (1.0 + mod[:, d:2 * d]) + mod[:, 0:d]).astype(BF16)
        h_s[0:lc, :] = norm_mod(xc_ref[0], mc_ref[...])
        h_s[lc:tt, :] = norm_mod(xl_ref[0], ml_ref[0])
        zeros = jnp.zeros((CONV_SLABS, CONV_PAD, 128), F32)
        conv_s[:, 0:CONV_PAD, :] = zeros
        for i in range(nlines + 1):
            r = lat0 + i * pitch
            conv_s[:, r:r + CONV_PAD, :] = zeros

    def conv(act):
        def taps(n, r, rows):
            ls = slice(n * 128, (n + 1) * 128)
            cw = cw_ref[:, ls]
            out = cb_ref[:, ls] + cw[1:2, :] * conv_s[n, r:r + rows, :]
            out = out + cw[0:1, :] * conv_s[n, r - 1:r - 1 + rows, :]
            out = out + cw[2:3, :] * conv_s[n, r + 1:r + 1 + rows, :]
            out = out + cw[3:4, :] * conv_s[n, r + 2:r + 2 + rows, :]
            return act(out)

        lines = [(0, CONV_PAD, lc)] + [(lc + i * line, lat0 + i * pitch + CONV_PAD, line) for i in range(nlines)]
        bounds = [0] + [lc + ((p * nlines) // CONV_PARTS) * line for p in range(1, CONV_PARTS)] + [tt]
        for p in range(CONV_PARTS):
            t0, t1 = bounds[p], bounds[p + 1]
            up = _dot(h_s[t0:t1, :], w_ref[...])
            part = [ln for ln in lines if t0 <= ln[0] < t1]
            for n in range(CONV_SLABS):
                for t, r, rows in part:
                    conv_s[n, r:r + rows, :] = up[t - t0:t - t0 + rows, n * 128:(n + 1) * 128]
            for t, r, rows in part:
                for n in range(CONV_SLABS):
                    u_ref[0, 0, t:t + rows, n * 128:(n + 1) * 128] = taps(n, r, rows)

    is_xbc = j < SSD_XBC // TN
    is_xa = (j >= C_XA // TN) & (j < (C_XA + D_MODEL) // TN)

    @pl.when(is_xbc)
    def _():
        conv(_silu)

    @pl.when(is_xa)
    def _():
        conv(lambda v: v)

    @pl.when(jnp.logical_not(is_xbc | is_xa))
    def _():
        u_ref[0, 0] = _dot(h_s[...], w_ref[...])


def _in_proj(x_ctx, x_lat, mod_lat, mod_ctx, pre_g, w_in_p, cw, cb, *, line):
    b, lc, d = x_ctx.shape
    seq = x_lat.shape[1]
    tt = lc + seq
    assert line % 8 == 0 and seq % line == 0 and lc % 8 == 0
    conv_rows = CONV_PAD + lc + (seq // line) * (line + CONV_PAD) + CONV_PAD
    return pl.pallas_call(
        functools.partial(_in_kernel, line=line),
        grid=(b, N_TOT // TN),
        in_specs=[
            pl.BlockSpec((1, lc, d), lambda i, j: (i, 0, 0)),
            pl.BlockSpec((1, seq, d), lambda i, j: (i, 0, 0)),
            pl.BlockSpec((1, 1, 3 * d), lambda i, j: (i, 0, 0)),
            pl.BlockSpec((1, 3 * d), lambda i, j: (0, 0)),
            pl.BlockSpec((1, d), lambda i, j: (0, 0)),
            pl.BlockSpec((d, TN), lambda i, j: (0, j)),
            pl.BlockSpec((CONV_W, TN), lambda i, j: (0, j)),
            pl.BlockSpec((1, TN), lambda i, j: (0, j)),
        ],
        out_specs=pl.BlockSpec((1, 1, tt, TN), lambda i, j: (i, j, 0, 0)),
        out_shape=jax.ShapeDtypeStruct((b, N_TOT // TN, tt, TN), F32),
        scratch_shapes=[pltpu.VMEM((tt, d), BF16), pltpu.VMEM((CONV_SLABS, conv_rows, 128), F32)],
        compiler_params=pltpu.CompilerParams(
            dimension_semantics=("arbitrary", "arbitrary"), vmem_limit_bytes=VMEM_LIMIT),
        name="in_proj",
    )(x_ctx, x_lat, mod_lat, mod_ctx, pre_g, w_in_p, cw, cb)


LRU_TC = 64
LRU_WT = 512
LRU_PITCH = LRU_TC + 8


def _lru_kernel(xf_ref, xb_ref, wr_ref, wi_ref, br_ref, bi_ref, lam_ref, yf_ref, yb_ref,
                a_s, b_s, o_s, h_s):
    i = pl.program_id(1)
    nb, _, tc, wt = xf_ref.shape

    @pl.when(i == 0)
    def _():
        h_s[...] = jnp.zeros_like(h_s)

    def gates(bi, carry):
        r0 = pl.multiple_of(bi * LRU_PITCH, 8)
        for dr, x_ref in enumerate((xf_ref, xb_ref)):
            for n in range(wt // LRU_BW):
                sl = slice(n * LRU_BW, (n + 1) * LRU_BW)
                xs = x_ref[bi, 0, :, sl]
                xs16 = xs.astype(BF16)
                r = _sigmoid(_dot(xs16, wr_ref[dr, n]) + br_ref[dr:dr + 1, sl])
                g = _sigmoid(_dot(xs16, wi_ref[dr, n]) + bi_ref[dr:dr + 1, sl])
                nla = (LRU_C * _softplus(-lam_ref[dr:dr + 1, sl])) * r
                a = jnp.exp(-nla)
                a_s[dr, n, pl.ds(r0, tc), :] = a
                b_s[dr, n, pl.ds(r0, tc), :] = jnp.sqrt(jnp.tanh(nla) * (a * a + 1.0)) * (g * xs)
        return carry
    lax.fori_loop(0, nb, gates, 0, unroll=2)

    nl = wt // LRU_BW
    hf = [h_s[0, n] for n in range(nl)]
    hb = [h_s[1, n] for n in range(nl)]
    for t in range(tc):
        rows_f = pl.ds(t, nb, stride=LRU_PITCH)
        rows_b = pl.ds(tc - 1 - t, nb, stride=LRU_PITCH)
        for n in range(nl):
            hf[n] = a_s[0, n, rows_f, :] * hf[n] + b_s[0, n, rows_f, :]
            hb[n] = a_s[1, n, rows_b, :] * hb[n] + b_s[1, n, rows_b, :]
            o_s[0, n, rows_f, :] = hf[n]
            o_s[1, n, rows_b, :] = hb[n]
    for n in range(nl):
        h_s[0, n] = hf[n]
        h_s[1, n] = hb[n]

    def emit(bi, carry):
        r0 = pl.multiple_of(bi * LRU_PITCH, 8)
        for n in range(nl):
            sl = slice(n * LRU_BW, (n + 1) * LRU_BW)
            yf_ref[bi, :, sl] = o_s[0, n, pl.ds(r0, tc), :]
            yb_ref[bi, :, sl] = o_s[1, n, pl.ds(r0, tc), :]
        return carry
    lax.fori_loop(0, nb, emit, 0)


def _bwd_block(i, nctx, ntot):
    return jnp.where(i < nctx, nctx - 1 - i, ntot - 1 + nctx - i)


def _lru(u, wr, wi, br, bi, lam, *, lc):
    b, _, tt, _ = u.shape
    w = D_MODEL
    nt = tt // LRU_TC
    nctx = lc // LRU_TC
    assert LRU_WT == TN
    c0 = C_XA // TN
    nblk = LRU_WT // LRU_BW
    y_spec_f = pl.BlockSpec((b, LRU_TC, LRU_WT), lambda j, i: (0, i, j))
    y_spec_b = pl.BlockSpec((b, LRU_TC, LRU_WT), lambda j, i: (0, _bwd_block(i, nctx, nt), j))
    return pl.pallas_call(
        _lru_kernel,
        grid=(w // LRU_WT, nt),
        in_specs=[
            pl.BlockSpec((b, 1, LRU_TC, LRU_WT), lambda j, i: (0, c0 + j, i, 0)),
            pl.BlockSpec((b, 1, LRU_TC, LRU_WT), lambda j, i: (0, c0 + j, _bwd_block(i, nctx, nt), 0)),
            pl.BlockSpec((2, nblk, LRU_BW, LRU_BW), lambda j, i: (0, j, 0, 0)),
            pl.BlockSpec((2, nblk, LRU_BW, LRU_BW), lambda j, i: (0, j, 0, 0)),
            pl.BlockSpec((2, LRU_WT), lambda j, i: (0, j)),
            pl.BlockSpec((2, LRU_WT), lambda j, i: (0, j)),
            pl.BlockSpec((2, LRU_WT), lambda j, i: (0, j)),
        ],
        out_specs=[y_spec_f, y_spec_b],
        out_shape=[jax.ShapeDtypeStruct((b, tt, w), F32)] * 2,
        scratch_shapes=[
            pltpu.VMEM((2, nblk, b * LRU_PITCH, LRU_BW), F32),
            pltpu.VMEM((2, nblk, b * LRU_PITCH, LRU_BW), F32),
            pltpu.VMEM((2, nblk, b * LRU_PITCH, LRU_BW), F32),
            pltpu.VMEM((2, nblk, b, LRU_BW), F32),
        ],
        compiler_params=pltpu.CompilerParams(
            dimension_semantics=("arbitrary", "arbitrary"), vmem_limit_bytes=VMEM_LIMIT),
        name="lru_scan",
    )(u, u, wr, wi, br, bi, lam)


SCAN_TB = 256
SM_W = 128
SM_DT = 0
SM_LOW = SSD_HEADS
BD_HEADS = 4
BD_W = BD_HEADS * SSD_P
LOG_FLOOR = -1e30


T_SSD_B = SSD_INNER // TN
T_SSD_C = T_SSD_B + 1


def _seg_spec(rows, c0, width, blk):
    nt = width // TN
    assert width % TN == 0 and (c0 // TN) % nt == 0
    return pl.BlockSpec((1, nt, rows, TN), lambda bi, i: (bi, c0 // width, blk(i), 0))


def _wide(ref, rs, ntiles=None):
    nt = ref.shape[1] if ntiles is None else ntiles
    return jnp.concatenate([ref[0, t, rs, :] for t in range(nt)], axis=1)


def _split3(x):
    hi = x.astype(BF16).astype(F32)
    r = x - hi
    mid = r.astype(BF16).astype(F32)
    return hi, mid, r - mid


def _cumsum_rows(tri16, x):
    n = x.shape[1]
    p = _dot(tri16, jnp.concatenate([s.astype(BF16) for s in _split3(x)], axis=1))
    return (p[:, 0:n] + p[:, n:2 * n]) + p[:, 2 * n:3 * n]


def _pack3(x, lane):
    hi, mid, lo = _split3(x)
    h = SSD_HEADS
    return jnp.where(lane < h, hi,
                     jnp.where(lane < 2 * h, pltpu.roll(mid, h, 1),
                               jnp.where(lane < 3 * h, pltpu.roll(lo, 2 * h, 1), 0.0)))


def _scan_kernel(qf_ref, kf_ref, vf_ref, xf_ref, sf_ref, qb_ref, kb_ref, vb_ref, xb_ref, sb_ref,
                 wup_ref, gb_ref, e3_ref, dtb_ref, na_ref, sd_ref,
                 ybf_ref, ycf_ref, ybb_ref, ycb_ref, gla_s, ssd_s):
    i = pl.program_id(1)

    @pl.when(i == 0)
    def _():
        gla_s[...] = jnp.zeros_like(gla_s)
        ssd_s[...] = jnp.zeros_like(ssd_s)

    tb = qf_ref.shape[2]
    nch = tb // CHUNK
    row = lax.broadcasted_iota(jnp.int32, (CHUNK, CHUNK), 0)
    col = lax.broadcasted_iota(jnp.int32, (CHUNK, CHUNK), 1)
    row_x = lax.broadcasted_iota(jnp.int32, (CHUNK, SSD_INNER), 0)
    col_x = lax.broadcasted_iota(jnp.int32, (CHUNK, SSD_INNER), 1) & (CHUNK - 1)
    r2 = lax.broadcasted_iota(jnp.int32, (BD_W, BD_W), 0)
    c2 = lax.broadcasted_iota(jnp.int32, (BD_W, BD_W), 1)
    blk_mask = (r2 // CHUNK) == (c2 // SSD_P)
    lane = lax.broadcasted_iota(jnp.int32, (4 * CHUNK, SM_W), 1)
    prow = lax.broadcasted_iota(jnp.int32, (3 * CHUNK, SM_W), 0)
    plane = lax.broadcasted_iota(jnp.int32, (3 * CHUNK, SM_W), 1)
    ones_part = jnp.where((prow < CHUNK) & (plane < 3 * SSD_HEADS), 1.0, 0.0).astype(BF16)
    heads = range(GLA_HEADS)
    groups = range(SSD_G)

    def ksl(h):
        return slice(h * GLA_DKH, (h + 1) * GLA_DKH)

    def vsl(h):
        return slice(h * GLA_DVH, (h + 1) * GLA_DVH)

    def gsl(g):
        return slice(g * SSD_GW, (g + 1) * SSD_GW)

    dirs = (
        (0, qf_ref, kf_ref, vf_ref, xf_ref, sf_ref, ybf_ref, ycf_ref),
        (1, qb_ref, kb_ref, vb_ref, xb_ref, sb_ref, ybb_ref, ycb_ref),
    )
    for step in range(nch):
        insts = []
        for dr, q_ref, k_ref, v_ref, x_ref, s_ref, yb_ref, yc_ref in dirs:
            fwd = dr == 0
            c = step if fwd else nch - 1 - step
            causal = (row >= col) if fwd else (row <= col)
            insts.append(dict(
                dr=dr, rs=slice(c * CHUNK, (c + 1) * CHUNK), last=CHUNK - 1 if fwd else 0,
                causal=causal, causal_x=(row_x >= col_x) if fwd else (row_x <= col_x),
                tri16=jnp.where(causal, 1.0, 0.0).astype(BF16),
                q_ref=q_ref, k_ref=k_ref, v_ref=v_ref, x_ref=x_ref, s_ref=s_ref, yb_ref=yb_ref, yc_ref=yc_ref))

        for it in insts:
            dr, rs, x_ref = it["dr"], it["rs"], it["x_ref"]
            it["sm"] = it["s_ref"][0, 0, rs, :]
            it["z"] = _dot(it["sm"].astype(BF16), wup_ref[dr])
            it["bm16"] = [x_ref[0, T_SSD_B, rs, g * SSD_N:(g + 1) * SSD_N].astype(BF16) for g in groups]
            it["cm16"] = [x_ref[0, T_SSD_C, rs, g * SSD_N:(g + 1) * SSD_N].astype(BF16) for g in groups]
            it["cb"] = [_dot_nt(it["cm16"][g], it["bm16"][g]) for g in groups]

        for it in insts:
            dr, rs, last = it["dr"], it["rs"], it["last"]
            logg = -_softplus(-(it["z"] + gb_ref[dr:dr + 1, :])) * (1.0 / GLA_TAU)
            bc = _cumsum_rows(it["tri16"], logg)
            btot = bc[last:last + 1, :]
            k = it["k_ref"][0, 0, rs, :]
            it["qe16"] = ((it["q_ref"][0, 0, rs, :] * (GLA_DKH ** -0.5)) * jnp.exp(bc)).astype(BF16)
            it["ke16"] = (k * jnp.exp(-bc)).astype(BF16)
            it["kd16"] = (k * jnp.exp(btot - bc)).astype(BF16)
            it["etot"] = jnp.exp(btot)
            it["v16"] = _wide(it["v_ref"], rs).astype(BF16)

            dt = _softplus(it["sm"] + dtb_ref[dr:dr + 1, :])
            cum = _cumsum_rows(it["tri16"], dt * na_ref[dr:dr + 1, :])
            clast = cum[last:last + 1, :]
            wrow = jnp.maximum(jnp.log(dt), LOG_FLOOR) - cum
            packed = _pack3(jnp.concatenate([cum, jnp.exp(cum), dt * jnp.exp(clast - cum), wrow], axis=0), lane)
            pt = packed[3 * CHUNK:4 * CHUNK].T
            pt2 = jnp.concatenate([pt, pt], axis=1).astype(BF16)
            rm = jnp.concatenate([pt2] * (SSD_INNER // 128), axis=1) * e3_ref[...]
            it["a16"] = jnp.concatenate([packed[0:3 * CHUNK].astype(BF16), ones_part], axis=1)
            it["rhs"] = jnp.concatenate([e3_ref[...], rm], axis=0)

        for it in insts:
            it["big"] = _dot(it["a16"], it["rhs"])
            it["att"] = [_dot_nt(it["qe16"][:, ksl(h)], it["ke16"][:, ksl(h)]) for h in heads]

        for it in insts:
            big = it["big"]
            lmat = jnp.exp(jnp.where(it["causal_x"], big[0:CHUNK], -jnp.inf))
            m16 = []
            for g in groups:
                cb2 = jnp.concatenate([it["cb"][g], it["cb"][g]], axis=1)
                m16.append(jnp.concatenate(
                    [cb2 * lmat[:, g * SSD_GW + q * 128:g * SSD_GW + (q + 1) * 128] for q in range(SSD_GW // 128)],
                    axis=1).astype(BF16))
            it["m16"] = m16
            it["ecum_x"] = big[CHUNK:2 * CHUNK]
            xs = _wide(it["x_ref"], it["rs"], SSD_INNER // TN)
            it["xs16"] = xs.astype(BF16)
            it["xdec16"] = (xs * big[2 * CHUNK:3 * CHUNK]).astype(BF16)
            it["att16"] = [jnp.where(it["causal"], a, 0.0).astype(BF16) for a in it["att"]]

        for it in insts:
            dr = it["dr"]
            it["o_in"] = [_dot(it["att16"][h], it["v16"][:, vsl(h)]) for h in heads]
            it["o_st"] = [_dot_nt(it["qe16"][:, ksl(h)], gla_s[dr, h].astype(BF16)) for h in heads]
            it["u_gla"] = [_dot_tn(it["v16"][:, vsl(h)], it["kd16"][:, ksl(h)]) for h in heads]
            y_in = []
            for g in groups:
                parts = []
                for pr in range(SSD_GW // BD_W):
                    ls = slice(g * SSD_GW + pr * BD_W, g * SSD_GW + (pr + 1) * BD_W)
                    xh = it["xs16"][:, ls]
                    bd = jnp.where(blk_mask, jnp.concatenate([xh] * BD_HEADS, axis=0), jnp.zeros((), BF16))
                    parts.append(_dot(it["m16"][g][:, pr * BD_W:(pr + 1) * BD_W], bd))
                y_in.append(jnp.concatenate(parts, axis=1))
            it["y_in"] = y_in
            it["y_st"] = [_dot(it["cm16"][g], ssd_s[dr, g].astype(BF16)) for g in groups]
            it["u_ssd"] = [_dot_tn(it["bm16"][g], it["xdec16"][:, gsl(g)]) for g in groups]

        for it in insts:
            dr, rs, last = it["dr"], it["rs"], it["last"]
            for h in heads:
                it["yb_ref"][0, rs, vsl(h)] = it["o_in"][h] + it["o_st"][h]
                gla_s[dr, h] = gla_s[dr, h] * it["etot"][:, ksl(h)] + it["u_gla"][h]
            for g in groups:
                ecum = it["ecum_x"][:, gsl(g)]
                yc = it["y_in"][g] + it["y_st"][g] * ecum
                if dr == 0:
                    yc = yc + sd_ref[:, gsl(g)] * it["x_ref"][0, g, rs, :]
                it["yc_ref"][0, rs, gsl(g)] = yc
                ssd_s[dr, g] = ssd_s[dr, g] * ecum[last:last + 1, :] + it["u_ssd"][g]


def _scan(u, wup, gb, e3, dtb, na, sd_x, *, lc):
    b, _, tt, _ = u.shape
    tb = SCAN_TB
    nt = tt // tb
    nctx = lc // tb

    def fblk(i):
        return i

    def bblk(i):
        return _bwd_block(i, nctx, nt)

    def fmap(cb):
        return lambda bi, i: (bi, i, cb)

    def bmap(cb):
        return lambda bi, i: (bi, _bwd_block(i, nctx, nt), cb)

    def tok_specs(blk, dr):
        return [
            _seg_spec(tb, C_Q, GLA_DK, blk), _seg_spec(tb, C_K, GLA_DK, blk), _seg_spec(tb, C_V, GLA_DV, blk),
            _seg_spec(tb, C_XBC, SSD_XBC, blk),
            pl.BlockSpec((1, 1, tb, SM_W), lambda bi, i: (bi, C_SM // TN, blk(i), dr)),
        ]

    def const_spec(a):
        nd = a.ndim
        return pl.BlockSpec(a.shape, lambda bi, i: (0,) * nd)

    assert SSD_GW == TN
    consts = (wup, gb, e3, dtb, na, sd_x)
    return pl.pallas_call(
        _scan_kernel,
        grid=(b, nt),
        in_specs=tok_specs(fblk, 0) + tok_specs(bblk, 1) + [const_spec(a) for a in consts],
        out_specs=[
            pl.BlockSpec((1, tb, GLA_DV), fmap(0)),
            pl.BlockSpec((1, tb, SSD_INNER), fmap(0)),
            pl.BlockSpec((1, tb, GLA_DV), bmap(0)),
            pl.BlockSpec((1, tb, SSD_INNER), bmap(0)),
        ],
        out_shape=[
            jax.ShapeDtypeStruct((b, tt, GLA_DV), F32),
            jax.ShapeDtypeStruct((b, tt, SSD_INNER), F32),
            jax.ShapeDtypeStruct((b, tt, GLA_DV), F32),
            jax.ShapeDtypeStruct((b, tt, SSD_INNER), F32),
        ],
        scratch_shapes=[
            pltpu.VMEM((2, GLA_HEADS, GLA_DVH, GLA_DKH), F32),
            pltpu.VMEM((2, SSD_G, SSD_N, SSD_GW), F32),
        ],
        compiler_params=pltpu.CompilerParams(
            dimension_semantics=("arbitrary", "arbitrary"), vmem_limit_bytes=VMEM_LIMIT),
        name="gla_ssd_scan",
    )(u, u, u, u, u, u, u, u, u, u, *consts)


FIN_TM = 256


def _group_rms(x, width):
    parts = []
    for s in range(x.shape[1] // width):
        xs = x[:, s * width:(s + 1) * width]
        parts.append(xs * lax.rsqrt(jnp.mean(xs * xs, axis=-1, keepdims=True) + EPS))
    return jnp.concatenate(parts, axis=1)


def _fin_kernel(yaf_ref, yab_ref, ybf_ref, ybb_ref, ycf_ref, ycb_ref, lg_ref, gg_ref, z_ref, mg_ref,
                xc_ref, xl_ref, ml_ref, mc_ref, gng_ref, sng_ref, pog_ref,
                wpa_ref, wpb_ref, wpc_ref, wo_ref, oc_ref, ol_ref, *, nctx):
    i = pl.program_id(1)
    d = D_MODEL
    rows = slice(None)
    ya = yaf_ref[0] + yab_ref[0]
    pa = _dot((ya * _silu(_wide(lg_ref, rows))).astype(BF16), wpa_ref[...])
    yb = ybf_ref[0] + ybb_ref[0]
    ob = (_group_rms(yb, GLA_DVH) * gng_ref[...]) * _silu(_wide(gg_ref, rows))
    pb = _dot(ob.astype(BF16), wpb_ref[...])
    yc = ycf_ref[0] + ycb_ref[0]
    oc = _group_rms(yc * _silu(_wide(z_ref, rows)), SSD_GW) * sng_ref[...]
    pc = _dot(oc.astype(BF16), wpc_ref[...])
    gates = _sigmoid(_wide(mg_ref, rows))
    merged = gates[:, 0:d] * pa + gates[:, d:2 * d] * pb + gates[:, 2 * d:3 * d] * pc
    out = _dot(merged.astype(BF16), wo_ref[...])
    normed = _group_rms(out, d) * pog_ref[...]

    @pl.when(i < nctx)
    def _():
        oc_ref[0] = xc_ref[0] + mc_ref[:, 2 * d:3 * d] * normed

    @pl.when(i >= nctx)
    def _():
        ol_ref[0] = xl_ref[0] + ml_ref[0][:, 2 * d:3 * d] * normed


def _finish(ya_f, ya_b, yb_f, yb_b, yc_f, yc_b, u, x_ctx, x_lat, mod_lat, mod_ctx,
            gng, sng, post_g, w_pa, w_pb, w_pc, w_out):
    b, lc, d = x_ctx.shape
    seq = x_lat.shape[1]
    tm = FIN_TM
    nctx = lc // tm
    ctx_spec = pl.BlockSpec((1, tm, d), lambda bi, i: (bi, jnp.minimum(i, nctx - 1), 0))
    lat_spec = pl.BlockSpec((1, tm, d), lambda bi, i: (bi, jnp.maximum(i - nctx, 0), 0))

    def tok(width):
        return pl.BlockSpec((1, tm, width), lambda bi, i: (bi, i, 0))

    def blk(i):
        return i

    def const_spec(a):
        nd = a.ndim
        return pl.BlockSpec(a.shape, lambda bi, i: (0,) * nd, pipeline_mode=pl.Buffered(1))

    consts = (gng, sng, post_g, w_pa, w_pb, w_pc, w_out)
    return pl.pallas_call(
        functools.partial(_fin_kernel, nctx=nctx),
        grid=(b, (lc + seq) // tm),
        in_specs=[
            tok(d), tok(d), tok(GLA_DV), tok(GLA_DV), tok(SSD_INNER), tok(SSD_INNER),
            _seg_spec(tm, C_LG, d, blk), _seg_spec(tm, C_GG, GLA_DV, blk), _seg_spec(tm, C_Z, SSD_INNER, blk),
            _seg_spec(tm, C_MRG, 3 * d, blk), ctx_spec, lat_spec,
            pl.BlockSpec((1, 1, 3 * d), lambda bi, i: (bi, 0, 0)),
            pl.BlockSpec((1, 3 * d), lambda bi, i: (0, 0)),
        ] + [const_spec(a) for a in consts],
        out_specs=[ctx_spec, lat_spec],
        out_shape=[jax.ShapeDtypeStruct((b, lc, d), F32), jax.ShapeDtypeStruct((b, seq, d), F32)],
        compiler_params=pltpu.CompilerParams(
            dimension_semantics=("arbitrary", "arbitrary"), vmem_limit_bytes=VMEM_LIMIT),
        name="finish",
    )(ya_f, ya_b, yb_f, yb_b, yc_f, yc_b, u, u, u, u, x_ctx, x_lat, mod_lat, mod_ctx, *consts)


def _reorder_in_weights(w_in):
    d = w_in.shape[0]
    o = 0
    seg = {}
    for name, wd in (("xa", D_MODEL), ("lg", D_MODEL), ("q", GLA_DK), ("k", GLA_DK), ("v", GLA_DV),
                     ("gg", GLA_DV), ("low", 2 * GLA_RANK), ("z", SSD_INNER), ("xbc", SSD_XBC),
                     ("dt", 2 * SSD_HEADS), ("mrg", 3 * D_MODEL)):
        seg[name] = w_in[:, o:o + wd]
        o += wd
    pad = jnp.zeros((d, SM_W - GLA_RANK - SSD_HEADS), w_in.dtype)
    cols = [seg["xbc"], seg["mrg"], seg["z"], seg["xa"], seg["v"], seg["lg"], seg["gg"],
            seg["q"], seg["k"]]
    for dr in range(2):
        cols += [seg["dt"][:, dr * SSD_HEADS:(dr + 1) * SSD_HEADS],
                 seg["low"][:, dr * GLA_RANK:(dr + 1) * GLA_RANK], pad]
    cols.append(jnp.zeros((d, N_TOT - C_SM - 2 * SM_W), w_in.dtype))
    return jnp.concatenate(cols, axis=1)


def _conv_tables(conv_a_w, conv_a_b, conv_c_w, conv_c_b):
    cw = jnp.zeros((CONV_W, N_TOT), F32)
    cw = cw.at[:, C_XBC:C_XBC + SSD_XBC].set(conv_c_w).at[:, C_XA:C_XA + D_MODEL].set(conv_a_w)
    cb = jnp.zeros((1, N_TOT), F32)
    cb = cb.at[0, C_XBC:C_XBC + SSD_XBC].set(conv_c_b).at[0, C_XA:C_XA + D_MODEL].set(conv_a_b)
    return cw, cb


def _scan_tables(gla_alpha_up, ssd_dt_bias, ssd_a_log):
    wup = jnp.zeros((2, SM_W, GLA_DK), F32).at[:, SM_LOW:SM_LOW + GLA_RANK, :].set(gla_alpha_up)
    dtb = jnp.zeros((2, SM_W), F32).at[:, SM_DT:SM_DT + SSD_HEADS].set(ssd_dt_bias)
    a_log = jnp.zeros((2, SM_W), F32).at[:, SM_DT:SM_DT + SSD_HEADS].set(ssd_a_log)
    valid = np.zeros((2, SM_W), np.float32)
    valid[:, SM_DT:SM_DT + SSD_HEADS] = 1.0
    e3 = np.zeros((SM_W, SSD_INNER), np.float32)
    for k in range(3 * SSD_HEADS):
        h = k % SSD_HEADS
        e3[k, h * SSD_P:(h + 1) * SSD_P] = 1.0
    return wup.astype(BF16), jnp.asarray(e3, BF16), dtb, a_log, jnp.asarray(valid)


def _to_col_major(h, rows):
    b, length, dm = h.shape
    return h.reshape(b, rows, GRID_W, dm).transpose(0, 2, 1, 3).reshape(b, length, dm)


def _from_col_major(h, rows):
    b, length, dm = h.shape
    return h.reshape(b, GRID_W, rows, dm).transpose(0, 2, 1, 3).reshape(b, length, dm)


def _neg_exp_kernel(a_ref, v_ref, o_ref):
    o_ref[...] = -jnp.exp(a_ref[...]) * v_ref[...]


def kernel(x, c, ctx, c_ctx, ada_w, ada_b, pre_g, post_g, w_in, conv_a_w, conv_a_b, lru_wr, lru_br,
           lru_wi, lru_bi, lru_lam, gla_alpha_up, gla_alpha_b, gla_norm_g, conv_c_w, conv_c_b,
           ssd_a_log, ssd_dt_bias, ssd_d, ssd_norm_g, w_pa, w_pb, w_pc, w_out):
    b, seq, d = x.shape
    lc = ctx.shape[1]
    depth = w_in.shape[0]
    rows = seq // GRID_W
    assert d == D_MODEL and lc % SCAN_TB == 0 and seq % SCAN_TB == 0 and b % 8 == 0

    pad_rows = (-(b + 1)) % 8
    cc = jnp.concatenate([c, c_ctx[None, :], jnp.zeros((pad_rows, d), F32)], axis=0)
    mods = _adaln(cc, ada_w, ada_b)

    w_in16 = w_in.astype(BF16)
    x_lat, x_ctx = x, ctx
    for l in range(depth):
        col_major = l % 2 == 1
        line = rows if col_major else GRID_W
        mod_lat = mods[l, :b].reshape(b, 1, 3 * d)
        mod_ctx = mods[l, b:b + 1]
        lat = _to_col_major(x_lat, rows) if col_major else x_lat

        cw, cb = _conv_tables(conv_a_w[l], conv_a_b[l], conv_c_w[l], conv_c_b[l])
        u = _in_proj(x_ctx, lat, mod_lat, mod_ctx, pre_g[l].reshape(1, d), _reorder_in_weights(w_in16[l]),
                     cw, cb, line=line)

        ya_f, ya_b = _lru(u, lru_wr[l].astype(BF16), lru_wi[l].astype(BF16), lru_br[l], lru_bi[l],
                          lru_lam[l], lc=lc)

        wup, e3, dtb, a_log, valid = _scan_tables(gla_alpha_up[l], ssd_dt_bias[l], ssd_a_log[l])
        na = pl.pallas_call(_neg_exp_kernel, out_shape=jax.ShapeDtypeStruct(a_log.shape, F32),
                            name="ssd_neg_a")(a_log, valid)
        sd_x = jnp.repeat(ssd_d[l], SSD_P).reshape(1, SSD_INNER)
        yb_f, yc_f, yb_b, yc_b = _scan(u, wup, gla_alpha_b[l], e3, dtb, na, sd_x, lc=lc)

        x_ctx, lat = _finish(ya_f, ya_b, yb_f, yb_b, yc_f, yc_b, u, x_ctx, lat, mod_lat, mod_ctx,
                             jnp.tile(gla_norm_g[l], GLA_HEADS).reshape(1, GLA_DV),
                             ssd_norm_g[l].reshape(1, SSD_INNER),
                             post_g[l].reshape(1, d),
                             w_pa[l].astype(BF16), w_pb[l].astype(BF16), w_pc[l].astype(BF16),
                             w_out[l].astype(BF16))
        x_lat = _from_col_major(lat, rows) if col_major else lat
    return x_lat
```

```python
import functools

import jax
import jax.numpy as jnp
import numpy as np
from jax import lax
from jax.experimental import pallas as pl
from jax.experimental.pallas import tpu as pltpu

F32 = jnp.float32
BF16 = jnp.bfloat16

D_MODEL = 1024
GRID_W = 64
CONV_W = 4
EPS = 1e-6
LRU_BW = 128
LRU_C = 8.0
LOG2E = 1.4426950408889634
GLA_HEADS = 4
GLA_DK = 512
GLA_DV = 1024
GLA_DKH = 128
GLA_DVH = 256
GLA_RANK = 16
GLA_TAU = 16.0
SSD_INNER = 2048
SSD_P = 64
SSD_HEADS = 32
SSD_N = 128
SSD_G = 4
SSD_GW = SSD_INNER // SSD_G
SSD_XBC = SSD_INNER + 2 * SSD_G * SSD_N
CHUNK = 64

TN = 512
C_XBC, C_MRG, C_Z, C_XA, C_V, C_LG, C_GG, C_Q, C_K, C_SM = (
    0, 3072, 6144, 8192, 9216, 10240, 11264, 12288, 12800, 13312)
N_TOT = C_SM + TN

VMEM_LIMIT = 56 * 1024 * 1024


def _sigmoid(x):
    return 0.5 * jnp.tanh(0.5 * x) + 0.5


def _silu(x):
    return x * _sigmoid(x)


def _softplus(x):
    return jnp.maximum(x, 0.0) + jnp.log1p(jnp.exp(-jnp.abs(x)))


def _dot(a, b):
    return jnp.dot(a, b, preferred_element_type=F32)


def _dot_nt(a, b):
    return lax.dot_general(a, b, (((1,), (1,)), ((), ())), preferred_element_type=F32)


def _dot_tn(a, b):
    return lax.dot_general(a, b, (((0,), (0,)), ((), ())), preferred_element_type=F32)


def _adaln_kernel(c_ref, w_ref, b_ref, o_ref):
    c = c_ref[...]
    o_ref[0] = _dot(_silu(c).astype(BF16), w_ref[0]) + b_ref[0]


def _adaln(cc, ada_w, ada_b):
    depth, d, n3 = ada_w.shape
    rows = cc.shape[0]
    tn = 1024
    return pl.pallas_call(
        _adaln_kernel,
        grid=(depth, n3 // tn),
        in_specs=[
            pl.BlockSpec((rows, d), lambda l, n: (0, 0)),
            pl.BlockSpec((1, d, tn), lambda l, n: (l, 0, n)),
            pl.BlockSpec((1, 1, tn), lambda l, n: (l, 0, n)),
        ],
        out_specs=pl.BlockSpec((1, rows, tn), lambda l, n: (l, 0, n)),
        out_shape=jax.ShapeDtypeStruct((depth, rows, n3), F32),
        name="adaln",
    )(cc, ada_w.astype(BF16), ada_b.reshape(depth, 1, n3))


CONV_PAD = 8
CONV_SLABS = TN // 128
CONV_PARTS = 8

def _in_kernel(xc_ref, xl_ref, ml_ref, mc_ref, pg_ref, w_ref, cw_ref, cb_ref, u_ref, h_s, conv_s, *, line):
    j = pl.program_id(1)
    lc = xc_ref.shape[1]
    tt = lc + xl_ref.shape[1]
    d = xc_ref.shape[2]
    nlines = (tt - lc) // line
    pitch = line + CONV_PAD
    lat0 = CONV_PAD + lc

    @pl.when(j == 0)
    def _():
        def norm_mod(x, mod):
            y = x * lax.rsqrt(jnp.mean(x * x, axis=-1, keepdims=True) + EPS) * pg_ref[...]
            return (y * (1.0 + mod[:, d:2 * d]) + mod[:, 0:d]).astype(BF16)
        h_s[0:lc, :] = norm_mod(xc_ref[0], mc_ref[...])
        h_s[lc:tt, :] = norm_mod(xl_ref[0], ml_ref[0])
        zeros = jnp.zeros((CONV_SLABS, CONV_PAD, 128), F32)
        conv_s[:, 0:CONV_PAD, :] = zeros
        for i in range(nlines + 1):
            r = lat0 + i * pitch
            conv_s[:, r:r + CONV_PAD, :] = zeros

    def conv(act):
        def taps(n, r, rows):
            ls = slice(n * 128, (n + 1) * 128)
            cw = cw_ref[:, ls]
            out = cb_ref[:, ls] + cw[1:2, :] * conv_s[n, r:r + rows, :]
            out = out + cw[0:1, :] * conv_s[n, r - 1:r - 1 + rows, :]
            out = out + cw[2:3, :] * conv_s[n, r + 1:r + 1 + rows, :]
            out = out + cw[3:4, :] * conv_s[n, r + 2:r + 2 + rows, :]
            return act(out)

        lines = [(0, CONV_PAD, lc)] + [(lc + i * line, lat0 + i * pitch + CONV_PAD, line) for i in range(nlines)]
        bounds = [0] + [lc + ((p * nlines) // CONV_PARTS) * line for p in range(1, CONV_PARTS)] + [tt]
        for p in range(CONV_PARTS):
            t0, t1 = bounds[p], bounds[p + 1]
            up = _dot(h_s[t0:t1, :], w_ref[...])
            part = [ln for ln in lines if t0 <= ln[0] < t1]
            for n in range(CONV_SLABS):
                for t, r, rows in part:
                    conv_s[n, r:r + rows, :] = up[t - t0:t - t0 + rows, n * 128:(n + 1) * 128]
            for t, r, rows in part:
                for n in range(CONV_SLABS):
                    u_ref[0, 0, t:t + rows, n * 128:(n + 1) * 128] = taps(n, r, rows)

    is_xbc = j < SSD_XBC // TN
    is_xa = (j >= C_XA // TN) & (j < (C_XA + D_MODEL) // TN)

    @pl.when(is_xbc)
    def _():
        conv(_silu)

    @pl.when(is_xa)
    def _():
        conv(lambda v: v)

    @pl.when(jnp.logical_not(is_xbc | is_xa))
    def _():
        u_ref[0, 0] = _dot(h_s[...], w_ref[...])


def _in_proj(x_ctx, x_lat, mod_lat, mod_ctx, pre_g, w_in_p, cw, cb, *, line):
    b, lc, d = x_ctx.shape
    seq = x_lat.shape[1]
    tt = lc + seq
    assert line % 8 == 0 and seq % line == 0 and lc % 8 == 0
    conv_rows = CONV_PAD + lc + (seq // line) * (line + CONV_PAD) + CONV_PAD
    return pl.pallas_call(
        functools.partial(_in_kernel, line=line),
        grid=(b, N_TOT // TN),
        in_specs=[
            pl.BlockSpec((1, lc, d), lambda i, j: (i, 0, 0)),
            pl.BlockSpec((1, seq, d), lambda i, j: (i, 0, 0)),
            pl.BlockSpec((1, 1, 3 * d), lambda i, j: (i, 0, 0)),
            pl.BlockSpec((1, 3 * d), lambda i, j: (0, 0)),
            pl.BlockSpec((1, d), lambda i, j: (0, 0)),
            pl.BlockSpec((d, TN), lambda i, j: (0, j)),
            pl.BlockSpec((CONV_W, TN), lambda i, j: (0, j)),
            pl.BlockSpec((1, TN), lambda i, j: (0, j)),
        ],
        out_specs=pl.BlockSpec((1, 1, tt, TN), lambda i, j: (i, j, 0, 0)),
        out_shape=jax.ShapeDtypeStruct((b, N_TOT // TN, tt, TN), F32),
        scratch_shapes=[pltpu.VMEM((tt, d), BF16), pltpu.VMEM((CONV_SLABS, conv_rows, 128), F32)],
        compiler_params=pltpu.CompilerParams(
            dimension_semantics=("arbitrary", "arbitrary"), vmem_limit_bytes=VMEM_LIMIT),
        name="in_proj",
    )(x_ctx, x_lat, mod_lat, mod_ctx, pre_g, w_in_p, cw, cb)


LRU_TC = 64
LRU_WT = 512
LRU_PITCH = LRU_TC + 8


def _lru_kernel(xf_ref, xb_ref, wr_ref, wi_ref, br_ref, bi_ref, lam_ref, yf_ref, yb_ref,
                a_s, b_s, o_s, h_s):
    i = pl.program_id(1)
    nb, _, tc, wt = xf_ref.shape

    @pl.when(i == 0)
    def _():
        h_s[...] = jnp.zeros_like(h_s)

    def gates(bi, carry):
        r0 = pl.multiple_of(bi * LRU_PITCH, 8)
        for dr, x_ref in enumerate((xf_ref, xb_ref)):
            for n in range(wt // LRU_BW):
                sl = slice(n * LRU_BW, (n + 1) * LRU_BW)
                xs = x_ref[bi, 0, :, sl]
                xs16 = xs.astype(BF16)
                hc = (0.5 * LRU_C) * _softplus(-lam_ref[dr:dr + 1, sl])
                nla = hc * jnp.tanh(0.5 * (_dot(xs16, wr_ref[dr, n]) + br_ref[dr:dr + 1, sl])) + hc
                g = _sigmoid(_dot(xs16, wi_ref[dr, n]) + bi_ref[dr:dr + 1, sl])
                a = jnp.exp2(nla * (-LOG2E))
                m2 = jnp.tanh(nla) * (a * a + 1.0)
                root = jnp.where(m2 > 0.0, m2 * lax.rsqrt(m2), 0.0)
                a_s[dr, n, pl.ds(r0, tc), :] = a
                b_s[dr, n, pl.ds(r0, tc), :] = root * (g * xs)
        return carry
    lax.fori_loop(0, nb, gates, 0, unroll=2)

    nl = wt // LRU_BW
    hf = [h_s[0, n] for n in range(nl)]
    hb = [h_s[1, n] for n in range(nl)]
    for t in range(tc):
        rows_f = pl.ds(t, nb, stride=LRU_PITCH)
        rows_b = pl.ds(tc - 1 - t, nb, stride=LRU_PITCH)
        for n in range(nl):
            hf[n] = a_s[0, n, rows_f, :] * hf[n] + b_s[0, n, rows_f, :]
            hb[n] = a_s[1, n, rows_b, :] * hb[n] + b_s[1, n, rows_b, :]
            o_s[0, n, rows_f, :] = hf[n]
            o_s[1, n, rows_b, :] = hb[n]
    for n in range(nl):
        h_s[0, n] = hf[n]
        h_s[1, n] = hb[n]

    def emit(bi, carry):
        r0 = pl.multiple_of(bi * LRU_PITCH, 8)
        for n in range(nl):
            sl = slice(n * LRU_BW, (n + 1) * LRU_BW)
            yf_ref[bi, :, sl] = o_s[0, n, pl.ds(r0, tc), :]
            yb_ref[bi, :, sl] = o_s[1, n, pl.ds(r0, tc), :]
        return carry
    lax.fori_loop(0, nb, emit, 0)


def _bwd_block(i, nctx, ntot):
    return jnp.where(i < nctx, nctx - 1 - i, ntot - 1 + nctx - i)


def _lru(u, wr, wi, br, bi, lam, *, lc):
    b, _, tt, _ = u.shape
    w = D_MODEL
    nt = tt // LRU_TC
    nctx = lc // LRU_TC
    assert LRU_WT == TN
    c0 = C_XA // TN
    nblk = LRU_WT // LRU_BW
    y_spec_f = pl.BlockSpec((b, LRU_TC, LRU_WT), lambda j, i: (0, i, j))
    y_spec_b = pl.BlockSpec((b, LRU_TC, LRU_WT), lambda j, i: (0, _bwd_block(i, nctx, nt), j))
    return pl.pallas_call(
        _lru_kernel,
        grid=(w // LRU_WT, nt),
        in_specs=[
            pl.BlockSpec((b, 1, LRU_TC, LRU_WT), lambda j, i: (0, c0 + j, i, 0)),
            pl.BlockSpec((b, 1, LRU_TC, LRU_WT), lambda j, i: (0, c0 + j, _bwd_block(i, nctx, nt), 0)),
            pl.BlockSpec((2, nblk, LRU_BW, LRU_BW), lambda j, i: (0, j, 0, 0)),
            pl.BlockSpec((2, nblk, LRU_BW, LRU_BW), lambda j, i: (0, j, 0, 0)),
            pl.BlockSpec((2, LRU_WT), lambda j, i: (0, j)),
            pl.BlockSpec((2, LRU_WT), lambda j, i: (0, j)),
            pl.BlockSpec((2, LRU_WT), lambda j, i: (0, j)),
        ],
        out_specs=[y_spec_f, y_spec_b],
        out_shape=[jax.ShapeDtypeStruct((b, tt, w), F32)] * 2,
        scratch_shapes=[
            pltpu.VMEM((2, nblk, b * LRU_PITCH, LRU_BW), F32),
            pltpu.VMEM((2, nblk, b * LRU_PITCH, LRU_BW), F32),
            pltpu.VMEM((2, nblk, b * LRU_PITCH, LRU_BW), F32),
            pltpu.VMEM((2, nblk, b, LRU_BW), F32),
        ],
        compiler_params=pltpu.CompilerParams(
            dimension_semantics=("arbitrary", "arbitrary"), vmem_limit_bytes=VMEM_LIMIT),
        name="lru_scan",
    )(u, u, wr, wi, br, bi, lam)


SCAN_TB = 256
SM_W = 128
SM_DT = 0
SM_LOW = SSD_HEADS
BD_HEADS = 4
BD_W = BD_HEADS * SSD_P
LOG_FLOOR = -1e30


T_SSD_B = SSD_INNER // TN
T_SSD_C = T_SSD_B + 1


def _seg_spec(rows, c0, width, blk):
    nt = width // TN
    assert width % TN == 0 and (c0 // TN) % nt == 0
    return pl.BlockSpec((1, nt, rows, TN), lambda bi, i: (bi, c0 // width, blk(i), 0))


def _wide(ref, rs, ntiles=None):
    nt = ref.shape[1] if ntiles is None else ntiles
    return jnp.concatenate([ref[0, t, rs, :] for t in range(nt)], axis=1)


def _split3(x):
    hi = x.astype(BF16).astype(F32)
    r = x - hi
    mid = r.astype(BF16).astype(F32)
    return hi, mid, r - mid


def _cumsum_rows(tri16, x):
    n = x.shape[1]
    p = _dot(tri16, jnp.concatenate([s.astype(BF16) for s in _split3(x)], axis=1))
    return (p[:, 0:n] + p[:, n:2 * n]) + p[:, 2 * n:3 * n]


def _pack3(x, lane):
    hi, mid, lo = _split3(x)
    h = SSD_HEADS
    return jnp.where(lane < h, hi,
                     jnp.where(lane < 2 * h, pltpu.roll(mid, h, 1),
                               jnp.where(lane < 3 * h, pltpu.roll(lo, 2 * h, 1), 0.0)))


def _scan_kernel(qf_ref, kf_ref, vf_ref, xf_ref, sf_ref, qb_ref, kb_ref, vb_ref, xb_ref, sb_ref,
                 wup_ref, gb_ref, e3_ref, dtb_ref, na_ref, sd_ref,
                 ybf_ref, ycf_ref, ybb_ref, ycb_ref, gla_s, ssd_s):
    i = pl.program_id(1)

    @pl.when(i == 0)
    def _():
        gla_s[...] = jnp.zeros_like(gla_s)
        ssd_s[...] = jnp.zeros_like(ssd_s)

    tb = qf_ref.shape[2]
    nch = tb // CHUNK
    row = lax.broadcasted_iota(jnp.int32, (CHUNK, CHUNK), 0)
    col = lax.broadcasted_iota(jnp.int32, (CHUNK, CHUNK), 1)
    row_x = lax.broadcasted_iota(jnp.int32, (CHUNK, SSD_INNER), 0)
    col_x = lax.broadcasted_iota(jnp.int32, (CHUNK, SSD_INNER), 1) & (CHUNK - 1)
    r2 = lax.broadcasted_iota(jnp.int32, (BD_W, BD_W), 0)
    c2 = lax.broadcasted_iota(jnp.int32, (BD_W, BD_W), 1)
    blk_mask = (r2 // CHUNK) == (c2 // SSD_P)
    lane = lax.broadcasted_iota(jnp.int32, (4 * CHUNK, SM_W), 1)
    prow = lax.broadcasted_iota(jnp.int32, (3 * CHUNK, SM_W), 0)
    plane = lax.broadcasted_iota(jnp.int32, (3 * CHUNK, SM_W), 1)
    ones_part = jnp.where((prow < CHUNK) & (plane < 3 * SSD_HEADS), 1.0, 0.0).astype(BF16)
    heads = range(GLA_HEADS)
    groups = range(SSD_G)

    def ksl(h):
        return slice(h * GLA_DKH, (h + 1) * GLA_DKH)

    def vsl(h):
        return slice(h * GLA_DVH, (h + 1) * GLA_DVH)

    def gsl(g):
        return slice(g * SSD_GW, (g + 1) * SSD_GW)

    dirs = (
        (0, qf_ref, kf_ref, vf_ref, xf_ref, sf_ref, ybf_ref, ycf_ref),
        (1, qb_ref, kb_ref, vb_ref, xb_ref, sb_ref, ybb_ref, ycb_ref),
    )
    def prepare(step):
        insts = []
        for dr, q_ref, k_ref, v_ref, x_ref, s_ref, yb_ref, yc_ref in dirs:
            fwd = dr == 0
            c = step if fwd else nch - 1 - step
            causal = (row >= col) if fwd else (row <= col)
            insts.append(dict(
                dr=dr, rs=slice(c * CHUNK, (c + 1) * CHUNK), last=CHUNK - 1 if fwd else 0,
                causal=causal, causal_x=(row_x >= col_x) if fwd else (row_x <= col_x),
                tri16=jnp.where(causal, 1.0, 0.0).astype(BF16),
                q_ref=q_ref, k_ref=k_ref, v_ref=v_ref, x_ref=x_ref, s_ref=s_ref, yb_ref=yb_ref, yc_ref=yc_ref))

        for it in insts:
            dr, rs, x_ref = it["dr"], it["rs"], it["x_ref"]
            it["sm"] = it["s_ref"][0, 0, rs, :]
            it["z"] = _dot(it["sm"].astype(BF16), wup_ref[dr])
            it["bm16"] = [x_ref[0, T_SSD_B, rs, g * SSD_N:(g + 1) * SSD_N].astype(BF16) for g in groups]
            it["cm16"] = [x_ref[0, T_SSD_C, rs, g * SSD_N:(g + 1) * SSD_N].astype(BF16) for g in groups]
            it["cb"] = [_dot_nt(it["cm16"][g], it["bm16"][g]) for g in groups]

        for it in insts:
            dr, rs, last = it["dr"], it["rs"], it["last"]
            logg = -_softplus(-(it["z"] + gb_ref[dr:dr + 1, :])) * (1.0 / GLA_TAU)
            bc = _cumsum_rows(it["tri16"], logg)
            btot = bc[last:last + 1, :]
            k = it["k_ref"][0, 0, rs, :]
            it["qe16"] = ((it["q_ref"][0, 0, rs, :] * (GLA_DKH ** -0.5)) * jnp.exp(bc)).astype(BF16)
            it["ke16"] = (k * jnp.exp(-bc)).astype(BF16)
            it["kd16"] = (k * jnp.exp(btot - bc)).astype(BF16)
            it["etot"] = jnp.exp(btot)
            it["v16"] = _wide(it["v_ref"], rs).astype(BF16)

            dt = _softplus(it["sm"] + dtb_ref[dr:dr + 1, :])
            cum = _cumsum_rows(it["tri16"], dt * na_ref[dr:dr + 1, :])
            clast = cum[last:last + 1, :]
            wrow = jnp.maximum(jnp.log(dt), LOG_FLOOR) - cum
            packed = _pack3(jnp.concatenate([cum, jnp.exp(cum), dt * jnp.exp(clast - cum), wrow], axis=0), lane)
            pt = packed[3 * CHUNK:4 * CHUNK].T
            pt2 = jnp.concatenate([pt, pt], axis=1).astype(BF16)
            rm = jnp.concatenate([pt2] * (SSD_INNER // 128), axis=1) * e3_ref[...]
            it["a16"] = jnp.concatenate([packed[0:3 * CHUNK].astype(BF16), ones_part], axis=1)
            it["rhs"] = jnp.concatenate([e3_ref[...], rm], axis=0)
        return insts

    def advance(insts):
        for it in insts:
            it["big"] = _dot(it["a16"], it["rhs"])
            it["att"] = [_dot_nt(it["qe16"][:, ksl(h)], it["ke16"][:, ksl(h)]) for h in heads]

        for it in insts:
            big = it["big"]
            lmat = jnp.exp(jnp.where(it["causal_x"], big[0:CHUNK], -jnp.inf))
            m16 = []
            for g in groups:
                cb2 = jnp.concatenate([it["cb"][g], it["cb"][g]], axis=1)
                m16.append(jnp.concatenate(
                    [cb2 * lmat[:, g * SSD_GW + q * 128:g * SSD_GW + (q + 1) * 128] for q in range(SSD_GW // 128)],
                    axis=1).astype(BF16))
            it["m16"] = m16
            it["ecum_x"] = big[CHUNK:2 * CHUNK]
            xs = _wide(it["x_ref"], it["rs"], SSD_INNER // TN)
            it["xs16"] = xs.astype(BF16)
            it["xdec16"] = (xs * big[2 * CHUNK:3 * CHUNK]).astype(BF16)
            it["att16"] = [jnp.where(it["causal"], a, 0.0).astype(BF16) for a in it["att"]]

        for it in insts:
            dr = it["dr"]
            it["o_in"] = [_dot(it["att16"][h], it["v16"][:, vsl(h)]) for h in heads]
            it["o_st"] = [_dot_nt(it["qe16"][:, ksl(h)], gla_s[dr, h].astype(BF16)) for h in heads]
            it["u_gla"] = [_dot_tn(it["v16"][:, vsl(h)], it["kd16"][:, ksl(h)]) for h in heads]
            y_in = []
            for g in groups:
                parts = []
                for pr in range(SSD_GW // BD_W):
                    ls = slice(g * SSD_GW + pr * BD_W, g * SSD_GW + (pr + 1) * BD_W)
                    xh = it["xs16"][:, ls]
                    bd = jnp.where(blk_mask, jnp.concatenate([xh] * BD_HEADS, axis=0), jnp.zeros((), BF16))
                    parts.append(_dot(it["m16"][g][:, pr * BD_W:(pr + 1) * BD_W], bd))
                y_in.append(jnp.concatenate(parts, axis=1))
            it["y_in"] = y_in
            it["y_st"] = [_dot(it["cm16"][g], ssd_s[dr, g].astype(BF16)) for g in groups]
            it["u_ssd"] = [_dot_tn(it["bm16"][g], it["xdec16"][:, gsl(g)]) for g in groups]

        for it in insts:
            dr, rs, last = it["dr"], it["rs"], it["last"]
            for h in heads:
                it["yb_ref"][0, rs, vsl(h)] = it["o_in"][h] + it["o_st"][h]
                gla_s[dr, h] = gla_s[dr, h] * it["etot"][:, ksl(h)] + it["u_gla"][h]
            for g in groups:
                ecum = it["ecum_x"][:, gsl(g)]
                yc = it["y_in"][g] + it["y_st"][g] * ecum
                if dr == 0:
                    yc = yc + sd_ref[:, gsl(g)] * it["x_ref"][0, g, rs, :]
                it["yc_ref"][0, rs, gsl(g)] = yc
                ssd_s[dr, g] = ssd_s[dr, g] * ecum[last:last + 1, :] + it["u_ssd"][g]

    ready = prepare(0)
    for step in range(nch):
        insts = ready
        if step + 1 < nch:
            ready = prepare(step + 1)
        advance(insts)


def _scan(u, wup, gb, e3, dtb, na, sd_x, *, lc):
    b, _, tt, _ = u.shape
    tb = SCAN_TB
    nt = tt // tb
    nctx = lc // tb

    def fblk(i):
        return i

    def bblk(i):
        return _bwd_block(i, nctx, nt)

    def fmap(cb):
        return lambda bi, i: (bi, i, cb)

    def bmap(cb):
        return lambda bi, i: (bi, _bwd_block(i, nctx, nt), cb)

    def tok_specs(blk, dr):
        return [
            _seg_spec(tb, C_Q, GLA_DK, blk), _seg_spec(tb, C_K, GLA_DK, blk), _seg_spec(tb, C_V, GLA_DV, blk),
            _seg_spec(tb, C_XBC, SSD_XBC, blk),
            pl.BlockSpec((1, 1, tb, SM_W), lambda bi, i: (bi, C_SM // TN, blk(i), dr)),
        ]

    def const_spec(a):
        nd = a.ndim
        return pl.BlockSpec(a.shape, lambda bi, i: (0,) * nd)

    assert SSD_GW == TN
    consts = (wup, gb, e3, dtb, na, sd_x)
    return pl.pallas_call(
        _scan_kernel,
        grid=(b, nt),
        in_specs=tok_specs(fblk, 0) + tok_specs(bblk, 1) + [const_spec(a) for a in consts],
        out_specs=[
            pl.BlockSpec((1, tb, GLA_DV), fmap(0)),
            pl.BlockSpec((1, tb, SSD_INNER), fmap(0)),
            pl.BlockSpec((1, tb, GLA_DV), bmap(0)),
            pl.BlockSpec((1, tb, SSD_INNER), bmap(0)),
        ],
        out_shape=[
            jax.ShapeDtypeStruct((b, tt, GLA_DV), F32),
            jax.ShapeDtypeStruct((b, tt, SSD_INNER), F32),
            jax.ShapeDtypeStruct((b, tt, GLA_DV), F32),
            jax.ShapeDtypeStruct((b, tt, SSD_INNER), F32),
        ],
        scratch_shapes=[
            pltpu.VMEM((2, GLA_HEADS, GLA_DVH, GLA_DKH), F32),
            pltpu.VMEM((2, SSD_G, SSD_N, SSD_GW), F32),
        ],
        compiler_params=pltpu.CompilerParams(
            dimension_semantics=("arbitrary", "arbitrary"), vmem_limit_bytes=VMEM_LIMIT),
        name="gla_ssd_scan",
    )(u, u, u, u, u, u, u, u, u, u, *consts)


FIN_TM = 256


def _group_rms(x, width):
    parts = []
    for s in range(x.shape[1] // width):
        xs = x[:, s * width:(s + 1) * width]
        parts.append(xs * lax.rsqrt(jnp.mean(xs * xs, axis=-1, keepdims=True) + EPS))
    return jnp.concatenate(parts, axis=1)


def _fin_kernel(yaf_ref, yab_ref, ybf_ref, ybb_ref, ycf_ref, ycb_ref, lg_ref, gg_ref, z_ref, mg_ref,
                xc_ref, xl_ref, ml_ref, mc_ref, gng_ref, sng_ref, pog_ref,
                wpa_ref, wpb_ref, wpc_ref, wo_ref, oc_ref, ol_ref, *, nctx):
    i = pl.program_id(1)
    d = D_MODEL
    rows = slice(None)
    ya = yaf_ref[0] + yab_ref[0]
    pa = _dot((ya * _silu(_wide(lg_ref, rows))).astype(BF16), wpa_ref[...])
    yb = ybf_ref[0] + ybb_ref[0]
    ob = (_group_rms(yb, GLA_DVH) * gng_ref[...]) * _silu(_wide(gg_ref, rows))
    pb = _dot(ob.astype(BF16), wpb_ref[...])
    yc = ycf_ref[0] + ycb_ref[0]
    oc = _group_rms(yc * _silu(_wide(z_ref, rows)), SSD_GW) * sng_ref[...]
    pc = _dot(oc.astype(BF16), wpc_ref[...])
    gates = _sigmoid(_wide(mg_ref, rows))
    merged = gates[:, 0:d] * pa + gates[:, d:2 * d] * pb + gates[:, 2 * d:3 * d] * pc
    out = _dot(merged.astype(BF16), wo_ref[...])
    normed = _group_rms(out, d) * pog_ref[...]

    @pl.when(i < nctx)
    def _():
        oc_ref[0] = xc_ref[0] + mc_ref[:, 2 * d:3 * d] * normed

    @pl.when(i >= nctx)
    def _():
        ol_ref[0] = xl_ref[0] + ml_ref[0][:, 2 * d:3 * d] * normed


def _finish(ya_f, ya_b, yb_f, yb_b, yc_f, yc_b, u, x_ctx, x_lat, mod_lat, mod_ctx,
            gng, sng, post_g, w_pa, w_pb, w_pc, w_out):
    b, lc, d = x_ctx.shape
    seq = x_lat.shape[1]
    tm = FIN_TM
    nctx = lc // tm
    ctx_spec = pl.BlockSpec((1, tm, d), lambda bi, i: (bi, jnp.minimum(i, nctx - 1), 0))
    lat_spec = pl.BlockSpec((1, tm, d), lambda bi, i: (bi, jnp.maximum(i - nctx, 0), 0))

    def tok(width):
        return pl.BlockSpec((1, tm, width), lambda bi, i: (bi, i, 0))

    def blk(i):
        return i

    def const_spec(a):
        nd = a.ndim
        return pl.BlockSpec(a.shape, lambda bi, i: (0,) * nd, pipeline_mode=pl.Buffered(1))

    consts = (gng, sng, post_g, w_pa, w_pb, w_pc, w_out)
    return pl.pallas_call(
        functools.partial(_fin_kernel, nctx=nctx),
        grid=(b, (lc + seq) // tm),
        in_specs=[
            tok(d), tok(d), tok(GLA_DV), tok(GLA_DV), tok(SSD_INNER), tok(SSD_INNER),
            _seg_spec(tm, C_LG, d, blk), _seg_spec(tm, C_GG, GLA_DV, blk), _seg_spec(tm, C_Z, SSD_INNER, blk),
            _seg_spec(tm, C_MRG, 3 * d, blk), ctx_spec, lat_spec,
            pl.BlockSpec((1, 1, 3 * d), lambda bi, i: (bi, 0, 0)),
            pl.BlockSpec((1, 3 * d), lambda bi, i: (0, 0)),
        ] + [const_spec(a) for a in consts],
        out_specs=[ctx_spec, lat_spec],
        out_shape=[jax.ShapeDtypeStruct((b, lc, d), F32), jax.ShapeDtypeStruct((b, seq, d), F32)],
        compiler_params=pltpu.CompilerParams(
            dimension_semantics=("arbitrary", "arbitrary"), vmem_limit_bytes=VMEM_LIMIT),
        name="finish",
    )(ya_f, ya_b, yb_f, yb_b, yc_f, yc_b, u, u, u, u, x_ctx, x_lat, mod_lat, mod_ctx, *consts)


def _reorder_in_weights(w_in):
    d = w_in.shape[0]
    o = 0
    seg = {}
    for name, wd in (("xa", D_MODEL), ("lg", D_MODEL), ("q", GLA_DK), ("k", GLA_DK), ("v", GLA_DV),
                     ("gg", GLA_DV), ("low", 2 * GLA_RANK), ("z", SSD_INNER), ("xbc", SSD_XBC),
                     ("dt", 2 * SSD_HEADS), ("mrg", 3 * D_MODEL)):
        seg[name] = w_in[:, o:o + wd]
        o += wd
    pad = jnp.zeros((d, SM_W - GLA_RANK - SSD_HEADS), w_in.dtype)
    cols = [seg["xbc"], seg["mrg"], seg["z"], seg["xa"], seg["v"], seg["lg"], seg["gg"],
            seg["q"], seg["k"]]
    for dr in range(2):
        cols += [seg["dt"][:, dr * SSD_HEADS:(dr + 1) * SSD_HEADS],
                 seg["low"][:, dr * GLA_RANK:(dr + 1) * GLA_RANK], pad]
    cols.append(jnp.zeros((d, N_TOT - C_SM - 2 * SM_W), w_in.dtype))
    return jnp.concatenate(cols, axis=1)


def _conv_tables(conv_a_w, conv_a_b, conv_c_w, conv_c_b):
    cw = jnp.zeros((CONV_W, N_TOT), F32)
    cw = cw.at[:, C_XBC:C_XBC + SSD_XBC].set(conv_c_w).at[:, C_XA:C_XA + D_MODEL].set(conv_a_w)
    cb = jnp.zeros((1, N_TOT), F32)
    cb = cb.at[0, C_XBC:C_XBC + SSD_XBC].set(conv_c_b).at[0, C_XA:C_XA + D_MODEL].set(conv_a_b)
    return cw, cb


def _scan_tables(gla_alpha_up, ssd_dt_bias, ssd_a_log):
    wup = jnp.zeros((2, SM_W, GLA_DK), F32).at[:, SM_LOW:SM_LOW + GLA_RANK, :].set(gla_alpha_up)
    dtb = jnp.zeros((2, SM_W), F32).at[:, SM_DT:SM_DT + SSD_HEADS].set(ssd_dt_bias)
    a_log = jnp.zeros((2, SM_W), F32).at[:, SM_DT:SM_DT + SSD_HEADS].set(ssd_a_log)
    valid = np.zeros((2, SM_W), np.float32)
    valid[:, SM_DT:SM_DT + SSD_HEADS] = 1.0
    e3 = np.zeros((SM_W, SSD_INNER), np.float32)
    for k in range(3 * SSD_HEADS):
        h = k % SSD_HEADS
        e3[k, h * SSD_P:(h + 1) * SSD_P] = 1.0
    return wup.astype(BF16), jnp.asarray(e3, BF16), dtb, a_log, jnp.asarray(valid)


def _to_col_major(h, rows):
    b, length, dm = h.shape
    return h.reshape(b, rows, GRID_W, dm).transpose(0, 2, 1, 3).reshape(b, length, dm)


def _from_col_major(h, rows):
    b, length, dm = h.shape
    return h.reshape(b, GRID_W, rows, dm).transpose(0, 2, 1, 3).reshape(b, length, dm)


def _neg_exp_kernel(a_ref, v_ref, o_ref):
    o_ref[...] = -jnp.exp(a_ref[...]) * v_ref[...]


def kernel(x, c, ctx, c_ctx, ada_w, ada_b, pre_g, post_g, w_in, conv_a_w, conv_a_b, lru_wr, lru_br,
           lru_wi, lru_bi, lru_lam, gla_alpha_up, gla_alpha_b, gla_norm_g, conv_c_w, conv_c_b,
           ssd_a_log, ssd_dt_bias, ssd_d, ssd_norm_g, w_pa, w_pb, w_pc, w_out):
    b, seq, d = x.shape
    lc = ctx.shape[1]
    depth = w_in.shape[0]
    rows = seq // GRID_W
    assert d == D_MODEL and lc % SCAN_TB == 0 and seq % SCAN_TB == 0 and b % 8 == 0

    pad_rows = (-(b + 1)) % 8
    cc = jnp.concatenate([c, c_ctx[None, :], jnp.zeros((pad_rows, d), F32)], axis=0)
    mods = _adaln(cc, ada_w, ada_b)

    w_in16 = w_in.astype(BF16)
    x_lat, x_ctx = x, ctx
    for l in range(depth):
        col_major = l % 2 == 1
        line = rows if col_major else GRID_W
        mod_lat = mods[l, :b].reshape(b, 1, 3 * d)
        mod_ctx = mods[l, b:b + 1]
        lat = _to_col_major(x_lat, rows) if col_major else x_lat

        cw, cb = _conv_tables(conv_a_w[l], conv_a_b[l], conv_c_w[l], conv_c_b[l])
        u = _in_proj(x_ctx, lat, mod_lat, mod_ctx, pre_g[l].reshape(1, d), _reorder_in_weights(w_in16[l]),
                     cw, cb, line=line)

        ya_f, ya_b = _lru(u, lru_wr[l].astype(BF16), lru_wi[l].astype(BF16), lru_br[l], lru_bi[l],
                          lru_lam[l], lc=lc)

        wup, e3, dtb, a_log, valid = _scan_tables(gla_alpha_up[l], ssd_dt_bias[l], ssd_a_log[l])
        na = pl.pallas_call(_neg_exp_kernel, out_shape=jax.ShapeDtypeStruct(a_log.shape, F32),
                            name="ssd_neg_a")(a_log, valid)
        sd_x = jnp.repeat(ssd_d[l], SSD_P).reshape(1, SSD_INNER)
        yb_f, yc_f, yb_b, yc_b = _scan(u, wup, gla_alpha_b[l], e3, dtb, na, sd_x, lc=lc)

        x_ctx, lat = _finish(ya_f, ya_b, yb_f, yb_b, yc_f, yc_b, u, x_ctx, lat, mod_lat, mod_ctx,
                             jnp.tile(gla_norm_g[l], GLA_HEADS).reshape(1, GLA_DV),
                             ssd_norm_g[l].reshape(1, SSD_INNER),
                             post_g[l].reshape(1, d),
                             w_pa[l].astype(BF16), w_pb[l].astype(BF16), w_pc[l].astype(BF16),
                             w_out[l].astype(BF16))
        x_lat = _from_col_major(lat, rows) if col_major else lat
    return x_lat
```

```python
import functools

import jax
import jax.numpy as jnp
import numpy as np
from jax import lax
from jax.experimental import pallas as pl
from jax.experimental.pallas import tpu as pltpu

F32 = jnp.float32
BF16 = jnp.bfloat16

D_MODEL = 1024
GRID_W = 64
CONV_W = 4
EPS = 1e-6
LRU_BW = 128
LRU_C = 8.0
LOG2E = 1.4426950408889634
GLA_HEADS = 4
GLA_DK = 512
GLA_DV = 1024
GLA_DKH = 128
GLA_DVH = 256
GLA_RANK = 16
GLA_TAU = 16.0
SSD_INNER = 2048
SSD_P = 64
SSD_HEADS = 32
SSD_N = 128
SSD_G = 4
SSD_GW = SSD_INNER // SSD_G
SSD_XBC = SSD_INNER + 2 * SSD_G * SSD_N
CHUNK = 64

TN = 512
C_XBC, C_MRG, C_Z, C_XA, C_V, C_LG, C_GG, C_Q, C_K, C_SM = (
    0, 3072, 6144, 8192, 9216, 10240, 11264, 12288, 12800, 13312)
N_TOT = C_SM + TN

VMEM_LIMIT = 56 * 1024 * 1024


def _sigmoid(x):
    return 0.5 * jnp.tanh(0.5 * x) + 0.5


def _silu(x):
    return x * _sigmoid(x)


def _softplus(x):
    return jnp.maximum(x, 0.0) + jnp.log1p(jnp.exp(-jnp.abs(x)))


def _dot(a, b):
    return jnp.dot(a, b, preferred_element_type=F32)


def _dot_nt(a, b):
    return lax.dot_general(a, b, (((1,), (1,)), ((), ())), preferred_element_type=F32)


def _dot_tn(a, b):
    return lax.dot_general(a, b, (((0,), (0,)), ((), ())), preferred_element_type=F32)


def _adaln_kernel(c_ref, w_ref, b_ref, o_ref):
    c = c_ref[...]
    o_ref[0] = _dot(_silu(c).astype(BF16), w_ref[0]) + b_ref[0]


def _adaln(cc, ada_w, ada_b):
    depth, d, n3 = ada_w.shape
    rows = cc.shape[0]
    tn = 1024
    return pl.pallas_call(
        _adaln_kernel,
        grid=(depth, n3 // tn),
        in_specs=[
            pl.BlockSpec((rows, d), lambda l, n: (0, 0)),
            pl.BlockSpec((1, d, tn), lambda l, n: (l, 0, n)),
            pl.BlockSpec((1, 1, tn), lambda l, n: (l, 0, n)),
        ],
        out_specs=pl.BlockSpec((1, rows, tn), lambda l, n: (l, 0, n)),
        out_shape=jax.ShapeDtypeStruct((depth, rows, n3), F32),
        name="adaln",
    )(cc, ada_w.astype(BF16), ada_b.reshape(depth, 1, n3))


CONV_PAD = 8
CONV_SLABS = TN // 128
CONV_PARTS = 4

def _in_kernel(xc_ref, xl_ref, ml_ref, mc_ref, pg_ref, w_ref, cw_ref, cb_ref, u_ref, h_s, conv_s, *, line):
    j = pl.program_id(1)
    lc = xc_ref.shape[1]
    tt = lc + xl_ref.shape[1]
    d = xc_ref.shape[2]
    nlines = (tt - lc) // line
    pitch = line + CONV_PAD
    lat0 = CONV_PAD + lc

    @pl.when(j == 0)
    def _():
        def norm_mod(x, mod):
            y = x * lax.rsqrt(jnp.mean(x * x, axis=-1, keepdims=True) + EPS) * pg_ref[...]
            return (y * (1.0 + mod[:, d:2 * d]) + mod[:, 0:d]).astype(BF16)
        h_s[0:lc, :] = norm_mod(xc_ref[0], mc_ref[...])
        h_s[lc:tt, :] = norm_mod(xl_ref[0], ml_ref[0])
        zeros = jnp.zeros((CONV_SLABS, CONV_PAD, 128), F32)
        conv_s[:, 0:CONV_PAD, :] = zeros
        for i in range(nlines + 1):
            r = lat0 + i * pitch
            conv_s[:, r:r + CONV_PAD, :] = zeros

    def conv(act):
        def taps(n, r, rows):
            ls = slice(n * 128, (n + 1) * 128)
            cw = cw_ref[:, ls]
            out = cb_ref[:, ls] + cw[1:2, :] * conv_s[n, r:r + rows, :]
            out = out + cw[0:1, :] * conv_s[n, r - 1:r - 1 + rows, :]
            out = out + cw[2:3, :] * conv_s[n, r + 1:r + 1 + rows, :]
            out = out + cw[3:4, :] * conv_s[n, r + 2:r + 2 + rows, :]
            return act(out)

        lines = [(0, CONV_PAD, lc)] + [(lc + i * line, lat0 + i * pitch + CONV_PAD, line) for i in range(nlines)]
        bounds = [0] + [lc + ((p * nlines) // CONV_PARTS) * line for p in range(1, CONV_PARTS)] + [tt]
        for p in range(CONV_PARTS):
            t0, t1 = bounds[p], bounds[p + 1]
            up = _dot(h_s[t0:t1, :], w_ref[...])
            part = [ln for ln in lines if t0 <= ln[0] < t1]
            for n in range(CONV_SLABS):
                for t, r, rows in part:
                    conv_s[n, r:r + rows, :] = up[t - t0:t - t0 + rows, n * 128:(n + 1) * 128]
            for t, r, rows in part:
                for n in range(CONV_SLABS):
                    u_ref[0, 0, t:t + rows, n * 128:(n + 1) * 128] = taps(n, r, rows)

    is_xbc = j < SSD_XBC // TN
    is_xa = (j >= C_XA // TN) & (j < (C_XA + D_MODEL) // TN)

    @pl.when(is_xbc)
    def _():
        conv(_silu)

    @pl.when(is_xa)
    def _():
        conv(lambda v: v)

    @pl.when(jnp.logical_not(is_xbc | is_xa))
    def _():
        u_ref[0, 0] = _dot(h_s[...], w_ref[...])


def _in_proj(x_ctx, x_lat, mod_lat, mod_ctx, pre_g, w_in_p, cw, cb, *, line):
    b, lc, d = x_ctx.shape
    seq = x_lat.shape[1]
    tt = lc + seq
    assert line % 8 == 0 and seq % line == 0 and lc % 8 == 0
    conv_rows = CONV_PAD + lc + (seq // line) * (line + CONV_PAD) + CONV_PAD
    return pl.pallas_call(
        functools.partial(_in_kernel, line=line),
        grid=(b, N_TOT // TN),
        in_specs=[
            pl.BlockSpec((1, lc, d), lambda i, j: (i, 0, 0)),
            pl.BlockSpec((1, seq, d), lambda i, j: (i, 0, 0)),
            pl.BlockSpec((1, 1, 3 * d), lambda i, j: (i, 0, 0)),
            pl.BlockSpec((1, 3 * d), lambda i, j: (0, 0)),
            pl.BlockSpec((1, d), lambda i, j: (0, 0)),
            pl.BlockSpec((d, TN), lambda i, j: (0, j)),
            pl.BlockSpec((CONV_W, TN), lambda i, j: (0, j)),
            pl.BlockSpec((1, TN), lambda i, j: (0, j)),
        ],
        out_specs=pl.BlockSpec((1, 1, tt, TN), lambda i, j: (i, j, 0, 0)),
        out_shape=jax.ShapeDtypeStruct((b, N_TOT // TN, tt, TN), F32),
        scratch_shapes=[pltpu.VMEM((tt, d), BF16), pltpu.VMEM((CONV_SLABS, conv_rows, 128), F32)],
        compiler_params=pltpu.CompilerParams(
            dimension_semantics=("arbitrary", "arbitrary"), vmem_limit_bytes=VMEM_LIMIT),
        name="in_proj",
    )(x_ctx, x_lat, mod_lat, mod_ctx, pre_g, w_in_p, cw, cb)


LRU_TC = 64
LRU_WT = 512
LRU_PITCH = LRU_TC + 8


def _lru_kernel(xf_ref, xb_ref, wr_ref, wi_ref, br_ref, bi_ref, lam_ref, yf_ref, yb_ref,
                a_s, b_s, o_s, h_s):
    i = pl.program_id(1)
    nb, _, tc, wt = xf_ref.shape

    @pl.when(i == 0)
    def _():
        h_s[...] = jnp.zeros_like(h_s)

    def gates(bi, carry):
        r0 = pl.multiple_of(bi * LRU_PITCH, 8)
        for dr, x_ref in enumerate((xf_ref, xb_ref)):
            for n in range(wt // LRU_BW):
                sl = slice(n * LRU_BW, (n + 1) * LRU_BW)
                xs = x_ref[bi, 0, :, sl]
                xs16 = xs.astype(BF16)
                hc = (0.5 * LRU_C) * _softplus(-lam_ref[dr:dr + 1, sl])
                nla = hc * jnp.tanh(0.5 * (_dot(xs16, wr_ref[dr, n]) + br_ref[dr:dr + 1, sl])) + hc
                g = _sigmoid(_dot(xs16, wi_ref[dr, n]) + bi_ref[dr:dr + 1, sl])
                a = jnp.exp2(nla * (-LOG2E))
                m2 = jnp.tanh(nla) * (a * a + 1.0)
                root = jnp.where(m2 > 0.0, m2 * lax.rsqrt(m2), 0.0)
                a_s[dr, n, pl.ds(r0, tc), :] = a
                b_s[dr, n, pl.ds(r0, tc), :] = root * (g * xs)
        return carry
    lax.fori_loop(0, nb, gates, 0, unroll=2)

    nl = wt // LRU_BW
    hf = [h_s[0, n] for n in range(nl)]
    hb = [h_s[1, n] for n in range(nl)]
    for t in range(tc):
        rows_f = pl.ds(t, nb, stride=LRU_PITCH)
        rows_b = pl.ds(tc - 1 - t, nb, stride=LRU_PITCH)
        for n in range(nl):
            hf[n] = a_s[0, n, rows_f, :] * hf[n] + b_s[0, n, rows_f, :]
            hb[n] = a_s[1, n, rows_b, :] * hb[n] + b_s[1, n, rows_b, :]
            o_s[0, n, rows_f, :] = hf[n]
            o_s[1, n, rows_b, :] = hb[n]
    for n in range(nl):
        h_s[0, n] = hf[n]
        h_s[1, n] = hb[n]

    def emit(bi, carry):
        r0 = pl.multiple_of(bi * LRU_PITCH, 8)
        for n in range(nl):
            sl = slice(n * LRU_BW, (n + 1) * LRU_BW)
            yf_ref[bi, :, sl] = o_s[0, n, pl.ds(r0, tc), :]
            yb_ref[bi, :, sl] = o_s[1, n, pl.ds(r0, tc), :]
        return carry
    lax.fori_loop(0, nb, emit, 0)


def _bwd_block(i, nctx, ntot):
    return jnp.where(i < nctx, nctx - 1 - i, ntot - 1 + nctx - i)


def _lru(u, wr, wi, br, bi, lam, *, lc):
    b, _, tt, _ = u.shape
    w = D_MODEL
    nt = tt // LRU_TC
    nctx = lc // LRU_TC
    assert LRU_WT == TN
    c0 = C_XA // TN
    nblk = LRU_WT // LRU_BW
    y_spec_f = pl.BlockSpec((b, LRU_TC, LRU_WT), lambda j, i: (0, i, j))
    y_spec_b = pl.BlockSpec((b, LRU_TC, LRU_WT), lambda j, i: (0, _bwd_block(i, nctx, nt), j))
    return pl.pallas_call(
        _lru_kernel,
        grid=(w // LRU_WT, nt),
        in_specs=[
            pl.BlockSpec((b, 1, LRU_TC, LRU_WT), lambda j, i: (0, c0 + j, i, 0)),
            pl.BlockSpec((b, 1, LRU_TC, LRU_WT), lambda j, i: (0, c0 + j, _bwd_block(i, nctx, nt), 0)),
            pl.BlockSpec((2, nblk, LRU_BW, LRU_BW), lambda j, i: (0, j, 0, 0)),
            pl.BlockSpec((2, nblk, LRU_BW, LRU_BW), lambda j, i: (0, j, 0, 0)),
            pl.BlockSpec((2, LRU_WT), lambda j, i: (0, j)),
            pl.BlockSpec((2, LRU_WT), lambda j, i: (0, j)),
            pl.BlockSpec((2, LRU_WT), lambda j, i: (0, j)),
        ],
        out_specs=[y_spec_f, y_spec_b],
        out_shape=[jax.ShapeDtypeStruct((b, tt, w), F32)] * 2,
        scratch_shapes=[
            pltpu.VMEM((2, nblk, b * LRU_PITCH, LRU_BW), F32),
            pltpu.VMEM((2, nblk, b * LRU_PITCH, LRU_BW), F32),
            pltpu.VMEM((2, nblk, b * LRU_PITCH, LRU_BW), F32),
            pltpu.VMEM((2, nblk, b, LRU_BW), F32),
        ],
        compiler_params=pltpu.CompilerParams(
            dimension_semantics=("arbitrary", "arbitrary"), vmem_limit_bytes=VMEM_LIMIT),
        name="lru_scan",
    )(u, u, wr, wi, br, bi, lam)


SCAN_TB = 256
SM_W = 128
SM_DT = 0
SM_LOW = SSD_HEADS
BD_HEADS = 4
BD_W = BD_HEADS * SSD_P
LOG_FLOOR = -1e30


T_SSD_B = SSD_INNER // TN
T_SSD_C = T_SSD_B + 1


def _seg_spec(rows, c0, width, blk):
    nt = width // TN
    assert width % TN == 0 and (c0 // TN) % nt == 0
    return pl.BlockSpec((1, nt, rows, TN), lambda bi, i: (bi, c0 // width, blk(i), 0))


def _wide(ref, rs, ntiles=None):
    nt = ref.shape[1] if ntiles is None else ntiles
    return jnp.concatenate([ref[0, t, rs, :] for t in range(nt)], axis=1)


def _split3(x):
    hi = x.astype(BF16).astype(F32)
    r = x - hi
    mid = r.astype(BF16).astype(F32)
    return hi, mid, r - mid


def _cumsum_rows(tri16, x):
    n = x.shape[1]
    p = _dot(tri16, jnp.concatenate([s.astype(BF16) for s in _split3(x)], axis=1))
    return (p[:, 0:n] + p[:, n:2 * n]) + p[:, 2 * n:3 * n]


def _pack3(x, lane):
    hi, mid, lo = _split3(x)
    h = SSD_HEADS
    return jnp.where(lane < h, hi,
                     jnp.where(lane < 2 * h, pltpu.roll(mid, h, 1),
                               jnp.where(lane < 3 * h, pltpu.roll(lo, 2 * h, 1), 0.0)))


def _scan_kernel(qf_ref, kf_ref, vf_ref, xf_ref, sf_ref, qb_ref, kb_ref, vb_ref, xb_ref, sb_ref,
                 wup_ref, gb_ref, e3_ref, dtb_ref, na_ref, sd_ref,
                 ybf_ref, ycf_ref, ybb_ref, ycb_ref, gla_s, ssd_s):
    i = pl.program_id(1)

    @pl.when(i == 0)
    def _():
        gla_s[...] = jnp.zeros_like(gla_s)
        ssd_s[...] = jnp.zeros_like(ssd_s)

    tb = qf_ref.shape[2]
    nch = tb // CHUNK
    row = lax.broadcasted_iota(jnp.int32, (CHUNK, CHUNK), 0)
    col = lax.broadcasted_iota(jnp.int32, (CHUNK, CHUNK), 1)
    row_x = lax.broadcasted_iota(jnp.int32, (CHUNK, SSD_INNER), 0)
    col_x = lax.broadcasted_iota(jnp.int32, (CHUNK, SSD_INNER), 1) & (CHUNK - 1)
    r2 = lax.broadcasted_iota(jnp.int32, (BD_W, BD_W), 0)
    c2 = lax.broadcasted_iota(jnp.int32, (BD_W, BD_W), 1)
    blk_mask = (r2 // CHUNK) == (c2 // SSD_P)
    lane = lax.broadcasted_iota(jnp.int32, (4 * CHUNK, SM_W), 1)
    prow = lax.broadcasted_iota(jnp.int32, (3 * CHUNK, SM_W), 0)
    plane = lax.broadcasted_iota(jnp.int32, (3 * CHUNK, SM_W), 1)
    ones_part = jnp.where((prow < CHUNK) & (plane < 3 * SSD_HEADS), 1.0, 0.0).astype(BF16)
    heads = range(GLA_HEADS)
    groups = range(SSD_G)

    def ksl(h):
        return slice(h * GLA_DKH, (h + 1) * GLA_DKH)

    def vsl(h):
        return slice(h * GLA_DVH, (h + 1) * GLA_DVH)

    def gsl(g):
        return slice(g * SSD_GW, (g + 1) * SSD_GW)

    dirs = (
        (0, qf_ref, kf_ref, vf_ref, xf_ref, sf_ref, ybf_ref, ycf_ref),
        (1, qb_ref, kb_ref, vb_ref, xb_ref, sb_ref, ybb_ref, ycb_ref),
    )
    def prepare(step):
        insts = []
        for dr, q_ref, k_ref, v_ref, x_ref, s_ref, yb_ref, yc_ref in dirs:
            fwd = dr == 0
            c = step if fwd else nch - 1 - step
            causal = (row >= col) if fwd else (row <= col)
            insts.append(dict(
                dr=dr, rs=slice(c * CHUNK, (c + 1) * CHUNK), last=CHUNK - 1 if fwd else 0,
                causal=causal, causal_x=(row_x >= col_x) if fwd else (row_x <= col_x),
                tri16=jnp.where(causal, 1.0, 0.0).astype(BF16),
                q_ref=q_ref, k_ref=k_ref, v_ref=v_ref, x_ref=x_ref, s_ref=s_ref, yb_ref=yb_ref, yc_ref=yc_ref))

        for it in insts:
            dr, rs, x_ref = it["dr"], it["rs"], it["x_ref"]
            it["sm"] = it["s_ref"][0, 0, rs, :]
            it["z"] = _dot(it["sm"].astype(BF16), wup_ref[dr])
            it["bm16"] = [x_ref[0, T_SSD_B, rs, g * SSD_N:(g + 1) * SSD_N].astype(BF16) for g in groups]
            it["cm16"] = [x_ref[0, T_SSD_C, rs, g * SSD_N:(g + 1) * SSD_N].astype(BF16) for g in groups]
            it["cb"] = [_dot_nt(it["cm16"][g], it["bm16"][g]) for g in groups]

        for it in insts:
            dr, rs, last = it["dr"], it["rs"], it["last"]
            logg = -_softplus(-(it["z"] + gb_ref[dr:dr + 1, :])) * (1.0 / GLA_TAU)
            bc = _cumsum_rows(it["tri16"], logg)
            btot = bc[last:last + 1, :]
            k = it["k_ref"][0, 0, rs, :]
            it["qe16"] = ((it["q_ref"][0, 0, rs, :] * (GLA_DKH ** -0.5)) * jnp.exp(bc)).astype(BF16)
            it["ke16"] = (k * jnp.exp(-bc)).astype(BF16)
            it["kd16"] = (k * jnp.exp(btot - bc)).astype(BF16)
            it["etot"] = jnp.exp(btot)
            it["v16"] = _wide(it["v_ref"], rs).astype(BF16)

            dt = _softplus(it["sm"] + dtb_ref[dr:dr + 1, :])
            cum = _cumsum_rows(it["tri16"], dt * na_ref[dr:dr + 1, :])
            clast = cum[last:last + 1, :]
            wrow = jnp.maximum(jnp.log(dt), LOG_FLOOR) - cum
            packed = _pack3(jnp.concatenate([cum, jnp.exp(cum), dt * jnp.exp(clast - cum), wrow], axis=0), lane)
            pt = packed[3 * CHUNK:4 * CHUNK].T
            pt2 = jnp.concatenate([pt, pt], axis=1).astype(BF16)
            rm = jnp.concatenate([pt2] * (SSD_INNER // 128), axis=1) * e3_ref[...]
            it["a16"] = jnp.concatenate([packed[0:3 * CHUNK].astype(BF16), ones_part], axis=1)
            it["rhs"] = jnp.concatenate([e3_ref[...], rm], axis=0)
        return insts

    def advance(insts):
        for it in insts:
            it["big"] = _dot(it["a16"], it["rhs"])
            it["att"] = [_dot_nt(it["qe16"][:, ksl(h)], it["ke16"][:, ksl(h)]) for h in heads]

        for it in insts:
            big = it["big"]
            lmat = jnp.exp(jnp.where(it["causal_x"], big[0:CHUNK], -jnp.inf))
            m16 = []
            for g in groups:
                cb2 = jnp.concatenate([it["cb"][g], it["cb"][g]], axis=1)
                m16.append(jnp.concatenate(
                    [cb2 * lmat[:, g * SSD_GW + q * 128:g * SSD_GW + (q + 1) * 128] for q in range(SSD_GW // 128)],
                    axis=1).astype(BF16))
            it["m16"] = m16
            it["ecum_x"] = big[CHUNK:2 * CHUNK]
            xs = _wide(it["x_ref"], it["rs"], SSD_INNER // TN)
            it["xs16"] = xs.astype(BF16)
            it["xdec16"] = (xs * big[2 * CHUNK:3 * CHUNK]).astype(BF16)
            it["att16"] = [jnp.where(it["causal"], a, 0.0).astype(BF16) for a in it["att"]]

        for it in insts:
            dr = it["dr"]
            it["o_in"] = [_dot(it["att16"][h], it["v16"][:, vsl(h)]) for h in heads]
            it["o_st"] = [_dot_nt(it["qe16"][:, ksl(h)], gla_s[dr, h].astype(BF16)) for h in heads]
            it["u_gla"] = [_dot_tn(it["v16"][:, vsl(h)], it["kd16"][:, ksl(h)]) for h in heads]
            y_in = []
            for g in groups:
                parts = []
                for pr in range(SSD_GW // BD_W):
                    ls = slice(g * SSD_GW + pr * BD_W, g * SSD_GW + (pr + 1) * BD_W)
                    xh = it["xs16"][:, ls]
                    bd = jnp.where(blk_mask, jnp.concatenate([xh] * BD_HEADS, axis=0), jnp.zeros((), BF16))
                    parts.append(_dot(it["m16"][g][:, pr * BD_W:(pr + 1) * BD_W], bd))
                y_in.append(jnp.concatenate(parts, axis=1))
            it["y_in"] = y_in
            it["y_st"] = [_dot(it["cm16"][g], ssd_s[dr, g].astype(BF16)) for g in groups]
            it["u_ssd"] = [_dot_tn(it["bm16"][g], it["xdec16"][:, gsl(g)]) for g in groups]

        for it in insts:
            dr, rs, last = it["dr"], it["rs"], it["last"]
            for h in heads:
                it["yb_ref"][0, rs, vsl(h)] = it["o_in"][h] + it["o_st"][h]
                gla_s[dr, h] = gla_s[dr, h] * it["etot"][:, ksl(h)] + it["u_gla"][h]
            for g in groups:
                ecum = it["ecum_x"][:, gsl(g)]
                yc = it["y_in"][g] + it["y_st"][g] * ecum
                if dr == 0:
                    yc = yc + sd_ref[:, gsl(g)] * it["x_ref"][0, g, rs, :]
                it["yc_ref"][0, rs, gsl(g)] = yc
                ssd_s[dr, g] = ssd_s[dr, g] * ecum[last:last + 1, :] + it["u_ssd"][g]

    ready = prepare(0)
    for step in range(nch):
        insts = ready
        if step + 1 < nch:
            ready = prepare(step + 1)
        advance(insts)


def _scan(u, wup, gb, e3, dtb, na, sd_x, *, lc):
    b, _, tt, _ = u.shape
    tb = SCAN_TB
    nt = tt // tb
    nctx = lc // tb

    def fblk(i):
        return i

    def bblk(i):
        return _bwd_block(i, nctx, nt)

    def fmap(cb):
        return lambda bi, i: (bi, i, cb)

    def bmap(cb):
        return lambda bi, i: (bi, _bwd_block(i, nctx, nt), cb)

    def tok_specs(blk, dr):
        return [
            _seg_spec(tb, C_Q, GLA_DK, blk), _seg_spec(tb, C_K, GLA_DK, blk), _seg_spec(tb, C_V, GLA_DV, blk),
            _seg_spec(tb, C_XBC, SSD_XBC, blk),
            pl.BlockSpec((1, 1, tb, SM_W), lambda bi, i: (bi, C_SM // TN, blk(i), dr)),
        ]

    def const_spec(a):
        nd = a.ndim
        return pl.BlockSpec(a.shape, lambda bi, i: (0,) * nd)

    assert SSD_GW == TN
    consts = (wup, gb, e3, dtb, na, sd_x)
    return pl.pallas_call(
        _scan_kernel,
        grid=(b, nt),
        in_specs=tok_specs(fblk, 0) + tok_specs(bblk, 1) + [const_spec(a) for a in consts],
        out_specs=[
            pl.BlockSpec((1, tb, GLA_DV), fmap(0)),
            pl.BlockSpec((1, tb, SSD_INNER), fmap(0)),
            pl.BlockSpec((1, tb, GLA_DV), bmap(0)),
            pl.BlockSpec((1, tb, SSD_INNER), bmap(0)),
        ],
        out_shape=[
            jax.ShapeDtypeStruct((b, tt, GLA_DV), F32),
            jax.ShapeDtypeStruct((b, tt, SSD_INNER), F32),
            jax.ShapeDtypeStruct((b, tt, GLA_DV), F32),
            jax.ShapeDtypeStruct((b, tt, SSD_INNER), F32),
        ],
        scratch_shapes=[
            pltpu.VMEM((2, GLA_HEADS, GLA_DVH, GLA_DKH), F32),
            pltpu.VMEM((2, SSD_G, SSD_N, SSD_GW), F32),
        ],
        compiler_params=pltpu.CompilerParams(
            dimension_semantics=("arbitrary", "arbitrary"), vmem_limit_bytes=VMEM_LIMIT),
        name="gla_ssd_scan",
    )(u, u, u, u, u, u, u, u, u, u, *consts)


FIN_TM = 256


def _group_rms(x, width):
    parts = []
    for s in range(x.shape[1] // width):
        xs = x[:, s * width:(s + 1) * width]
        parts.append(xs * lax.rsqrt(jnp.mean(xs * xs, axis=-1, keepdims=True) + EPS))
    return jnp.concatenate(parts, axis=1)


def _fin_kernel(yaf_ref, yab_ref, ybf_ref, ybb_ref, ycf_ref, ycb_ref, lg_ref, gg_ref, z_ref, mg_ref,
                xc_ref, xl_ref, ml_ref, mc_ref, gng_ref, sng_ref, pog_ref,
                wpa_ref, wpb_ref, wpc_ref, wo_ref, oc_ref, ol_ref, *, nctx):
    i = pl.program_id(1)
    d = D_MODEL
    rows = slice(None)
    ya = yaf_ref[0] + yab_ref[0]
    pa = _dot((ya * _silu(_wide(lg_ref, rows))).astype(BF16), wpa_ref[...])
    yb = ybf_ref[0] + ybb_ref[0]
    ob = (_group_rms(yb, GLA_DVH) * gng_ref[...]) * _silu(_wide(gg_ref, rows))
    pb = _dot(ob.astype(BF16), wpb_ref[...])
    yc = ycf_ref[0] + ycb_ref[0]
    oc = _group_rms(yc * _silu(_wide(z_ref, rows)), SSD_GW) * sng_ref[...]
    pc = _dot(oc.astype(BF16), wpc_ref[...])
    gates = _sigmoid(_wide(mg_ref, rows))
    merged = gates[:, 0:d] * pa + gates[:, d:2 * d] * pb + gates[:, 2 * d:3 * d] * pc
    out = _dot(merged.astype(BF16), wo_ref[...])
    normed = _group_rms(out, d) * pog_ref[...]

    @pl.when(i < nctx)
    def _():
        oc_ref[0] = xc_ref[0] + mc_ref[:, 2 * d:3 * d] * normed

    @pl.when(i >= nctx)
    def _():
        ol_ref[0] = xl_ref[0] + ml_ref[0][:, 2 * d:3 * d] * normed


def _finish(ya_f, ya_b, yb_f, yb_b, yc_f, yc_b, u, x_ctx, x_lat, mod_lat, mod_ctx,
            gng, sng, post_g, w_pa, w_pb, w_pc, w_out):
    b, lc, d = x_ctx.shape
    seq = x_lat.shape[1]
    tm = FIN_TM
    nctx = lc // tm
    ctx_spec = pl.BlockSpec((1, tm, d), lambda bi, i: (bi, jnp.minimum(i, nctx - 1), 0))
    lat_spec = pl.BlockSpec((1, tm, d), lambda bi, i: (bi, jnp.maximum(i - nctx, 0), 0))

    def tok(width):
        return pl.BlockSpec((1, tm, width), lambda bi, i: (bi, i, 0))

    def blk(i):
        return i

    def const_spec(a):
        nd = a.ndim
        return pl.BlockSpec(a.shape, lambda bi, i: (0,) * nd, pipeline_mode=pl.Buffered(1))

    consts = (gng, sng, post_g, w_pa, w_pb, w_pc, w_out)
    return pl.pallas_call(
        functools.partial(_fin_kernel, nctx=nctx),
        grid=(b, (lc + seq) // tm),
        in_specs=[
            tok(d), tok(d), tok(GLA_DV), tok(GLA_DV), tok(SSD_INNER), tok(SSD_INNER),
            _seg_spec(tm, C_LG, d, blk), _seg_spec(tm, C_GG, GLA_DV, blk), _seg_spec(tm, C_Z, SSD_INNER, blk),
            _seg_spec(tm, C_MRG, 3 * d, blk), ctx_spec, lat_spec,
            pl.BlockSpec((1, 1, 3 * d), lambda bi, i: (bi, 0, 0)),
            pl.BlockSpec((1, 3 * d), lambda bi, i: (0, 0)),
        ] + [const_spec(a) for a in consts],
        out_specs=[ctx_spec, lat_spec],
        out_shape=[jax.ShapeDtypeStruct((b, lc, d), F32), jax.ShapeDtypeStruct((b, seq, d), F32)],
        compiler_params=pltpu.CompilerParams(
            dimension_semantics=("arbitrary", "arbitrary"), vmem_limit_bytes=VMEM_LIMIT),
        name="finish",
    )(ya_f, ya_b, yb_f, yb_b, yc_f, yc_b, u, u, u, u, x_ctx, x_lat, mod_lat, mod_ctx, *consts)


def _reorder_in_weights(w_in):
    d = w_in.shape[0]
    o = 0
    seg = {}
    for name, wd in (("xa", D_MODEL), ("lg", D_MODEL), ("q", GLA_DK), ("k", GLA_DK), ("v", GLA_DV),
                     ("gg", GLA_DV), ("low", 2 * GLA_RANK), ("z", SSD_INNER), ("xbc", SSD_XBC),
                     ("dt", 2 * SSD_HEADS), ("mrg", 3 * D_MODEL)):
        seg[name] = w_in[:, o:o + wd]
        o += wd
    pad = jnp.zeros((d, SM_W - GLA_RANK - SSD_HEADS), w_in.dtype)
    cols = [seg["xbc"], seg["mrg"], seg["z"], seg["xa"], seg["v"], seg["lg"], seg["gg"],
            seg["q"], seg["k"]]
    for dr in range(2):
        cols += [seg["dt"][:, dr * SSD_HEADS:(dr + 1) * SSD_HEADS],
                 seg["low"][:, dr * GLA_RANK:(dr + 1) * GLA_RANK], pad]
    cols.append(jnp.zeros((d, N_TOT - C_SM - 2 * SM_W), w_in.dtype))
    return jnp.concatenate(cols, axis=1)


def _conv_tables(conv_a_w, conv_a_b, conv_c_w, conv_c_b):
    cw = jnp.zeros((CONV_W, N_TOT), F32)
    cw = cw.at[:, C_XBC:C_XBC + SSD_XBC].set(conv_c_w).at[:, C_XA:C_XA + D_MODEL].set(conv_a_w)
    cb = jnp.zeros((1, N_TOT), F32)
    cb = cb.at[0, C_XBC:C_XBC + SSD_XBC].set(conv_c_b).at[0, C_XA:C_XA + D_MODEL].set(conv_a_b)
    return cw, cb


def _scan_tables(gla_alpha_up, ssd_dt_bias, ssd_a_log):
    wup = jnp.zeros((2, SM_W, GLA_DK), F32).at[:, SM_LOW:SM_LOW + GLA_RANK, :].set(gla_alpha_up)
    dtb = jnp.zeros((2, SM_W), F32).at[:, SM_DT:SM_DT + SSD_HEADS].set(ssd_dt_bias)
    a_log = jnp.zeros((2, SM_W), F32).at[:, SM_DT:SM_DT + SSD_HEADS].set(ssd_a_log)
    valid = np.zeros((2, SM_W), np.float32)
    valid[:, SM_DT:SM_DT + SSD_HEADS] = 1.0
    e3 = np.zeros((SM_W, SSD_INNER), np.float32)
    for k in range(3 * SSD_HEADS):
        h = k % SSD_HEADS
        e3[k, h * SSD_P:(h + 1) * SSD_P] = 1.0
    return wup.astype(BF16), jnp.asarray(e3, BF16), dtb, a_log, jnp.asarray(valid)


def _to_col_major(h, rows):
    b, length, dm = h.shape
    return h.reshape(b, rows, GRID_W, dm).transpose(0, 2, 1, 3).reshape(b, length, dm)


def _from_col_major(h, rows):
    b, length, dm = h.shape
    return h.reshape(b, GRID_W, rows, dm).transpose(0, 2, 1, 3).reshape(b, length, dm)


def _neg_exp_kernel(a_ref, v_ref, o_ref):
    o_ref[...] = -jnp.exp(a_ref[...]) * v_ref[...]


def kernel(x, c, ctx, c_ctx, ada_w, ada_b, pre_g, post_g, w_in, conv_a_w, conv_a_b, lru_wr, lru_br,
           lru_wi, lru_bi, lru_lam, gla_alpha_up, gla_alpha_b, gla_norm_g, conv_c_w, conv_c_b,
           ssd_a_log, ssd_dt_bias, ssd_d, ssd_norm_g, w_pa, w_pb, w_pc, w_out):
    b, seq, d = x.shape
    lc = ctx.shape[1]
    depth = w_in.shape[0]
    rows = seq // GRID_W
    assert d == D_MODEL and lc % SCAN_TB == 0 and seq % SCAN_TB == 0 and b % 8 == 0

    pad_rows = (-(b + 1)) % 8
    cc = jnp.concatenate([c, c_ctx[None, :], jnp.zeros((pad_rows, d), F32)], axis=0)
    mods = _adaln(cc, ada_w, ada_b)

    w_in16 = w_in.astype(BF16)
    x_lat, x_ctx = x, ctx
    for l in range(depth):
        col_major = l % 2 == 1
        line = rows if col_major else GRID_W
        mod_lat = mods[l, :b].reshape(b, 1, 3 * d)
        mod_ctx = mods[l, b:b + 1]
        lat = _to_col_major(x_lat, rows) if col_major else x_lat

        cw, cb = _conv_tables(conv_a_w[l], conv_a_b[l], conv_c_w[l], conv_c_b[l])
        u = _in_proj(x_ctx, lat, mod_lat, mod_ctx, pre_g[l].reshape(1, d), _reorder_in_weights(w_in16[l]),
                     cw, cb, line=line)

        ya_f, ya_b = _lru(u, lru_wr[l].astype(BF16), lru_wi[l].astype(BF16), lru_br[l], lru_bi[l],
                          lru_lam[l], lc=lc)

        wup, e3, dtb, a_log, valid = _scan_tables(gla_alpha_up[l], ssd_dt_bias[l], ssd_a_log[l])
        na = pl.pallas_call(_neg_exp_kernel, out_shape=jax.ShapeDtypeStruct(a_log.shape, F32),
                            name="ssd_neg_a")(a_log, valid)
        sd_x = jnp.repeat(ssd_d[l], SSD_P).reshape(1, SSD_INNER)
        yb_f, yc_f, yb_b, yc_b = _scan(u, wup, gla_alpha_b[l], e3, dtb, na, sd_x, lc=lc)

        x_ctx, lat = _finish(ya_f, ya_b, yb_f, yb_b, yc_f, yc_b, u, x_ctx, lat, mod_lat, mod_ctx,
                             jnp.tile(gla_norm_g[l], GLA_HEADS).reshape(1, GLA_DV),
                             ssd_norm_g[l].reshape(1, SSD_INNER),
                             post_g[l].reshape(1, d),
                             w_pa[l].astype(BF16), w_pb[l].astype(BF16), w_pc[l].astype(BF16),
                             w_out[l].astype(BF16))
        x_lat = _from_col_major(lat, rows) if col_major else lat
    return x_lat
```

```python
import functools

import jax
import jax.numpy as jnp
import numpy as np
from jax import lax
from jax.experimental import pallas as pl
from jax.experimental.pallas import tpu as pltpu

F32 = jnp.float32
BF16 = jnp.bfloat16

LANES = 128
SUBLANES = 8

D_MODEL = 1024
GRID_W = 64
CONV_W = 4
EPS = 1e-6
LRU_BW = 128
LRU_C = 8.0
LOG2E = 1.4426950408889634
GLA_HEADS = 4
GLA_DK = 512
GLA_DV = 1024
GLA_DKH = 128
GLA_DVH = 256
GLA_RANK = 16
GLA_TAU = 16.0
SSD_INNER = 2048
SSD_P = 64
SSD_HEADS = 32
SSD_N = 128
SSD_G = 4
SSD_GW = SSD_INNER // SSD_G
SSD_XBC = SSD_INNER + 2 * SSD_G * SSD_N
CHUNK = 64

TN = 512
C_XBC, C_MRG, C_Z, C_XA, C_V, C_LG, C_GG, C_Q, C_K, C_SM = (
    0, 3072, 6144, 8192, 9216, 10240, 11264, 12288, 12800, 13312)
N_TOT = C_SM + TN

VMEM_LIMIT = 56 * 1024 * 1024


def _sigmoid(x):
    return 0.5 * jnp.tanh(0.5 * x) + 0.5


def _silu(x):
    return x * _sigmoid(x)


def _softplus(x):
    return jnp.maximum(x, 0.0) + jnp.log1p(jnp.exp(-jnp.abs(x)))


def _dot(a, b):
    return jnp.dot(a, b, preferred_element_type=F32)


def _dot_nt(a, b):
    return lax.dot_general(a, b, (((1,), (1,)), ((), ())), preferred_element_type=F32)


def _dot_tn(a, b):
    return lax.dot_general(a, b, (((0,), (0,)), ((), ())), preferred_element_type=F32)


def _adaln_kernel(c_ref, w_ref, b_ref, o_ref):
    c = c_ref[...]
    o_ref[0] = _dot(_silu(c).astype(BF16), w_ref[0]) + b_ref[0]


def _adaln(cc, ada_w, ada_b):
    depth, d, n3 = ada_w.shape
    rows = cc.shape[0]
    tn = 1024
    return pl.pallas_call(
        _adaln_kernel,
        grid=(depth, n3 // tn),
        in_specs=[
            pl.BlockSpec((rows, d), lambda l, n: (0, 0)),
            pl.BlockSpec((1, d, tn), lambda l, n: (l, 0, n)),
            pl.BlockSpec((1, 1, tn), lambda l, n: (l, 0, n)),
        ],
        out_specs=pl.BlockSpec((1, rows, tn), lambda l, n: (l, 0, n)),
        out_shape=jax.ShapeDtypeStruct((depth, rows, n3), F32),
        name="adaln",
    )(cc, ada_w.astype(BF16), ada_b.reshape(depth, 1, n3))


CONV_PAD = SUBLANES
CONV_SLABS = TN // LANES
CONV_PARTS = 4

def _in_kernel(xc_ref, xl_ref, ml_ref, mc_ref, pg_ref, w_ref, cw_ref, cb_ref, u_ref, h_s, conv_s, *, line):
    j = pl.program_id(1)
    lc = xc_ref.shape[1]
    tt = lc + xl_ref.shape[1]
    d = xc_ref.shape[2]
    nlines = (tt - lc) // line
    pitch = line + CONV_PAD
    lat0 = CONV_PAD + lc

    @pl.when(j == 0)
    def _():
        def norm_mod(x, mod):
            y = x * lax.rsqrt(jnp.mean(x * x, axis=-1, keepdims=True) + EPS) * pg_ref[...]
            return (y * (1.0 + mod[:, d:2 * d]) + mod[:, 0:d]).astype(BF16)
        h_s[0:lc, :] = norm_mod(xc_ref[0], mc_ref[...])
        h_s[lc:tt, :] = norm_mod(xl_ref[0], ml_ref[0])
        zeros = jnp.zeros((CONV_SLABS, CONV_PAD, LANES), F32)
        conv_s[:, 0:CONV_PAD, :] = zeros
        for i in range(nlines + 1):
            r = lat0 + i * pitch
            conv_s[:, r:r + CONV_PAD, :] = zeros

    def conv(act):
        def taps(n, r, rows):
            ls = slice(n * LANES, (n + 1) * LANES)
            cw = cw_ref[:, ls]
            out = cb_ref[:, ls] + cw[1:2, :] * conv_s[n, r:r + rows, :]
            out = out + cw[0:1, :] * conv_s[n, r - 1:r - 1 + rows, :]
            out = out + cw[2:3, :] * conv_s[n, r + 1:r + 1 + rows, :]
            out = out + cw[3:4, :] * conv_s[n, r + 2:r + 2 + rows, :]
            return act(out)

        lines = [(0, CONV_PAD, lc)] + [(lc + i * line, lat0 + i * pitch + CONV_PAD, line) for i in range(nlines)]
        bounds = [0] + [lc + ((p * nlines) // CONV_PARTS) * line for p in range(1, CONV_PARTS)] + [tt]
        for p in range(CONV_PARTS):
            t0, t1 = bounds[p], bounds[p + 1]
            up = _dot(h_s[t0:t1, :], w_ref[...])
            part = [ln for ln in lines if t0 <= ln[0] < t1]
            for n in range(CONV_SLABS):
                for t, r, rows in part:
                    conv_s[n, r:r + rows, :] = up[t - t0:t - t0 + rows, n * LANES:(n + 1) * LANES]
            for t, r, rows in part:
                for n in range(CONV_SLABS):
                    u_ref[0, 0, t:t + rows, n * LANES:(n + 1) * LANES] = taps(n, r, rows)

    is_xbc = j < SSD_XBC // TN
    is_xa = (j >= C_XA // TN) & (j < (C_XA + D_MODEL) // TN)

    @pl.when(is_xbc)
    def _():
        conv(_silu)

    @pl.when(is_xa)
    def _():
        conv(lambda v: v)

    @pl.when(jnp.logical_not(is_xbc | is_xa))
    def _():
        u_ref[0, 0] = _dot(h_s[...], w_ref[...])


def _in_proj(x_ctx, x_lat, mod_lat, mod_ctx, pre_g, w_in_p, cw, cb, *, line):
    b, lc, d = x_ctx.shape
    seq = x_lat.shape[1]
    tt = lc + seq
    assert line % SUBLANES == 0 and seq % line == 0 and lc % SUBLANES == 0
    conv_rows = CONV_PAD + lc + (seq // line) * (line + CONV_PAD) + CONV_PAD
    return pl.pallas_call(
        functools.partial(_in_kernel, line=line),
        grid=(b, N_TOT // TN),
        in_specs=[
            pl.BlockSpec((1, lc, d), lambda i, j: (i, 0, 0)),
            pl.BlockSpec((1, seq, d), lambda i, j: (i, 0, 0)),
            pl.BlockSpec((1, 1, 3 * d), lambda i, j: (i, 0, 0)),
            pl.BlockSpec((1, 3 * d), lambda i, j: (0, 0)),
            pl.BlockSpec((1, d), lambda i, j: (0, 0)),
            pl.BlockSpec((d, TN), lambda i, j: (0, j)),
            pl.BlockSpec((CONV_W, TN), lambda i, j: (0, j)),
            pl.BlockSpec((1, TN), lambda i, j: (0, j)),
        ],
        out_specs=pl.BlockSpec((1, 1, tt, TN), lambda i, j: (i, j, 0, 0)),
        out_shape=jax.ShapeDtypeStruct((b, N_TOT // TN, tt, TN), F32),
        scratch_shapes=[pltpu.VMEM((tt, d), BF16), pltpu.VMEM((CONV_SLABS, conv_rows, LANES), F32)],
        compiler_params=pltpu.CompilerParams(
            dimension_semantics=("arbitrary", "arbitrary"), vmem_limit_bytes=VMEM_LIMIT),
        name="in_proj",
    )(x_ctx, x_lat, mod_lat, mod_ctx, pre_g, w_in_p, cw, cb)


LRU_TC = 64
LRU_WT = 512
LRU_PITCH = LRU_TC + SUBLANES


def _lru_kernel(xf_ref, xb_ref, wr_ref, wi_ref, br_ref, bi_ref, lam_ref, yf_ref, yb_ref,
                a_s, b_s, o_s, h_s):
    i = pl.program_id(1)
    nb, _, tc, wt = xf_ref.shape

    @pl.when(i == 0)
    def _():
        h_s[...] = jnp.zeros_like(h_s)

    def gates(bi, carry):
        r0 = pl.multiple_of(bi * LRU_PITCH, SUBLANES)
        for dr, x_ref in enumerate((xf_ref, xb_ref)):
            for n in range(wt // LRU_BW):
                sl = slice(n * LRU_BW, (n + 1) * LRU_BW)
                xs = x_ref[bi, 0, :, sl]
                xs16 = xs.astype(BF16)
                hc = (0.5 * LRU_C) * _softplus(-lam_ref[dr:dr + 1, sl])
                nla = hc * jnp.tanh(0.5 * (_dot(xs16, wr_ref[dr, n]) + br_ref[dr:dr + 1, sl])) + hc
                g = _sigmoid(_dot(xs16, wi_ref[dr, n]) + bi_ref[dr:dr + 1, sl])
                a = jnp.exp2(nla * (-LOG2E))
                m2 = jnp.tanh(nla) * (a * a + 1.0)
                root = jnp.where(m2 > 0.0, m2 * lax.rsqrt(m2), 0.0)
                a_s[dr, n, pl.ds(r0, tc), :] = a
                b_s[dr, n, pl.ds(r0, tc), :] = root * (g * xs)
        return carry
    lax.fori_loop(0, nb, gates, 0, unroll=4)

    nl = wt // LRU_BW
    hf = [h_s[0, n] for n in range(nl)]
    hb = [h_s[1, n] for n in range(nl)]
    for t in range(tc):
        rows_f = pl.ds(t, nb, stride=LRU_PITCH)
        rows_b = pl.ds(tc - 1 - t, nb, stride=LRU_PITCH)
        for n in range(nl):
            hf[n] = a_s[0, n, rows_f, :] * hf[n] + b_s[0, n, rows_f, :]
            hb[n] = a_s[1, n, rows_b, :] * hb[n] + b_s[1, n, rows_b, :]
            o_s[0, n, rows_f, :] = hf[n]
            o_s[1, n, rows_b, :] = hb[n]
    for n in range(nl):
        h_s[0, n] = hf[n]
        h_s[1, n] = hb[n]

    def emit(bi, carry):
        r0 = pl.multiple_of(bi * LRU_PITCH, SUBLANES)
        for n in range(nl):
            sl = slice(n * LRU_BW, (n + 1) * LRU_BW)
            yf_ref[bi, :, sl] = o_s[0, n, pl.ds(r0, tc), :]
            yb_ref[bi, :, sl] = o_s[1, n, pl.ds(r0, tc), :]
        return carry
    lax.fori_loop(0, nb, emit, 0)


def _bwd_block(i, nctx, ntot):
    return jnp.where(i < nctx, nctx - 1 - i, ntot - 1 + nctx - i)


def _lru(u, wr, wi, br, bi, lam, *, lc):
    b, _, tt, _ = u.shape
    w = D_MODEL
    nt = tt // LRU_TC
    nctx = lc // LRU_TC
    assert LRU_WT == TN
    c0 = C_XA // TN
    nblk = LRU_WT // LRU_BW
    y_spec_f = pl.BlockSpec((b, LRU_TC, LRU_WT), lambda j, i: (0, i, j))
    y_spec_b = pl.BlockSpec((b, LRU_TC, LRU_WT), lambda j, i: (0, _bwd_block(i, nctx, nt), j))
    return pl.pallas_call(
        _lru_kernel,
        grid=(w // LRU_WT, nt),
        in_specs=[
            pl.BlockSpec((b, 1, LRU_TC, LRU_WT), lambda j, i: (0, c0 + j, i, 0)),
            pl.BlockSpec((b, 1, LRU_TC, LRU_WT), lambda j, i: (0, c0 + j, _bwd_block(i, nctx, nt), 0)),
            pl.BlockSpec((2, nblk, LRU_BW, LRU_BW), lambda j, i: (0, j, 0, 0)),
            pl.BlockSpec((2, nblk, LRU_BW, LRU_BW), lambda j, i: (0, j, 0, 0)),
            pl.BlockSpec((2, LRU_WT), lambda j, i: (0, j)),
            pl.BlockSpec((2, LRU_WT), lambda j, i: (0, j)),
            pl.BlockSpec((2, LRU_WT), lambda j, i: (0, j)),
        ],
        out_specs=[y_spec_f, y_spec_b],
        out_shape=[jax.ShapeDtypeStruct((b, tt, w), F32)] * 2,
        scratch_shapes=[
            pltpu.VMEM((2, nblk, b * LRU_PITCH, LRU_BW), F32),
            pltpu.VMEM((2, nblk, b * LRU_PITCH, LRU_BW), F32),
            pltpu.VMEM((2, nblk, b * LRU_PITCH, LRU_BW), F32),
            pltpu.VMEM((2, nblk, b, LRU_BW), F32),
        ],
        compiler_params=pltpu.CompilerParams(
            dimension_semantics=("arbitrary", "arbitrary"), vmem_limit_bytes=VMEM_LIMIT),
        name="lru_scan",
    )(u, u, wr, wi, br, bi, lam)


SCAN_TB = 256
SM_W = LANES
SM_DT = 0
SM_LOW = SSD_HEADS
BD_HEADS = 4
BD_W = BD_HEADS * SSD_P
LOG_FLOOR = -1e30


T_SSD_B = SSD_INNER // TN
T_SSD_C = T_SSD_B + 1


def _seg_spec(rows, c0, width, blk):
    nt = width // TN
    assert width % TN == 0 and (c0 // TN) % nt == 0
    return pl.BlockSpec((1, nt, rows, TN), lambda bi, i: (bi, c0 // width, blk(i), 0))


def _wide(ref, rs, ntiles=None):
    nt = ref.shape[1] if ntiles is None else ntiles
    return jnp.concatenate([ref[0, t, rs, :] for t in range(nt)], axis=1)


def _split3(x):
    hi = x.astype(BF16).astype(F32)
    r = x - hi
    mid = r.astype(BF16).astype(F32)
    return hi, mid, r - mid


def _cumsum_rows(tri16, x):
    n = x.shape[1]
    p = _dot(tri16, jnp.concatenate([s.astype(BF16) for s in _split3(x)], axis=1))
    return (p[:, 0:n] + p[:, n:2 * n]) + p[:, 2 * n:3 * n]


def _pack3(x, lane):
    hi, mid, lo = _split3(x)
    h = SSD_HEADS
    return jnp.where(lane < h, hi,
                     jnp.where(lane < 2 * h, pltpu.roll(mid, h, 1),
                               jnp.where(lane < 3 * h, pltpu.roll(lo, 2 * h, 1), 0.0)))


def _scan_kernel(qf_ref, kf_ref, vf_ref, xf_ref, sf_ref, qb_ref, kb_ref, vb_ref, xb_ref, sb_ref,
                 wup_ref, gb_ref, e3_ref, dtb_ref, na_ref, sd_ref,
                 ybf_ref, ycf_ref, ybb_ref, ycb_ref, gla_s, ssd_s):
    i = pl.program_id(1)

    @pl.when(i == 0)
    def _():
        gla_s[...] = jnp.zeros_like(gla_s)
        ssd_s[...] = jnp.zeros_like(ssd_s)

    tb = qf_ref.shape[2]
    nch = tb // CHUNK
    row = lax.broadcasted_iota(jnp.int32, (CHUNK, CHUNK), 0)
    col = lax.broadcasted_iota(jnp.int32, (CHUNK, CHUNK), 1)
    row_x = lax.broadcasted_iota(jnp.int32, (CHUNK, SSD_INNER), 0)
    col_x = lax.broadcasted_iota(jnp.int32, (CHUNK, SSD_INNER), 1) & (CHUNK - 1)
    r2 = lax.broadcasted_iota(jnp.int32, (BD_W, BD_W), 0)
    c2 = lax.broadcasted_iota(jnp.int32, (BD_W, BD_W), 1)
    blk_mask = (r2 // CHUNK) == (c2 // SSD_P)
    lane = lax.broadcasted_iota(jnp.int32, (4 * CHUNK, SM_W), 1)
    prow = lax.broadcasted_iota(jnp.int32, (3 * CHUNK, SM_W), 0)
    plane = lax.broadcasted_iota(jnp.int32, (3 * CHUNK, SM_W), 1)
    ones_part = jnp.where((prow < CHUNK) & (plane < 3 * SSD_HEADS), 1.0, 0.0).astype(BF16)
    heads = range(GLA_HEADS)
    groups = range(SSD_G)

    def ksl(h):
        return slice(h * GLA_DKH, (h + 1) * GLA_DKH)

    def vsl(h):
        return slice(h * GLA_DVH, (h + 1) * GLA_DVH)

    def gsl(g):
        return slice(g * SSD_GW, (g + 1) * SSD_GW)

    dirs = (
        (0, qf_ref, kf_ref, vf_ref, xf_ref, sf_ref, ybf_ref, ycf_ref),
        (1, qb_ref, kb_ref, vb_ref, xb_ref, sb_ref, ybb_ref, ycb_ref),
    )
    def prepare(step):
        insts = []
        for dr, q_ref, k_ref, v_ref, x_ref, s_ref, yb_ref, yc_ref in dirs:
            fwd = dr == 0
            c = step if fwd else nch - 1 - step
            causal = (row >= col) if fwd else (row <= col)
            insts.append(dict(
                dr=dr, rs=slice(c * CHUNK, (c + 1) * CHUNK), last=CHUNK - 1 if fwd else 0,
                causal=causal, causal_x=(row_x >= col_x) if fwd else (row_x <= col_x),
                tri16=jnp.where(causal, 1.0, 0.0).astype(BF16),
                q_ref=q_ref, k_ref=k_ref, v_ref=v_ref, x_ref=x_ref, s_ref=s_ref, yb_ref=yb_ref, yc_ref=yc_ref))

        for it in insts:
            dr, rs, x_ref = it["dr"], it["rs"], it["x_ref"]
            it["sm"] = it["s_ref"][0, 0, rs, :]
            it["z"] = _dot(it["sm"].astype(BF16), wup_ref[dr])
            it["bm16"] = [x_ref[0, T_SSD_B, rs, g * SSD_N:(g + 1) * SSD_N].astype(BF16) for g in groups]
            it["cm16"] = [x_ref[0, T_SSD_C, rs, g * SSD_N:(g + 1) * SSD_N].astype(BF16) for g in groups]
            it["cb"] = [_dot_nt(it["cm16"][g], it["bm16"][g]) for g in groups]

        for it in insts:
            dr, rs, last = it["dr"], it["rs"], it["last"]
            logg = -_softplus(-(it["z"] + gb_ref[dr:dr + 1, :])) * (1.0 / GLA_TAU)
            bc = _cumsum_rows(it["tri16"], logg)
            btot = bc[last:last + 1, :]
            k = it["k_ref"][0, 0, rs, :]
            it["qe16"] = ((it["q_ref"][0, 0, rs, :] * (GLA_DKH ** -0.5)) * jnp.exp(bc)).astype(BF16)
            it["ke16"] = (k * jnp.exp(-bc)).astype(BF16)
            it["kd16"] = (k * jnp.exp(btot - bc)).astype(BF16)
            it["etot"] = jnp.exp(btot)
            it["v16"] = _wide(it["v_ref"], rs).astype(BF16)

            dt = _softplus(it["sm"] + dtb_ref[dr:dr + 1, :])
            cum = _cumsum_rows(it["tri16"], dt * na_ref[dr:dr + 1, :])
            clast = cum[last:last + 1, :]
            wrow = jnp.maximum(jnp.log(dt), LOG_FLOOR) - cum
            packed = _pack3(jnp.concatenate([cum, jnp.exp(cum), dt * jnp.exp(clast - cum), wrow], axis=0), lane)
            pt = packed[3 * CHUNK:4 * CHUNK].T
            pt2 = jnp.concatenate([pt, pt], axis=1).astype(BF16)
            rm = jnp.concatenate([pt2] * (SSD_INNER // LANES), axis=1) * e3_ref[...]
            it["a16"] = jnp.concatenate([packed[0:3 * CHUNK].astype(BF16), ones_part], axis=1)
            it["rhs"] = jnp.concatenate([e3_ref[...], rm], axis=0)
        return insts

    def advance(insts):
        for it in insts:
            it["big"] = _dot(it["a16"], it["rhs"])
            it["att"] = [_dot_nt(it["qe16"][:, ksl(h)], it["ke16"][:, ksl(h)]) for h in heads]

        for it in insts:
            big = it["big"]
            lmat = jnp.exp(jnp.where(it["causal_x"], big[0:CHUNK], -jnp.inf))
            m16 = []
            for g in groups:
                cb2 = jnp.concatenate([it["cb"][g], it["cb"][g]], axis=1)
                m16.append(jnp.concatenate(
                    [cb2 * lmat[:, g * SSD_GW + q * LANES:g * SSD_GW + (q + 1) * LANES] for q in range(SSD_GW // LANES)],
                    axis=1).astype(BF16))
            it["m16"] = m16
            it["ecum_x"] = big[CHUNK:2 * CHUNK]
            xs = _wide(it["x_ref"], it["rs"], SSD_INNER // TN)
            it["xs16"] = xs.astype(BF16)
            it["xdec16"] = (xs * big[2 * CHUNK:3 * CHUNK]).astype(BF16)
            it["att16"] = [jnp.where(it["causal"], a, 0.0).astype(BF16) for a in it["att"]]

        for it in insts:
            dr = it["dr"]
            it["o_in"] = [_dot(it["att16"][h], it["v16"][:, vsl(h)]) for h in heads]
            it["o_st"] = [_dot_nt(it["qe16"][:, ksl(h)], gla_s[dr, h].astype(BF16)) for h in heads]
            it["u_gla"] = [_dot_tn(it["v16"][:, vsl(h)], it["kd16"][:, ksl(h)]) for h in heads]
            y_in = []
            for g in groups:
                parts = []
                for pr in range(SSD_GW // BD_W):
                    ls = slice(g * SSD_GW + pr * BD_W, g * SSD_GW + (pr + 1) * BD_W)
                    xh = it["xs16"][:, ls]
                    bd = jnp.where(blk_mask, jnp.concatenate([xh] * BD_HEADS, axis=0), jnp.zeros((), BF16))
                    parts.append(_dot(it["m16"][g][:, pr * BD_W:(pr + 1) * BD_W], bd))
                y_in.append(jnp.concatenate(parts, axis=1))
            it["y_in"] = y_in
            it["y_st"] = [_dot(it["cm16"][g], ssd_s[dr, g].astype(BF16)) for g in groups]
            it["u_ssd"] = [_dot_tn(it["bm16"][g], it["xdec16"][:, gsl(g)]) for g in groups]

        for it in insts:
            dr, rs, last = it["dr"], it["rs"], it["last"]
            for h in heads:
                it["yb_ref"][0, rs, vsl(h)] = it["o_in"][h] + it["o_st"][h]
                gla_s[dr, h] = gla_s[dr, h] * it["etot"][:, ksl(h)] + it["u_gla"][h]
            for g in groups:
                ecum = it["ecum_x"][:, gsl(g)]
                yc = it["y_in"][g] + it["y_st"][g] * ecum
                if dr == 0:
                    yc = yc + sd_ref[:, gsl(g)] * it["x_ref"][0, g, rs, :]
                it["yc_ref"][0, rs, gsl(g)] = yc
                ssd_s[dr, g] = ssd_s[dr, g] * ecum[last:last + 1, :] + it["u_ssd"][g]

    ready = prepare(0)
    for step in range(nch):
        insts = ready
        if step + 1 < nch:
            ready = prepare(step + 1)
        advance(insts)


def _scan(u, wup, gb, e3, dtb, na, sd_x, *, lc):
    b, _, tt, _ = u.shape
    tb = SCAN_TB
    nt = tt // tb
    nctx = lc // tb

    def fblk(i):
        return i

    def bblk(i):
        return _bwd_block(i, nctx, nt)

    def fmap(cb):
        return lambda bi, i: (bi, i, cb)

    def bmap(cb):
        return lambda bi, i: (bi, _bwd_block(i, nctx, nt), cb)

    def tok_specs(blk, dr):
        return [
            _seg_spec(tb, C_Q, GLA_DK, blk), _seg_spec(tb, C_K, GLA_DK, blk), _seg_spec(tb, C_V, GLA_DV, blk),
            _seg_spec(tb, C_XBC, SSD_XBC, blk),
            pl.BlockSpec((1, 1, tb, SM_W), lambda bi, i: (bi, C_SM // TN, blk(i), dr)),
        ]

    def const_spec(a):
        nd = a.ndim
        return pl.BlockSpec(a.shape, lambda bi, i: (0,) * nd)

    assert SSD_GW == TN
    consts = (wup, gb, e3, dtb, na, sd_x)
    return pl.pallas_call(
        _scan_kernel,
        grid=(b, nt),
        in_specs=tok_specs(fblk, 0) + tok_specs(bblk, 1) + [const_spec(a) for a in consts],
        out_specs=[
            pl.BlockSpec((1, tb, GLA_DV), fmap(0)),
            pl.BlockSpec((1, tb, SSD_INNER), fmap(0)),
            pl.BlockSpec((1, tb, GLA_DV), bmap(0)),
            pl.BlockSpec((1, tb, SSD_INNER), bmap(0)),
        ],
        out_shape=[
            jax.ShapeDtypeStruct((b, tt, GLA_DV), F32),
            jax.ShapeDtypeStruct((b, tt, SSD_INNER), F32),
            jax.ShapeDtypeStruct((b, tt, GLA_DV), F32),
            jax.ShapeDtypeStruct((b, tt, SSD_INNER), F32),
        ],
        scratch_shapes=[
            pltpu.VMEM((2, GLA_HEADS, GLA_DVH, GLA_DKH), F32),
            pltpu.VMEM((2, SSD_G, SSD_N, SSD_GW), F32),
        ],
        compiler_params=pltpu.CompilerParams(
            dimension_semantics=("arbitrary", "arbitrary"), vmem_limit_bytes=VMEM_LIMIT),
        name="gla_ssd_scan",
    )(u, u, u, u, u, u, u, u, u, u, *consts)


FIN_TM = 256


def _group_rms(x, width):
    parts = []
    for s in range(x.shape[1] // width):
        xs = x[:, s * width:(s + 1) * width]
        parts.append(xs * lax.rsqrt(jnp.mean(xs * xs, axis=-1, keepdims=True) + EPS))
    return jnp.concatenate(parts, axis=1)


def _fin_kernel(yaf_ref, yab_ref, ybf_ref, ybb_ref, ycf_ref, ycb_ref, lg_ref, gg_ref, z_ref, mg_ref,
                xc_ref, xl_ref, ml_ref, mc_ref, gng_ref, sng_ref, pog_ref,
                wpa_ref, wpb_ref, wpc_ref, wo_ref, oc_ref, ol_ref, *, nctx):
    i = pl.program_id(1)
    d = D_MODEL
    rows = slice(None)
    ya = yaf_ref[0] + yab_ref[0]
    pa = _dot((ya * _silu(_wide(lg_ref, rows))).astype(BF16), wpa_ref[...])
    yb = ybf_ref[0] + ybb_ref[0]
    ob = (_group_rms(yb, GLA_DVH) * gng_ref[...]) * _silu(_wide(gg_ref, rows))
    pb = _dot(ob.astype(BF16), wpb_ref[...])
    yc = ycf_ref[0] + ycb_ref[0]
    oc = _group_rms(yc * _silu(_wide(z_ref, rows)), SSD_GW) * sng_ref[...]
    pc = _dot(oc.astype(BF16), wpc_ref[...])
    gates = _sigmoid(_wide(mg_ref, rows))
    merged = gates[:, 0:d] * pa + gates[:, d:2 * d] * pb + gates[:, 2 * d:3 * d] * pc
    out = _dot(merged.astype(BF16), wo_ref[...])
    normed = _group_rms(out, d) * pog_ref[...]

    @pl.when(i < nctx)
    def _():
        oc_ref[0] = xc_ref[0] + mc_ref[:, 2 * d:3 * d] * normed

    @pl.when(i >= nctx)
    def _():
        ol_ref[0] = xl_ref[0] + ml_ref[0][:, 2 * d:3 * d] * normed


def _finish(ya_f, ya_b, yb_f, yb_b, yc_f, yc_b, u, x_ctx, x_lat, mod_lat, mod_ctx,
            gng, sng, post_g, w_pa, w_pb, w_pc, w_out):
    b, lc, d = x_ctx.shape
    seq = x_lat.shape[1]
    tm = FIN_TM
    nctx = lc // tm
    ctx_spec = pl.BlockSpec((1, tm, d), lambda bi, i: (bi, jnp.minimum(i, nctx - 1), 0))
    lat_spec = pl.BlockSpec((1, tm, d), lambda bi, i: (bi, jnp.maximum(i - nctx, 0), 0))

    def tok(width):
        return pl.BlockSpec((1, tm, width), lambda bi, i: (bi, i, 0))

    def blk(i):
        return i

    def const_spec(a):
        nd = a.ndim
        return pl.BlockSpec(a.shape, lambda bi, i: (0,) * nd, pipeline_mode=pl.Buffered(1))

    consts = (gng, sng, post_g, w_pa, w_pb, w_pc, w_out)
    return pl.pallas_call(
        functools.partial(_fin_kernel, nctx=nctx),
        grid=(b, (lc + seq) // tm),
        in_specs=[
            tok(d), tok(d), tok(GLA_DV), tok(GLA_DV), tok(SSD_INNER), tok(SSD_INNER),
            _seg_spec(tm, C_LG, d, blk), _seg_spec(tm, C_GG, GLA_DV, blk), _seg_spec(tm, C_Z, SSD_INNER, blk),
            _seg_spec(tm, C_MRG, 3 * d, blk), ctx_spec, lat_spec,
            pl.BlockSpec((1, 1, 3 * d), lambda bi, i: (bi, 0, 0)),
            pl.BlockSpec((1, 3 * d), lambda bi, i: (0, 0)),
        ] + [const_spec(a) for a in consts],
        out_specs=[ctx_spec, lat_spec],
        out_shape=[jax.ShapeDtypeStruct((b, lc, d), F32), jax.ShapeDtypeStruct((b, seq, d), F32)],
        compiler_params=pltpu.CompilerParams(
            dimension_semantics=("arbitrary", "arbitrary"), vmem_limit_bytes=VMEM_LIMIT),
        name="finish",
    )(ya_f, ya_b, yb_f, yb_b, yc_f, yc_b, u, u, u, u, x_ctx, x_lat, mod_lat, mod_ctx, *consts)


def _reorder_in_weights(w_in):
    lead = w_in.shape[:-1]
    o = 0
    seg = {}
    for name, wd in (("xa", D_MODEL), ("lg", D_MODEL), ("q", GLA_DK), ("k", GLA_DK), ("v", GLA_DV),
                     ("gg", GLA_DV), ("low", 2 * GLA_RANK), ("z", SSD_INNER), ("xbc", SSD_XBC),
                     ("dt", 2 * SSD_HEADS), ("mrg", 3 * D_MODEL)):
        seg[name] = w_in[..., o:o + wd]
        o += wd
    pad = jnp.zeros(lead + (SM_W - GLA_RANK - SSD_HEADS,), w_in.dtype)
    cols = [seg["xbc"], seg["mrg"], seg["z"], seg["xa"], seg["v"], seg["lg"], seg["gg"],
            seg["q"], seg["k"]]
    for dr in range(2):
        cols += [seg["dt"][..., dr * SSD_HEADS:(dr + 1) * SSD_HEADS],
                 seg["low"][..., dr * GLA_RANK:(dr + 1) * GLA_RANK], pad]
    cols.append(jnp.zeros(lead + (N_TOT - C_SM - 2 * SM_W,), w_in.dtype))
    return jnp.concatenate(cols, axis=-1)


def _conv_tables(conv_a_w, conv_a_b, conv_c_w, conv_c_b):
    cw = jnp.zeros((CONV_W, N_TOT), F32)
    cw = cw.at[:, C_XBC:C_XBC + SSD_XBC].set(conv_c_w).at[:, C_XA:C_XA + D_MODEL].set(conv_a_w)
    cb = jnp.zeros((1, N_TOT), F32)
    cb = cb.at[0, C_XBC:C_XBC + SSD_XBC].set(conv_c_b).at[0, C_XA:C_XA + D_MODEL].set(conv_a_b)
    return cw, cb


def _scan_tables(gla_alpha_up, ssd_dt_bias, ssd_a_log):
    wup = jnp.zeros((2, SM_W, GLA_DK), F32).at[:, SM_LOW:SM_LOW + GLA_RANK, :].set(gla_alpha_up)
    dtb = jnp.zeros((2, SM_W), F32).at[:, SM_DT:SM_DT + SSD_HEADS].set(ssd_dt_bias)
    a_log = jnp.zeros((2, SM_W), F32).at[:, SM_DT:SM_DT + SSD_HEADS].set(ssd_a_log)
    valid = np.zeros((2, SM_W), np.float32)
    valid[:, SM_DT:SM_DT + SSD_HEADS] = 1.0
    e3 = np.zeros((SM_W, SSD_INNER), np.float32)
    for k in range(3 * SSD_HEADS):
        h = k % SSD_HEADS
        e3[k, h * SSD_P:(h + 1) * SSD_P] = 1.0
    return wup.astype(BF16), jnp.asarray(e3, BF16), dtb, a_log, jnp.asarray(valid)


def _to_col_major(h, rows):
    b, length, dm = h.shape
    return h.reshape(b, rows, GRID_W, dm).transpose(0, 2, 1, 3).reshape(b, length, dm)


def _from_col_major(h, rows):
    b, length, dm = h.shape
    return h.reshape(b, GRID_W, rows, dm).transpose(0, 2, 1, 3).reshape(b, length, dm)


def _neg_exp_kernel(a_ref, v_ref, o_ref):
    o_ref[...] = -jnp.exp(a_ref[...]) * v_ref[...]


def kernel(x, c, ctx, c_ctx, ada_w, ada_b, pre_g, post_g, w_in, conv_a_w, conv_a_b, lru_wr, lru_br,
           lru_wi, lru_bi, lru_lam, gla_alpha_up, gla_alpha_b, gla_norm_g, conv_c_w, conv_c_b,
           ssd_a_log, ssd_dt_bias, ssd_d, ssd_norm_g, w_pa, w_pb, w_pc, w_out):
    b, seq, d = x.shape
    lc = ctx.shape[1]
    depth = w_in.shape[0]
    rows = seq // GRID_W
    assert d == D_MODEL and lc % SCAN_TB == 0 and seq % SCAN_TB == 0 and b % 8 == 0

    pad_rows = (-(b + 1)) % 8
    cc = jnp.concatenate([c, c_ctx[None, :], jnp.zeros((pad_rows, d), F32)], axis=0)
    mods = _adaln(cc, ada_w, ada_b)

    w_in_p = _reorder_in_weights(w_in.astype(BF16))
    x_lat, x_ctx = x, ctx
    for l in range(depth):
        col_major = l % 2 == 1
        line = rows if col_major else GRID_W
        mod_lat = mods[l, :b].reshape(b, 1, 3 * d)
        mod_ctx = mods[l, b:b + 1]
        lat = _to_col_major(x_lat, rows) if col_major else x_lat

        cw, cb = _conv_tables(conv_a_w[l], conv_a_b[l], conv_c_w[l], conv_c_b[l])
        u = _in_proj(x_ctx, lat, mod_lat, mod_ctx, pre_g[l].reshape(1, d), w_in_p[l],
                     cw, cb, line=line)

        ya_f, ya_b = _lru(u, lru_wr[l].astype(BF16), lru_wi[l].astype(BF16), lru_br[l], lru_bi[l],
                          lru_lam[l], lc=lc)

        wup, e3, dtb, a_log, valid = _scan_tables(gla_alpha_up[l], ssd_dt_bias[l], ssd_a_log[l])
        na = pl.pallas_call(_neg_exp_kernel, out_shape=jax.ShapeDtypeStruct(a_log.shape, F32),
                            name="ssd_neg_a")(a_log, valid)
        sd_x = jnp.repeat(ssd_d[l], SSD_P).reshape(1, SSD_INNER)
        yb_f, yc_f, yb_b, yc_b = _scan(u, wup, gla_alpha_b[l], e3, dtb, na, sd_x, lc=lc)

        x_ctx, lat = _finish(ya_f, ya_b, yb_f, yb_b, yc_f, yc_b, u, x_ctx, lat, mod_lat, mod_ctx,
                             jnp.tile(gla_norm_g[l], GLA_HEADS).reshape(1, GLA_DV),
                             ssd_norm_g[l].reshape(1, SSD_INNER),
                             post_g[l].reshape(1, d),
                             w_pa[l].astype(BF16), w_pb[l].astype(BF16), w_pc[l].astype(BF16),
                             w_out[l].astype(BF16))
        x_lat = _from_col_major(lat, rows) if col_major else lat
    return x_lat
```

```python
import functools

import jax
import jax.numpy as jnp
import numpy as np
from jax import lax
from jax.experimental import pallas as pl
from jax.experimental.pallas import tpu as pltpu

F32 = jnp.float32
BF16 = jnp.bfloat16

LANES = 128
SUBLANES = 8

D_MODEL = 1024
GRID_W = 64
CONV_W = 4
EPS = 1e-6
LRU_BW = 128
LRU_C = 8.0
LOG2E = 1.4426950408889634
GLA_HEADS = 4
GLA_DK = 512
GLA_DV = 1024
GLA_DKH = 128
GLA_DVH = 256
GLA_RANK = 16
GLA_TAU = 16.0
SSD_INNER = 2048
SSD_P = 64
SSD_HEADS = 32
SSD_N = 128
SSD_G = 4
SSD_GW = SSD_INNER // SSD_G
SSD_XBC = SSD_INNER + 2 * SSD_G * SSD_N
CHUNK = 64

TN = 512
C_XBC, C_MRG, C_Z, C_XA, C_V, C_LG, C_GG, C_Q, C_K, C_SM = (
    0, 3072, 6144, 8192, 9216, 10240, 11264, 12288, 12800, 13312)
N_TOT = C_SM + TN

VMEM_LIMIT = 56 * 1024 * 1024


def _sigmoid(x):
    return 0.5 * jnp.tanh(0.5 * x) + 0.5


def _silu(x):
    return x * _sigmoid(x)


def _softplus(x):
    return jnp.maximum(x, 0.0) + jnp.log1p(jnp.exp(-jnp.abs(x)))


def _dot(a, b):
    return jnp.dot(a, b, preferred_element_type=F32)


def _dot_nt(a, b):
    return lax.dot_general(a, b, (((1,), (1,)), ((), ())), preferred_element_type=F32)


def _dot_tn(a, b):
    return lax.dot_general(a, b, (((0,), (0,)), ((), ())), preferred_element_type=F32)


def _adaln_kernel(c_ref, w_ref, b_ref, o_ref):
    c = c_ref[...]
    o_ref[0] = _dot(_silu(c).astype(BF16), w_ref[0]) + b_ref[0]


def _adaln(cc, ada_w, ada_b):
    depth, d, n3 = ada_w.shape
    rows = cc.shape[0]
    tn = 1024
    return pl.pallas_call(
        _adaln_kernel,
        grid=(depth, n3 // tn),
        in_specs=[
            pl.BlockSpec((rows, d), lambda l, n: (0, 0)),
            pl.BlockSpec((1, d, tn), lambda l, n: (l, 0, n)),
            pl.BlockSpec((1, 1, tn), lambda l, n: (l, 0, n)),
        ],
        out_specs=pl.BlockSpec((1, rows, tn), lambda l, n: (l, 0, n)),
        out_shape=jax.ShapeDtypeStruct((depth, rows, n3), F32),
        name="adaln",
    )(cc, ada_w.astype(BF16), ada_b.reshape(depth, 1, n3))


CONV_PAD = SUBLANES
CONV_SLABS = TN // LANES
CONV_PARTS = 4

def _in_kernel(xc_ref, xl_ref, ml_ref, mc_ref, pg_ref, w_ref, cw_ref, cb_ref, u_ref, h_s, conv_s, *, line):
    j = pl.program_id(1)
    lc = xc_ref.shape[1]
    tt = lc + xl_ref.shape[1]
    d = xc_ref.shape[2]
    nlines = (tt - lc) // line
    pitch = line + CONV_PAD
    lat0 = CONV_PAD + lc

    @pl.when(j == 0)
    def _():
        def norm_mod(x, mod):
            y = x * lax.rsqrt(jnp.mean(x * x, axis=-1, keepdims=True) + EPS) * pg_ref[...]
            return (y * (1.0 + mod[:, d:2 * d]) + mod[:, 0:d]).astype(BF16)
        h_s[0:lc, :] = norm_mod(xc_ref[0], mc_ref[...])
        h_s[lc:tt, :] = norm_mod(xl_ref[0], ml_ref[0])
        zeros = jnp.zeros((CONV_SLABS, CONV_PAD, LANES), F32)
        conv_s[:, 0:CONV_PAD, :] = zeros
        for i in range(nlines + 1):
            r = lat0 + i * pitch
            conv_s[:, r:r + CONV_PAD, :] = zeros

    def conv(act):
        def taps(n, r, rows):
            ls = slice(n * LANES, (n + 1) * LANES)
            cw = cw_ref[:, ls]
            out = cb_ref[:, ls] + cw[1:2, :] * conv_s[n, r:r + rows, :]
            out = out + cw[0:1, :] * conv_s[n, r - 1:r - 1 + rows, :]
            out = out + cw[2:3, :] * conv_s[n, r + 1:r + 1 + rows, :]
            out = out + cw[3:4, :] * conv_s[n, r + 2:r + 2 + rows, :]
            return act(out)

        lines = [(0, CONV_PAD, lc)] + [(lc + i * line, lat0 + i * pitch + CONV_PAD, line) for i in range(nlines)]
        bounds = [0] + [lc + ((p * nlines) // CONV_PARTS) * line for p in range(1, CONV_PARTS)] + [tt]
        for p in range(CONV_PARTS):
            t0, t1 = bounds[p], bounds[p + 1]
            up = _dot(h_s[t0:t1, :], w_ref[...])
            part = [ln for ln in lines if t0 <= ln[0] < t1]
            for n in range(CONV_SLABS):
                for t, r, rows in part:
                    conv_s[n, r:r + rows, :] = up[t - t0:t - t0 + rows, n * LANES:(n + 1) * LANES]
            for t, r, rows in part:
                for n in range(CONV_SLABS):
                    u_ref[0, 0, t:t + rows, n * LANES:(n + 1) * LANES] = taps(n, r, rows)

    is_xbc = j < SSD_XBC // TN
    is_xa = (j >= C_XA // TN) & (j < (C_XA + D_MODEL) // TN)

    @pl.when(is_xbc)
    def _():
        conv(_silu)

    @pl.when(is_xa)
    def _():
        conv(lambda v: v)

    @pl.when(jnp.logical_not(is_xbc | is_xa))
    def _():
        u_ref[0, 0] = _dot(h_s[...], w_ref[...])


def _in_proj(x_ctx, x_lat, mod_lat, mod_ctx, pre_g, w_in_p, cw, cb, *, line):
    b, lc, d = x_ctx.shape
    seq = x_lat.shape[1]
    tt = lc + seq
    assert line % SUBLANES == 0 and seq % line == 0 and lc % SUBLANES == 0
    conv_rows = CONV_PAD + lc + (seq // line) * (line + CONV_PAD) + CONV_PAD
    return pl.pallas_call(
        functools.partial(_in_kernel, line=line),
        grid=(b, N_TOT // TN),
        in_specs=[
            pl.BlockSpec((1, lc, d), lambda i, j: (i, 0, 0)),
            pl.BlockSpec((1, seq, d), lambda i, j: (i, 0, 0)),
            pl.BlockSpec((1, 1, 3 * d), lambda i, j: (i, 0, 0)),
            pl.BlockSpec((1, 3 * d), lambda i, j: (0, 0)),
            pl.BlockSpec((1, d), lambda i, j: (0, 0)),
            pl.BlockSpec((d, TN), lambda i, j: (0, j)),
            pl.BlockSpec((CONV_W, TN), lambda i, j: (0, j)),
            pl.BlockSpec((1, TN), lambda i, j: (0, j)),
        ],
        out_specs=pl.BlockSpec((1, 1, tt, TN), lambda i, j: (i, j, 0, 0)),
        out_shape=jax.ShapeDtypeStruct((b, N_TOT // TN, tt, TN), F32),
        scratch_shapes=[pltpu.VMEM((tt, d), BF16), pltpu.VMEM((CONV_SLABS, conv_rows, LANES), F32)],
        compiler_params=pltpu.CompilerParams(
            dimension_semantics=("arbitrary", "arbitrary"), vmem_limit_bytes=VMEM_LIMIT),
        name="in_proj",
    )(x_ctx, x_lat, mod_lat, mod_ctx, pre_g, w_in_p, cw, cb)


LRU_TC = 64
LRU_WT = 512
LRU_PITCH = LRU_TC + 4


def _lru_kernel(xf_ref, xb_ref, wr_ref, wi_ref, br_ref, bi_ref, lam_ref, yf_ref, yb_ref,
                a_s, b_s, o_s, h_s):
    i = pl.program_id(1)
    nb, _, tc, wt = xf_ref.shape

    @pl.when(i == 0)
    def _():
        h_s[...] = jnp.zeros_like(h_s)

    def gates(bi, carry):
        r0 = bi * LRU_PITCH
        for dr, x_ref in enumerate((xf_ref, xb_ref)):
            for n in range(wt // LRU_BW):
                sl = slice(n * LRU_BW, (n + 1) * LRU_BW)
                xs = x_ref[bi, 0, :, sl]
                xs16 = xs.astype(BF16)
                hc = (0.5 * LRU_C) * _softplus(-lam_ref[dr:dr + 1, sl])
                nla = hc * jnp.tanh(0.5 * (_dot(xs16, wr_ref[dr, n]) + br_ref[dr:dr + 1, sl])) + hc
                g = _sigmoid(_dot(xs16, wi_ref[dr, n]) + bi_ref[dr:dr + 1, sl])
                a = jnp.exp2(nla * (-LOG2E))
                m2 = jnp.tanh(nla) * (a * a + 1.0)
                root = jnp.where(m2 > 0.0, m2 * lax.rsqrt(m2), 0.0)
                a_s[dr, n, pl.ds(r0, tc), :] = a
                b_s[dr, n, pl.ds(r0, tc), :] = root * (g * xs)
        return carry
    lax.fori_loop(0, nb, gates, 0, unroll=8)

    nl = wt // LRU_BW
    hf = [h_s[0, n] for n in range(nl)]
    hb = [h_s[1, n] for n in range(nl)]
    for t in range(tc):
        rows_f = pl.ds(t, nb, stride=LRU_PITCH)
        rows_b = pl.ds(tc - 1 - t, nb, stride=LRU_PITCH)
        for n in range(nl):
            hf[n] = a_s[0, n, rows_f, :] * hf[n] + b_s[0, n, rows_f, :]
            hb[n] = a_s[1, n, rows_b, :] * hb[n] + b_s[1, n, rows_b, :]
            o_s[0, n, rows_f, :] = hf[n]
            o_s[1, n, rows_b, :] = hb[n]
    for n in range(nl):
        h_s[0, n] = hf[n]
        h_s[1, n] = hb[n]

    def emit(bi, carry):
        r0 = bi * LRU_PITCH
        for n in range(nl):
            sl = slice(n * LRU_BW, (n + 1) * LRU_BW)
            yf_ref[bi, :, sl] = o_s[0, n, pl.ds(r0, tc), :]
            yb_ref[bi, :, sl] = o_s[1, n, pl.ds(r0, tc), :]
        return carry
    lax.fori_loop(0, nb, emit, 0)


def _bwd_block(i, nctx, ntot):
    return jnp.where(i < nctx, nctx - 1 - i, ntot - 1 + nctx - i)


def _lru(u, wr, wi, br, bi, lam, *, lc):
    b, _, tt, _ = u.shape
    w = D_MODEL
    nt = tt // LRU_TC
    nctx = lc // LRU_TC
    assert LRU_WT == TN
    c0 = C_XA // TN
    nblk = LRU_WT // LRU_BW
    y_spec_f = pl.BlockSpec((b, LRU_TC, LRU_WT), lambda j, i: (0, i, j))
    y_spec_b = pl.BlockSpec((b, LRU_TC, LRU_WT), lambda j, i: (0, _bwd_block(i, nctx, nt), j))
    return pl.pallas_call(
        _lru_kernel,
        grid=(w // LRU_WT, nt),
        in_specs=[
            pl.BlockSpec((b, 1, LRU_TC, LRU_WT), lambda j, i: (0, c0 + j, i, 0)),
            pl.BlockSpec((b, 1, LRU_TC, LRU_WT), lambda j, i: (0, c0 + j, _bwd_block(i, nctx, nt), 0)),
            pl.BlockSpec((2, nblk, LRU_BW, LRU_BW), lambda j, i: (0, j, 0, 0)),
            pl.BlockSpec((2, nblk, LRU_BW, LRU_BW), lambda j, i: (0, j, 0, 0)),
            pl.BlockSpec((2, LRU_WT), lambda j, i: (0, j)),
            pl.BlockSpec((2, LRU_WT), lambda j, i: (0, j)),
            pl.BlockSpec((2, LRU_WT), lambda j, i: (0, j)),
        ],
        out_specs=[y_spec_f, y_spec_b],
        out_shape=[jax.ShapeDtypeStruct((b, tt, w), F32)] * 2,
        scratch_shapes=[
            pltpu.VMEM((2, nblk, b * LRU_PITCH, LRU_BW), F32),
            pltpu.VMEM((2, nblk, b * LRU_PITCH, LRU_BW), F32),
            pltpu.VMEM((2, nblk, b * LRU_PITCH, LRU_BW), F32),
            pltpu.VMEM((2, nblk, b, LRU_BW), F32),
        ],
        compiler_params=pltpu.CompilerParams(
            dimension_semantics=("arbitrary", "arbitrary"), vmem_limit_bytes=VMEM_LIMIT),
        name="lru_scan",
    )(u, u, wr, wi, br, bi, lam)


SCAN_TB = 256
SM_W = LANES
SM_DT = 0
SM_LOW = SSD_HEADS
BD_HEADS = 4
BD_W = BD_HEADS * SSD_P
LOG_FLOOR = -1e30


T_SSD_B = SSD_INNER // TN
T_SSD_C = T_SSD_B + 1


def _seg_spec(rows, c0, width, blk):
    nt = width // TN
    assert width % TN == 0 and (c0 // TN) % nt == 0
    return pl.BlockSpec((1, nt, rows, TN), lambda bi, i: (bi, c0 // width, blk(i), 0))


def _wide(ref, rs, ntiles=None):
    nt = ref.shape[1] if ntiles is None else ntiles
    return jnp.concatenate([ref[0, t, rs, :] for t in range(nt)], axis=1)


def _split3(x):
    hi = x.astype(BF16).astype(F32)
    r = x - hi
    mid = r.astype(BF16).astype(F32)
    return hi, mid, r - mid


def _cumsum_rows(tri16, x):
    n = x.shape[1]
    p = _dot(tri16, jnp.concatenate([s.astype(BF16) for s in _split3(x)], axis=1))
    return (p[:, 0:n] + p[:, n:2 * n]) + p[:, 2 * n:3 * n]


def _pack3(x, lane):
    hi, mid, lo = _split3(x)
    h = SSD_HEADS
    return jnp.where(lane < h, hi,
                     jnp.where(lane < 2 * h, pltpu.roll(mid, h, 1),
                               jnp.where(lane < 3 * h, pltpu.roll(lo, 2 * h, 1), 0.0)))


def _scan_kernel(qf_ref, kf_ref, vf_ref, xf_ref, sf_ref, qb_ref, kb_ref, vb_ref, xb_ref, sb_ref,
                 wup_ref, gb_ref, e3_ref, dtb_ref, na_ref, sd_ref,
                 ybf_ref, ycf_ref, ybb_ref, ycb_ref, gla_s, ssd_s):
    i = pl.program_id(1)

    @pl.when(i == 0)
    def _():
        gla_s[...] = jnp.zeros_like(gla_s)
        ssd_s[...] = jnp.zeros_like(ssd_s)

    tb = qf_ref.shape[2]
    nch = tb // CHUNK
    row = lax.broadcasted_iota(jnp.int32, (CHUNK, CHUNK), 0)
    col = lax.broadcasted_iota(jnp.int32, (CHUNK, CHUNK), 1)
    row_x = lax.broadcasted_iota(jnp.int32, (CHUNK, SSD_INNER), 0)
    col_x = lax.broadcasted_iota(jnp.int32, (CHUNK, SSD_INNER), 1) & (CHUNK - 1)
    r2 = lax.broadcasted_iota(jnp.int32, (BD_W, BD_W), 0)
    c2 = lax.broadcasted_iota(jnp.int32, (BD_W, BD_W), 1)
    blk_mask = (r2 // CHUNK) == (c2 // SSD_P)
    lane = lax.broadcasted_iota(jnp.int32, (4 * CHUNK, SM_W), 1)
    prow = lax.broadcasted_iota(jnp.int32, (3 * CHUNK, SM_W), 0)
    plane = lax.broadcasted_iota(jnp.int32, (3 * CHUNK, SM_W), 1)
    ones_part = jnp.where((prow < CHUNK) & (plane < 3 * SSD_HEADS), 1.0, 0.0).astype(BF16)
    heads = range(GLA_HEADS)
    groups = range(SSD_G)

    def ksl(h):
        return slice(h * GLA_DKH, (h + 1) * GLA_DKH)

    def vsl(h):
        return slice(h * GLA_DVH, (h + 1) * GLA_DVH)

    def gsl(g):
        return slice(g * SSD_GW, (g + 1) * SSD_GW)

    dirs = (
        (0, qf_ref, kf_ref, vf_ref, xf_ref, sf_ref, ybf_ref, ycf_ref),
        (1, qb_ref, kb_ref, vb_ref, xb_ref, sb_ref, ybb_ref, ycb_ref),
    )
    def prepare(step):
        insts = []
        for dr, q_ref, k_ref, v_ref, x_ref, s_ref, yb_ref, yc_ref in dirs:
            fwd = dr == 0
            c = step if fwd else nch - 1 - step
            causal = (row >= col) if fwd else (row <= col)
            insts.append(dict(
                dr=dr, rs=slice(c * CHUNK, (c + 1) * CHUNK), last=CHUNK - 1 if fwd else 0,
                causal=causal, causal_x=(row_x >= col_x) if fwd else (row_x <= col_x),
                tri16=jnp.where(causal, 1.0, 0.0).astype(BF16),
                q_ref=q_ref, k_ref=k_ref, v_ref=v_ref, x_ref=x_ref, s_ref=s_ref, yb_ref=yb_ref, yc_ref=yc_ref))

        for it in insts:
            dr, rs, x_ref = it["dr"], it["rs"], it["x_ref"]
            it["sm"] = it["s_ref"][0, 0, rs, :]
            it["z"] = _dot(it["sm"].astype(BF16), wup_ref[dr])
            it["bm16"] = [x_ref[0, T_SSD_B, rs, g * SSD_N:(g + 1) * SSD_N].astype(BF16) for g in groups]
            it["cm16"] = [x_ref[0, T_SSD_C, rs, g * SSD_N:(g + 1) * SSD_N].astype(BF16) for g in groups]
            it["cb"] = [_dot_nt(it["cm16"][g], it["bm16"][g]) for g in groups]

        for it in insts:
            dr, rs, last = it["dr"], it["rs"], it["last"]
            logg = -_softplus(-(it["z"] + gb_ref[dr:dr + 1, :])) * (1.0 / GLA_TAU)
            bc = _cumsum_rows(it["tri16"], logg)
            btot = bc[last:last + 1, :]
            k = it["k_ref"][0, 0, rs, :]
            it["qe16"] = ((it["q_ref"][0, 0, rs, :] * (GLA_DKH ** -0.5)) * jnp.exp(bc)).astype(BF16)
            it["ke16"] = (k * jnp.exp(-bc)).astype(BF16)
            it["kd16"] = (k * jnp.exp(btot - bc)).astype(BF16)
            it["etot"] = jnp.exp(btot)
            it["v16"] = _wide(it["v_ref"], rs).astype(BF16)

            dt = _softplus(it["sm"] + dtb_ref[dr:dr + 1, :])
            cum = _cumsum_rows(it["tri16"], dt * na_ref[dr:dr + 1, :])
            clast = cum[last:last + 1, :]
            wrow = jnp.maximum(jnp.log(dt), LOG_FLOOR) - cum
            packed = _pack3(jnp.concatenate([cum, jnp.exp(cum), dt * jnp.exp(clast - cum), wrow], axis=0), lane)
            pt = packed[3 * CHUNK:4 * CHUNK].T
            pt2 = jnp.concatenate([pt, pt], axis=1).astype(BF16)
            rm = jnp.concatenate([pt2] * (SSD_INNER // LANES), axis=1) * e3_ref[...]
            it["a16"] = jnp.concatenate([packed[0:3 * CHUNK].astype(BF16), ones_part], axis=1)
            it["rhs"] = jnp.concatenate([e3_ref[...], rm], axis=0)
        return insts

    def advance(insts):
        for it in insts:
            it["big"] = _dot(it["a16"], it["rhs"])
            it["att"] = [_dot_nt(it["qe16"][:, ksl(h)], it["ke16"][:, ksl(h)]) for h in heads]

        for it in insts:
            big = it["big"]
            lmat = jnp.exp(jnp.where(it["causal_x"], big[0:CHUNK], -jnp.inf))
            m16 = []
            for g in groups:
                cb2 = jnp.concatenate([it["cb"][g], it["cb"][g]], axis=1)
                m16.append(jnp.concatenate(
                    [cb2 * lmat[:, g * SSD_GW + q * LANES:g * SSD_GW + (q + 1) * LANES] for q in range(SSD_GW // LANES)],
                    axis=1).astype(BF16))
            it["m16"] = m16
            it["ecum_x"] = big[CHUNK:2 * CHUNK]
            xs = _wide(it["x_ref"], it["rs"], SSD_INNER // TN)
            it["xs16"] = xs.astype(BF16)
            it["xdec16"] = (xs * big[2 * CHUNK:3 * CHUNK]).astype(BF16)
            it["att16"] = [jnp.where(it["causal"], a, 0.0).astype(BF16) for a in it["att"]]

        for it in insts:
            dr = it["dr"]
            it["o_in"] = [_dot(it["att16"][h], it["v16"][:, vsl(h)]) for h in heads]
            it["o_st"] = [_dot_nt(it["qe16"][:, ksl(h)], gla_s[dr, h].astype(BF16)) for h in heads]
            it["u_gla"] = [_dot_tn(it["v16"][:, vsl(h)], it["kd16"][:, ksl(h)]) for h in heads]
            y_in = []
            for g in groups:
                parts = []
                for pr in range(SSD_GW // BD_W):
                    ls = slice(g * SSD_GW + pr * BD_W, g * SSD_GW + (pr + 1) * BD_W)
                    xh = it["xs16"][:, ls]
                    bd = jnp.where(blk_mask, jnp.concatenate([xh] * BD_HEADS, axis=0), jnp.zeros((), BF16))
                    parts.append(_dot(it["m16"][g][:, pr * BD_W:(pr + 1) * BD_W], bd))
                y_in.append(jnp.concatenate(parts, axis=1))
            it["y_in"] = y_in
            it["y_st"] = [_dot(it["cm16"][g], ssd_s[dr, g].astype(BF16)) for g in groups]
            it["u_ssd"] = [_dot_tn(it["bm16"][g], it["xdec16"][:, gsl(g)]) for g in groups]

        for it in insts:
            dr, rs, last = it["dr"], it["rs"], it["last"]
            for h in heads:
                it["yb_ref"][0, rs, vsl(h)] = it["o_in"][h] + it["o_st"][h]
                gla_s[dr, h] = gla_s[dr, h] * it["etot"][:, ksl(h)] + it["u_gla"][h]
            for g in groups:
                ecum = it["ecum_x"][:, gsl(g)]
                yc = it["y_in"][g] + it["y_st"][g] * ecum
                if dr == 0:
                    yc = yc + sd_ref[:, gsl(g)] * it["x_ref"][0, g, rs, :]
                it["yc_ref"][0, rs, gsl(g)] = yc
                ssd_s[dr, g] = ssd_s[dr, g] * ecum[last:last + 1, :] + it["u_ssd"][g]

    ready = prepare(0)
    for step in range(nch):
        insts = ready
        if step + 1 < nch:
            ready = prepare(step + 1)
        advance(insts)


def _scan(u, wup, gb, e3, dtb, na, sd_x, *, lc):
    b, _, tt, _ = u.shape
    tb = SCAN_TB
    nt = tt // tb
    nctx = lc // tb

    def fblk(i):
        return i

    def bblk(i):
        return _bwd_block(i, nctx, nt)

    def fmap(cb):
        return lambda bi, i: (bi, i, cb)

    def bmap(cb):
        return lambda bi, i: (bi, _bwd_block(i, nctx, nt), cb)

    def tok_specs(blk, dr):
        return [
            _seg_spec(tb, C_Q, GLA_DK, blk), _seg_spec(tb, C_K, GLA_DK, blk), _seg_spec(tb, C_V, GLA_DV, blk),
            _seg_spec(tb, C_XBC, SSD_XBC, blk),
            pl.BlockSpec((1, 1, tb, SM_W), lambda bi, i: (bi, C_SM // TN, blk(i), dr)),
        ]

    def const_spec(a):
        nd = a.ndim
        return pl.BlockSpec(a.shape, lambda bi, i: (0,) * nd)

    assert SSD_GW == TN
    consts = (wup, gb, e3, dtb, na, sd_x)
    return pl.pallas_call(
        _scan_kernel,
        grid=(b, nt),
        in_specs=tok_specs(fblk, 0) + tok_specs(bblk, 1) + [const_spec(a) for a in consts],
        out_specs=[
            pl.BlockSpec((1, tb, GLA_DV), fmap(0)),
            pl.BlockSpec((1, tb, SSD_INNER), fmap(0)),
            pl.BlockSpec((1, tb, GLA_DV), bmap(0)),
            pl.BlockSpec((1, tb, SSD_INNER), bmap(0)),
        ],
        out_shape=[
            jax.ShapeDtypeStruct((b, tt, GLA_DV), F32),
            jax.ShapeDtypeStruct((b, tt, SSD_INNER), F32),
            jax.ShapeDtypeStruct((b, tt, GLA_DV), F32),
            jax.ShapeDtypeStruct((b, tt, SSD_INNER), F32),
        ],
        scratch_shapes=[
            pltpu.VMEM((2, GLA_HEADS, GLA_DVH, GLA_DKH), F32),
            pltpu.VMEM((2, SSD_G, SSD_N, SSD_GW), F32),
        ],
        compiler_params=pltpu.CompilerParams(
            dimension_semantics=("arbitrary", "arbitrary"), vmem_limit_bytes=VMEM_LIMIT),
        name="gla_ssd_scan",
    )(u, u, u, u, u, u, u, u, u, u, *consts)


FIN_TM = 256


def _group_rms(x, width):
    parts = []
    for s in range(x.shape[1] // width):
        xs = x[:, s * width:(s + 1) * width]
        parts.append(xs * lax.rsqrt(jnp.mean(xs * xs, axis=-1, keepdims=True) + EPS))
    return jnp.concatenate(parts, axis=1)


def _fin_kernel(yaf_ref, yab_ref, ybf_ref, ybb_ref, ycf_ref, ycb_ref, lg_ref, gg_ref, z_ref, mg_ref,
                xc_ref, xl_ref, ml_ref, mc_ref, gng_ref, sng_ref, pog_ref,
                wpa_ref, wpb_ref, wpc_ref, wo_ref, oc_ref, ol_ref, *, nctx):
    i = pl.program_id(1)
    d = D_MODEL
    rows = slice(None)
    ya = yaf_ref[0] + yab_ref[0]
    pa = _dot((ya * _silu(_wide(lg_ref, rows))).astype(BF16), wpa_ref[...])
    yb = ybf_ref[0] + ybb_ref[0]
    ob = (_group_rms(yb, GLA_DVH) * gng_ref[...]) * _silu(_wide(gg_ref, rows))
    pb = _dot(ob.astype(BF16), wpb_ref[...])
    yc = ycf_ref[0] + ycb_ref[0]
    oc = _group_rms(yc * _silu(_wide(z_ref, rows)), SSD_GW) * sng_ref[...]
    pc = _dot(oc.astype(BF16), wpc_ref[...])
    gates = _sigmoid(_wide(mg_ref, rows))
    merged = gates[:, 0:d] * pa + gates[:, d:2 * d] * pb + gates[:, 2 * d:3 * d] * pc
    out = _dot(merged.astype(BF16), wo_ref[...])
    normed = _group_rms(out, d) * pog_ref[...]

    @pl.when(i < nctx)
    def _():
        oc_ref[0] = xc_ref[0] + mc_ref[:, 2 * d:3 * d] * normed

    @pl.when(i >= nctx)
    def _():
        ol_ref[0] = xl_ref[0] + ml_ref[0][:, 2 * d:3 * d] * normed


def _finish(ya_f, ya_b, yb_f, yb_b, yc_f, yc_b, u, x_ctx, x_lat, mod_lat, mod_ctx,
            gng, sng, post_g, w_pa, w_pb, w_pc, w_out):
    b, lc, d = x_ctx.shape
    seq = x_lat.shape[1]
    tm = FIN_TM
    nctx = lc // tm
    ctx_spec = pl.BlockSpec((1, tm, d), lambda bi, i: (bi, jnp.minimum(i, nctx - 1), 0))
    lat_spec = pl.BlockSpec((1, tm, d), lambda bi, i: (bi, jnp.maximum(i - nctx, 0), 0))

    def tok(width):
        return pl.BlockSpec((1, tm, width), lambda bi, i: (bi, i, 0))

    def blk(i):
        return i

    def const_spec(a):
        nd = a.ndim
        return pl.BlockSpec(a.shape, lambda bi, i: (0,) * nd, pipeline_mode=pl.Buffered(1))

    consts = (gng, sng, post_g, w_pa, w_pb, w_pc, w_out)
    return pl.pallas_call(
        functools.partial(_fin_kernel, nctx=nctx),
        grid=(b, (lc + seq) // tm),
        in_specs=[
            tok(d), tok(d), tok(GLA_DV), tok(GLA_DV), tok(SSD_INNER), tok(SSD_INNER),
            _seg_spec(tm, C_LG, d, blk), _seg_spec(tm, C_GG, GLA_DV, blk), _seg_spec(tm, C_Z, SSD_INNER, blk),
            _seg_spec(tm, C_MRG, 3 * d, blk), ctx_spec, lat_spec,
            pl.BlockSpec((1, 1, 3 * d), lambda bi, i: (bi, 0, 0)),
            pl.BlockSpec((1, 3 * d), lambda bi, i: (0, 0)),
        ] + [const_spec(a) for a in consts],
        out_specs=[ctx_spec, lat_spec],
        out_shape=[jax.ShapeDtypeStruct((b, lc, d), F32), jax.ShapeDtypeStruct((b, seq, d), F32)],
        compiler_params=pltpu.CompilerParams(
            dimension_semantics=("arbitrary", "arbitrary"), vmem_limit_bytes=VMEM_LIMIT),
        name="finish",
    )(ya_f, ya_b, yb_f, yb_b, yc_f, yc_b, u, u, u, u, x_ctx, x_lat, mod_lat, mod_ctx, *consts)


def _reorder_in_weights(w_in):
    lead = w_in.shape[:-1]
    o = 0
    seg = {}
    for name, wd in (("xa", D_MODEL), ("lg", D_MODEL), ("q", GLA_DK), ("k", GLA_DK), ("v", GLA_DV),
                     ("gg", GLA_DV), ("low", 2 * GLA_RANK), ("z", SSD_INNER), ("xbc", SSD_XBC),
                     ("dt", 2 * SSD_HEADS), ("mrg", 3 * D_MODEL)):
        seg[name] = w_in[..., o:o + wd]
        o += wd
    pad = jnp.zeros(lead + (SM_W - GLA_RANK - SSD_HEADS,), w_in.dtype)
    cols = [seg["xbc"], seg["mrg"], seg["z"], seg["xa"], seg["v"], seg["lg"], seg["gg"],
            seg["q"], seg["k"]]
    for dr in range(2):
        cols += [seg["dt"][..., dr * SSD_HEADS:(dr + 1) * SSD_HEADS],
                 seg["low"][..., dr * GLA_RANK:(dr + 1) * GLA_RANK], pad]
    cols.append(jnp.zeros(lead + (N_TOT - C_SM - 2 * SM_W,), w_in.dtype))
    return jnp.concatenate(cols, axis=-1)


def _conv_tables(conv_a_w, conv_a_b, conv_c_w, conv_c_b):
    cw = jnp.zeros((CONV_W, N_TOT), F32)
    cw = cw.at[:, C_XBC:C_XBC + SSD_XBC].set(conv_c_w).at[:, C_XA:C_XA + D_MODEL].set(conv_a_w)
    cb = jnp.zeros((1, N_TOT), F32)
    cb = cb.at[0, C_XBC:C_XBC + SSD_XBC].set(conv_c_b).at[0, C_XA:C_XA + D_MODEL].set(conv_a_b)
    return cw, cb


def _scan_tables(gla_alpha_up, ssd_dt_bias, ssd_a_log):
    wup = jnp.zeros((2, SM_W, GLA_DK), F32).at[:, SM_LOW:SM_LOW + GLA_RANK, :].set(gla_alpha_up)
    dtb = jnp.zeros((2, SM_W), F32).at[:, SM_DT:SM_DT + SSD_HEADS].set(ssd_dt_bias)
    a_log = jnp.zeros((2, SM_W), F32).at[:, SM_DT:SM_DT + SSD_HEADS].set(ssd_a_log)
    valid = np.zeros((2, SM_W), np.float32)
    valid[:, SM_DT:SM_DT + SSD_HEADS] = 1.0
    e3 = np.zeros((SM_W, SSD_INNER), np.float32)
    for k in range(3 * SSD_HEADS):
        h = k % SSD_HEADS
        e3[k, h * SSD_P:(h + 1) * SSD_P] = 1.0
    return wup.astype(BF16), jnp.asarray(e3, BF16), dtb, a_log, jnp.asarray(valid)


def _to_col_major(h, rows):
    b, length, dm = h.shape
    return h.reshape(b, rows, GRID_W, dm).transpose(0, 2, 1, 3).reshape(b, length, dm)


def _from_col_major(h, rows):
    b, length, dm = h.shape
    return h.reshape(b, GRID_W, rows, dm).transpose(0, 2, 1, 3).reshape(b, length, dm)


def _neg_exp_kernel(a_ref, v_ref, o_ref):
    o_ref[...] = -jnp.exp(a_ref[...]) * v_ref[...]


def kernel(x, c, ctx, c_ctx, ada_w, ada_b, pre_g, post_g, w_in, conv_a_w, conv_a_b, lru_wr, lru_br,
           lru_wi, lru_bi, lru_lam, gla_alpha_up, gla_alpha_b, gla_norm_g, conv_c_w, conv_c_b,
           ssd_a_log, ssd_dt_bias, ssd_d, ssd_norm_g, w_pa, w_pb, w_pc, w_out):
    b, seq, d = x.shape
    lc = ctx.shape[1]
    depth = w_in.shape[0]
    rows = seq // GRID_W
    assert d == D_MODEL and lc % SCAN_TB == 0 and seq % SCAN_TB == 0 and b % 8 == 0

    pad_rows = (-(b + 1)) % 8
    cc = jnp.concatenate([c, c_ctx[None, :], jnp.zeros((pad_rows, d), F32)], axis=0)
    mods = _adaln(cc, ada_w, ada_b)

    w_in_p = _reorder_in_weights(w_in.astype(BF16))
    x_lat, x_ctx = x, ctx
    for l in range(depth):
        col_major = l % 2 == 1
        line = rows if col_major else GRID_W
        mod_lat = mods[l, :b].reshape(b, 1, 3 * d)
        mod_ctx = mods[l, b:b + 1]
        lat = _to_col_major(x_lat, rows) if col_major else x_lat

        cw, cb = _conv_tables(conv_a_w[l], conv_a_b[l], conv_c_w[l], conv_c_b[l])
        u = _in_proj(x_ctx, lat, mod_lat, mod_ctx, pre_g[l].reshape(1, d), w_in_p[l],
                     cw, cb, line=line)

        ya_f, ya_b = _lru(u, lru_wr[l].astype(BF16), lru_wi[l].astype(BF16), lru_br[l], lru_bi[l],
                          lru_lam[l], lc=lc)

        wup, e3, dtb, a_log, valid = _scan_tables(gla_alpha_up[l], ssd_dt_bias[l], ssd_a_log[l])
        na = pl.pallas_call(_neg_exp_kernel, out_shape=jax.ShapeDtypeStruct(a_log.shape, F32),
                            name="ssd_neg_a")(a_log, valid)
        sd_x = jnp.repeat(ssd_d[l], SSD_P).reshape(1, SSD_INNER)
        yb_f, yc_f, yb_b, yc_b = _scan(u, wup, gla_alpha_b[l], e3, dtb, na, sd_x, lc=lc)

        x_ctx, lat = _finish(ya_f, ya_b, yb_f, yb_b, yc_f, yc_b, u, x_ctx, lat, mod_lat, mod_ctx,
                             jnp.tile(gla_norm_g[l], GLA_HEADS).reshape(1, GLA_DV),
                             ssd_norm_g[l].reshape(1, SSD_INNER),
                             post_g[l].reshape(1, d),
                             w_pa[l].astype(BF16), w_pb[l].astype(BF16), w_pc[l].astype(BF16),
                             w_out[l].astype(BF16))
        x_lat = _from_col_major(lat, rows) if col_major else lat
    return x_lat
```

```python
import functools

import jax
import jax.numpy as jnp
import numpy as np
from jax import lax
from jax.experimental import pallas as pl
from jax.experimental.pallas import tpu as pltpu

F32 = jnp.float32
BF16 = jnp.bfloat16

LANES = 128
SUBLANES = 8

D_MODEL = 1024
GRID_W = 64
CONV_W = 4
EPS = 1e-6
LRU_BW = 128
LRU_C = 8.0
LOG2E = 1.4426950408889634
GLA_HEADS = 4
GLA_DK = 512
GLA_DV = 1024
GLA_DKH = 128
GLA_DVH = 256
GLA_RANK = 16
GLA_TAU = 16.0
SSD_INNER = 2048
SSD_P = 64
SSD_HEADS = 32
SSD_N = 128
SSD_G = 4
SSD_GW = SSD_INNER // SSD_G
SSD_XBC = SSD_INNER + 2 * SSD_G * SSD_N
CHUNK = 64

TN = 512
C_XBC, C_MRG, C_Z, C_XA, C_V, C_LG, C_GG, C_Q, C_K, C_SM = (
    0, 3072, 6144, 8192, 9216, 10240, 11264, 12288, 12800, 13312)
N_TOT = C_SM + TN

VMEM_LIMIT = 56 * 1024 * 1024


def _sigmoid(x):
    return 0.5 * jnp.tanh(0.5 * x) + 0.5


def _silu(x):
    return x * _sigmoid(x)


def _softplus(x):
    return jnp.maximum(x, 0.0) + jnp.log1p(jnp.exp(-jnp.abs(x)))


def _dot(a, b):
    return jnp.dot(a, b, preferred_element_type=F32)


def _dot_nt(a, b):
    return lax.dot_general(a, b, (((1,), (1,)), ((), ())), preferred_element_type=F32)


def _dot_tn(a, b):
    return lax.dot_general(a, b, (((0,), (0,)), ((), ())), preferred_element_type=F32)


def _adaln_kernel(c_ref, w_ref, b_ref, o_ref):
    c = c_ref[...]
    o_ref[0] = _dot(_silu(c).astype(BF16), w_ref[0]) + b_ref[0]


def _adaln(cc, ada_w, ada_b):
    depth, d, n3 = ada_w.shape
    rows = cc.shape[0]
    tn = 1024
    return pl.pallas_call(
        _adaln_kernel,
        grid=(depth, n3 // tn),
        in_specs=[
            pl.BlockSpec((rows, d), lambda l, n: (0, 0)),
            pl.BlockSpec((1, d, tn), lambda l, n: (l, 0, n)),
            pl.BlockSpec((1, 1, tn), lambda l, n: (l, 0, n)),
        ],
        out_specs=pl.BlockSpec((1, rows, tn), lambda l, n: (l, 0, n)),
        out_shape=jax.ShapeDtypeStruct((depth, rows, n3), F32),
        name="adaln",
    )(cc, ada_w.astype(BF16), ada_b.reshape(depth, 1, n3))


CONV_PAD = SUBLANES
CONV_SLABS = TN // LANES
CONV_PARTS = 4

def _in_kernel(xc_ref, xl_ref, ml_ref, mc_ref, pg_ref, w_ref, cw_ref, cb_ref, u_ref, h_s, conv_s, *, line):
    j = pl.program_id(1)
    lc = xc_ref.shape[1]
    tt = lc + xl_ref.shape[1]
    d = xc_ref.shape[2]
    nlines = (tt - lc) // line
    pitch = line + CONV_PAD
    lat0 = CONV_PAD + lc

    @pl.when(j == 0)
    def _():
        def norm_mod(x, mod):
            y = x * lax.rsqrt(jnp.mean(x * x, axis=-1, keepdims=True) + EPS) * pg_ref[...]
            return (y * (1.0 + mod[:, d:2 * d]) + mod[:, 0:d]).astype(BF16)
        h_s[0:lc, :] = norm_mod(xc_ref[0], mc_ref[...])
        h_s[lc:tt, :] = norm_mod(xl_ref[0], ml_ref[0])
        zeros = jnp.zeros((CONV_SLABS, CONV_PAD, LANES), F32)
        conv_s[:, 0:CONV_PAD, :] = zeros
        for i in range(nlines + 1):
            r = lat0 + i * pitch
            conv_s[:, r:r + CONV_PAD, :] = zeros

    def conv(act):
        def taps(n, r, rows):
            ls = slice(n * LANES, (n + 1) * LANES)
            cw = cw_ref[:, ls]
            out = cb_ref[:, ls] + cw[1:2, :] * conv_s[n, r:r + rows, :]
            out = out + cw[0:1, :] * conv_s[n, r - 1:r - 1 + rows, :]
            out = out + cw[2:3, :] * conv_s[n, r + 1:r + 1 + rows, :]
            out = out + cw[3:4, :] * conv_s[n, r + 2:r + 2 + rows, :]
            return act(out)

        lines = [(0, CONV_PAD, lc)] + [(lc + i * line, lat0 + i * pitch + CONV_PAD, line) for i in range(nlines)]
        bounds = [0] + [lc + ((p * nlines) // CONV_PARTS) * line for p in range(1, CONV_PARTS)] + [tt]
        for p in range(CONV_PARTS):
            t0, t1 = bounds[p], bounds[p + 1]
            up = _dot(h_s[t0:t1, :], w_ref[...])
            part = [ln for ln in lines if t0 <= ln[0] < t1]
            for n in range(CONV_SLABS):
                for t, r, rows in part:
                    conv_s[n, r:r + rows, :] = up[t - t0:t - t0 + rows, n * LANES:(n + 1) * LANES]
            for t, r, rows in part:
                for n in range(CONV_SLABS):
                    u_ref[0, 0, t:t + rows, n * LANES:(n + 1) * LANES] = taps(n, r, rows)

    is_xbc = j < SSD_XBC // TN
    is_xa = (j >= C_XA // TN) & (j < (C_XA + D_MODEL) // TN)

    @pl.when(is_xbc)
    def _():
        conv(_silu)

    @pl.when(is_xa)
    def _():
        conv(lambda v: v)

    @pl.when(jnp.logical_not(is_xbc | is_xa))
    def _():
        u_ref[0, 0] = _dot(h_s[...], w_ref[...])


def _in_proj(x_ctx, x_lat, mod_lat, mod_ctx, pre_g, w_in_p, cw, cb, *, line):
    b, lc, d = x_ctx.shape
    seq = x_lat.shape[1]
    tt = lc + seq
    assert line % SUBLANES == 0 and seq % line == 0 and lc % SUBLANES == 0
    conv_rows = CONV_PAD + lc + (seq // line) * (line + CONV_PAD) + CONV_PAD
    return pl.pallas_call(
        functools.partial(_in_kernel, line=line),
        grid=(b, N_TOT // TN),
        in_specs=[
            pl.BlockSpec((1, lc, d), lambda i, j: (i, 0, 0)),
            pl.BlockSpec((1, seq, d), lambda i, j: (i, 0, 0)),
            pl.BlockSpec((1, 1, 3 * d), lambda i, j: (i, 0, 0)),
            pl.BlockSpec((1, 3 * d), lambda i, j: (0, 0)),
            pl.BlockSpec((1, d), lambda i, j: (0, 0)),
            pl.BlockSpec((d, TN), lambda i, j: (0, j)),
            pl.BlockSpec((CONV_W, TN), lambda i, j: (0, j)),
            pl.BlockSpec((1, TN), lambda i, j: (0, j)),
        ],
        out_specs=pl.BlockSpec((1, 1, tt, TN), lambda i, j: (i, j, 0, 0)),
        out_shape=jax.ShapeDtypeStruct((b, N_TOT // TN, tt, TN), F32),
        scratch_shapes=[pltpu.VMEM((tt, d), BF16), pltpu.VMEM((CONV_SLABS, conv_rows, LANES), F32)],
        compiler_params=pltpu.CompilerParams(
            dimension_semantics=("arbitrary", "arbitrary"), vmem_limit_bytes=VMEM_LIMIT),
        name="in_proj",
    )(x_ctx, x_lat, mod_lat, mod_ctx, pre_g, w_in_p, cw, cb)


LRU_TC = 64
LRU_WT = 512
LRU_PITCH = LRU_TC + 4


def _lru_kernel(xf_ref, xb_ref, wr_ref, wi_ref, br_ref, bi_ref, lam_ref, yf_ref, yb_ref,
                a_s, b_s, o_s, h_s):
    i = pl.program_id(1)
    nb, _, tc, wt = xf_ref.shape

    @pl.when(i == 0)
    def _():
        h_s[...] = jnp.zeros_like(h_s)

    def gates(bi, carry):
        r0 = bi * LRU_PITCH
        for dr, x_ref in enumerate((xf_ref, xb_ref)):
            for n in range(wt // LRU_BW):
                sl = slice(n * LRU_BW, (n + 1) * LRU_BW)
                xs = x_ref[bi, 0, :, sl]
                xs16 = xs.astype(BF16)
                hc = (0.5 * LRU_C) * _softplus(-lam_ref[dr:dr + 1, sl])
                nla = hc * jnp.tanh(0.5 * (_dot(xs16, wr_ref[dr, n]) + br_ref[dr:dr + 1, sl])) + hc
                g = _sigmoid(_dot(xs16, wi_ref[dr, n]) + bi_ref[dr:dr + 1, sl])
                a = jnp.exp2(nla * (-LOG2E))
                m2 = jnp.tanh(nla) * (a * a + 1.0)
                root = jnp.where(m2 > 0.0, m2 * lax.rsqrt(m2), 0.0)
                a_s[dr, n, pl.ds(r0, tc), :] = a
                b_s[dr, n, pl.ds(r0, tc), :] = root * (g * xs)
        return carry
    lax.fori_loop(0, nb, gates, 0, unroll=True)

    nl = wt // LRU_BW
    hf = [h_s[0, n] for n in range(nl)]
    hb = [h_s[1, n] for n in range(nl)]
    for t in range(tc):
        rows_f = pl.ds(t, nb, stride=LRU_PITCH)
        rows_b = pl.ds(tc - 1 - t, nb, stride=LRU_PITCH)
        for n in range(nl):
            hf[n] = a_s[0, n, rows_f, :] * hf[n] + b_s[0, n, rows_f, :]
            hb[n] = a_s[1, n, rows_b, :] * hb[n] + b_s[1, n, rows_b, :]
            o_s[0, n, rows_f, :] = hf[n]
            o_s[1, n, rows_b, :] = hb[n]
    for n in range(nl):
        h_s[0, n] = hf[n]
        h_s[1, n] = hb[n]

    def emit(bi, carry):
        r0 = bi * LRU_PITCH
        for n in range(nl):
            sl = slice(n * LRU_BW, (n + 1) * LRU_BW)
            yf_ref[bi, :, sl] = o_s[0, n, pl.ds(r0, tc), :]
            yb_ref[bi, :, sl] = o_s[1, n, pl.ds(r0, tc), :]
        return carry
    lax.fori_loop(0, nb, emit, 0)


def _bwd_block(i, nctx, ntot):
    return jnp.where(i < nctx, nctx - 1 - i, ntot - 1 + nctx - i)


def _lru(u, wr, wi, br, bi, lam, *, lc):
    b, _, tt, _ = u.shape
    w = D_MODEL
    nt = tt // LRU_TC
    nctx = lc // LRU_TC
    assert LRU_WT == TN
    c0 = C_XA // TN
    nblk = LRU_WT // LRU_BW
    y_spec_f = pl.BlockSpec((b, LRU_TC, LRU_WT), lambda j, i: (0, i, j))
    y_spec_b = pl.BlockSpec((b, LRU_TC, LRU_WT), lambda j, i: (0, _bwd_block(i, nctx, nt), j))
    return pl.pallas_call(
        _lru_kernel,
        grid=(w // LRU_WT, nt),
        in_specs=[
            pl.BlockSpec((b, 1, LRU_TC, LRU_WT), lambda j, i: (0, c0 + j, i, 0)),
            pl.BlockSpec((b, 1, LRU_TC, LRU_WT), lambda j, i: (0, c0 + j, _bwd_block(i, nctx, nt), 0)),
            pl.BlockSpec((2, nblk, LRU_BW, LRU_BW), lambda j, i: (0, j, 0, 0)),
            pl.BlockSpec((2, nblk, LRU_BW, LRU_BW), lambda j, i: (0, j, 0, 0)),
            pl.BlockSpec((2, LRU_WT), lambda j, i: (0, j)),
            pl.BlockSpec((2, LRU_WT), lambda j, i: (0, j)),
            pl.BlockSpec((2, LRU_WT), lambda j, i: (0, j)),
        ],
        out_specs=[y_spec_f, y_spec_b],
        out_shape=[jax.ShapeDtypeStruct((b, tt, w), F32)] * 2,
        scratch_shapes=[
            pltpu.VMEM((2, nblk, b * LRU_PITCH, LRU_BW), F32),
            pltpu.VMEM((2, nblk, b * LRU_PITCH, LRU_BW), F32),
            pltpu.VMEM((2, nblk, b * LRU_PITCH, LRU_BW), F32),
            pltpu.VMEM((2, nblk, b, LRU_BW), F32),
        ],
        compiler_params=pltpu.CompilerParams(
            dimension_semantics=("arbitrary", "arbitrary"), vmem_limit_bytes=VMEM_LIMIT),
        name="lru_scan",
    )(u, u, wr, wi, br, bi, lam)


SCAN_TB = 256
SM_W = LANES
SM_DT = 0
SM_LOW = SSD_HEADS
BD_HEADS = 4
BD_W = BD_HEADS * SSD_P
LOG_FLOOR = -1e30


T_SSD_B = SSD_INNER // TN
T_SSD_C = T_SSD_B + 1


def _seg_spec(rows, c0, width, blk):
    nt = width // TN
    assert width % TN == 0 and (c0 // TN) % nt == 0
    return pl.BlockSpec((1, nt, rows, TN), lambda bi, i: (bi, c0 // width, blk(i), 0))


def _wide(ref, rs, ntiles=None):
    nt = ref.shape[1] if ntiles is None else ntiles
    return jnp.concatenate([ref[0, t, rs, :] for t in range(nt)], axis=1)


def _split3(x):
    hi = x.astype(BF16).astype(F32)
    r = x - hi
    mid = r.astype(BF16).astype(F32)
    return hi, mid, r - mid


def _cumsum_rows(tri16, x):
    n = x.shape[1]
    p = _dot(tri16, jnp.concatenate([s.astype(BF16) for s in _split3(x)], axis=1))
    return (p[:, 0:n] + p[:, n:2 * n]) + p[:, 2 * n:3 * n]


def _pack3(x, lane):
    hi, mid, lo = _split3(x)
    h = SSD_HEADS
    return jnp.where(lane < h, hi,
                     jnp.where(lane < 2 * h, pltpu.roll(mid, h, 1),
                               jnp.where(lane < 3 * h, pltpu.roll(lo, 2 * h, 1), 0.0)))


def _scan_kernel(qf_ref, kf_ref, vf_ref, xf_ref, sf_ref, qb_ref, kb_ref, vb_ref, xb_ref, sb_ref,
                 wup_ref, gb_ref, e3_ref, dtb_ref, na_ref, sd_ref,
                 ybf_ref, ycf_ref, ybb_ref, ycb_ref, gla_s, ssd_s):
    i = pl.program_id(1)

    @pl.when(i == 0)
    def _():
        gla_s[...] = jnp.zeros_like(gla_s)
        ssd_s[...] = jnp.zeros_like(ssd_s)

    tb = qf_ref.shape[2]
    nch = tb // CHUNK
    row = lax.broadcasted_iota(jnp.int32, (CHUNK, CHUNK), 0)
    col = lax.broadcasted_iota(jnp.int32, (CHUNK, CHUNK), 1)
    row_x = lax.broadcasted_iota(jnp.int32, (CHUNK, SSD_INNER), 0)
    col_x = lax.broadcasted_iota(jnp.int32, (CHUNK, SSD_INNER), 1) & (CHUNK - 1)
    r2 = lax.broadcasted_iota(jnp.int32, (BD_W, BD_W), 0)
    c2 = lax.broadcasted_iota(jnp.int32, (BD_W, BD_W), 1)
    blk_mask = (r2 // CHUNK) == (c2 // SSD_P)
    lane = lax.broadcasted_iota(jnp.int32, (4 * CHUNK, SM_W), 1)
    prow = lax.broadcasted_iota(jnp.int32, (3 * CHUNK, SM_W), 0)
    plane = lax.broadcasted_iota(jnp.int32, (3 * CHUNK, SM_W), 1)
    ones_part = jnp.where((prow < CHUNK) & (plane < 3 * SSD_HEADS), 1.0, 0.0).astype(BF16)
    heads = range(GLA_HEADS)
    groups = range(SSD_G)

    def ksl(h):
        return slice(h * GLA_DKH, (h + 1) * GLA_DKH)

    def vsl(h):
        return slice(h * GLA_DVH, (h + 1) * GLA_DVH)

    def gsl(g):
        return slice(g * SSD_GW, (g + 1) * SSD_GW)

    dirs = (
        (0, qf_ref, kf_ref, vf_ref, xf_ref, sf_ref, ybf_ref, ycf_ref),
        (1, qb_ref, kb_ref, vb_ref, xb_ref, sb_ref, ybb_ref, ycb_ref),
    )
    def prepare(step):
        insts = []
        for dr, q_ref, k_ref, v_ref, x_ref, s_ref, yb_ref, yc_ref in dirs:
            fwd = dr == 0
            c = step if fwd else nch - 1 - step
            causal = (row >= col) if fwd else (row <= col)
            insts.append(dict(
                dr=dr, rs=slice(c * CHUNK, (c + 1) * CHUNK), last=CHUNK - 1 if fwd else 0,
                causal=causal, causal_x=(row_x >= col_x) if fwd else (row_x <= col_x),
                tri16=jnp.where(causal, 1.0, 0.0).astype(BF16),
                q_ref=q_ref, k_ref=k_ref, v_ref=v_ref, x_ref=x_ref, s_ref=s_ref, yb_ref=yb_ref, yc_ref=yc_ref))

        for it in insts:
            dr, rs, x_ref = it["dr"], it["rs"], it["x_ref"]
            it["sm"] = it["s_ref"][0, 0, rs, :]
            it["z"] = _dot(it["sm"].astype(BF16), wup_ref[dr])
            it["bm16"] = [x_ref[0, T_SSD_B, rs, g * SSD_N:(g + 1) * SSD_N].astype(BF16) for g in groups]
            it["cm16"] = [x_ref[0, T_SSD_C, rs, g * SSD_N:(g + 1) * SSD_N].astype(BF16) for g in groups]
            it["cb"] = [_dot_nt(it["cm16"][g], it["bm16"][g]) for g in groups]

        for it in insts:
            dr, rs, last = it["dr"], it["rs"], it["last"]
            logg = -_softplus(-(it["z"] + gb_ref[dr:dr + 1, :])) * (1.0 / GLA_TAU)
            bc = _cumsum_rows(it["tri16"], logg)
            btot = bc[last:last + 1, :]
            k = it["k_ref"][0, 0, rs, :]
            it["qe16"] = ((it["q_ref"][0, 0, rs, :] * (GLA_DKH ** -0.5)) * jnp.exp(bc)).astype(BF16)
            it["ke16"] = (k * jnp.exp(-bc)).astype(BF16)
            it["kd16"] = (k * jnp.exp(btot - bc)).astype(BF16)
            it["etot"] = jnp.exp(btot)
            it["v16"] = _wide(it["v_ref"], rs).astype(BF16)

            dt = _softplus(it["sm"] + dtb_ref[dr:dr + 1, :])
            cum = _cumsum_rows(it["tri16"], dt * na_ref[dr:dr + 1, :])
            clast = cum[last:last + 1, :]
            wrow = jnp.maximum(jnp.log(dt), LOG_FLOOR) - cum
            packed = _pack3(jnp.concatenate([cum, jnp.exp(cum), dt * jnp.exp(clast - cum), wrow], axis=0), lane)
            pt = packed[3 * CHUNK:4 * CHUNK].T
            pt2 = jnp.concatenate([pt, pt], axis=1).astype(BF16)
            rm = jnp.concatenate([pt2] * (SSD_INNER // LANES), axis=1) * e3_ref[...]
            it["a16"] = jnp.concatenate([packed[0:3 * CHUNK].astype(BF16), ones_part], axis=1)
            it["rhs"] = jnp.concatenate([e3_ref[...], rm], axis=0)
        return insts

    def advance(insts):
        for it in insts:
            it["big"] = _dot(it["a16"], it["rhs"])
            it["att"] = [_dot_nt(it["qe16"][:, ksl(h)], it["ke16"][:, ksl(h)]) for h in heads]

        for it in insts:
            big = it["big"]
            lmat = jnp.exp(jnp.where(it["causal_x"], big[0:CHUNK], -jnp.inf))
            m16 = []
            for g in groups:
                cb2 = jnp.concatenate([it["cb"][g], it["cb"][g]], axis=1)
                m16.append(jnp.concatenate(
                    [cb2 * lmat[:, g * SSD_GW + q * LANES:g * SSD_GW + (q + 1) * LANES] for q in range(SSD_GW // LANES)],
                    axis=1).astype(BF16))
            it["m16"] = m16
            it["ecum_x"] = big[CHUNK:2 * CHUNK]
            xs = _wide(it["x_ref"], it["rs"], SSD_INNER // TN)
            it["xs16"] = xs.astype(BF16)
            it["xdec16"] = (xs * big[2 * CHUNK:3 * CHUNK]).astype(BF16)
            it["att16"] = [jnp.where(it["causal"], a, 0.0).astype(BF16) for a in it["att"]]

        for it in insts:
            dr = it["dr"]
            it["o_in"] = [_dot(it["att16"][h], it["v16"][:, vsl(h)]) for h in heads]
            it["o_st"] = [_dot_nt(it["qe16"][:, ksl(h)], gla_s[dr, h].astype(BF16)) for h in heads]
            it["u_gla"] = [_dot_tn(it["v16"][:, vsl(h)], it["kd16"][:, ksl(h)]) for h in heads]
            y_in = []
            for g in groups:
                parts = []
                for pr in range(SSD_GW // BD_W):
                    ls = slice(g * SSD_GW + pr * BD_W, g * SSD_GW + (pr + 1) * BD_W)
                    xh = it["xs16"][:, ls]
                    bd = jnp.where(blk_mask, jnp.concatenate([xh] * BD_HEADS, axis=0), jnp.zeros((), BF16))
                    parts.append(_dot(it["m16"][g][:, pr * BD_W:(pr + 1) * BD_W], bd))
                y_in.append(jnp.concatenate(parts, axis=1))
            it["y_in"] = y_in
            it["y_st"] = [_dot(it["cm16"][g], ssd_s[dr, g].astype(BF16)) for g in groups]
            it["u_ssd"] = [_dot_tn(it["bm16"][g], it["xdec16"][:, gsl(g)]) for g in groups]

        for it in insts:
            dr, rs, last = it["dr"], it["rs"], it["last"]
            for h in heads:
                it["yb_ref"][0, rs, vsl(h)] = it["o_in"][h] + it["o_st"][h]
                gla_s[dr, h] = gla_s[dr, h] * it["etot"][:, ksl(h)] + it["u_gla"][h]
            for g in groups:
                ecum = it["ecum_x"][:, gsl(g)]
                yc = it["y_in"][g] + it["y_st"][g] * ecum
                if dr == 0:
                    yc = yc + sd_ref[:, gsl(g)] * it["x_ref"][0, g, rs, :]
                it["yc_ref"][0, rs, gsl(g)] = yc
                ssd_s[dr, g] = ssd_s[dr, g] * ecum[last:last + 1, :] + it["u_ssd"][g]

    ready = prepare(0)
    for step in range(nch):
        insts = ready
        if step + 1 < nch:
            ready = prepare(step + 1)
        advance(insts)


def _scan(u, wup, gb, e3, dtb, na, sd_x, *, lc):
    b, _, tt, _ = u.shape
    tb = SCAN_TB
    nt = tt // tb
    nctx = lc // tb

    def fblk(i):
        return i

    def bblk(i):
        return _bwd_block(i, nctx, nt)

    def fmap(cb):
        return lambda bi, i: (bi, i, cb)

    def bmap(cb):
        return lambda bi, i: (bi, _bwd_block(i, nctx, nt), cb)

    def tok_specs(blk, dr):
        return [
            _seg_spec(tb, C_Q, GLA_DK, blk), _seg_spec(tb, C_K, GLA_DK, blk), _seg_spec(tb, C_V, GLA_DV, blk),
            _seg_spec(tb, C_XBC, SSD_XBC, blk),
            pl.BlockSpec((1, 1, tb, SM_W), lambda bi, i: (bi, C_SM // TN, blk(i), dr)),
        ]

    def const_spec(a):
        nd = a.ndim
        return pl.BlockSpec(a.shape, lambda bi, i: (0,) * nd)

    assert SSD_GW == TN
    consts = (wup, gb, e3, dtb, na, sd_x)
    return pl.pallas_call(
        _scan_kernel,
        grid=(b, nt),
        in_specs=tok_specs(fblk, 0) + tok_specs(bblk, 1) + [const_spec(a) for a in consts],
        out_specs=[
            pl.BlockSpec((1, tb, GLA_DV), fmap(0)),
            pl.BlockSpec((1, tb, SSD_INNER), fmap(0)),
            pl.BlockSpec((1, tb, GLA_DV), bmap(0)),
            pl.BlockSpec((1, tb, SSD_INNER), bmap(0)),
        ],
        out_shape=[
            jax.ShapeDtypeStruct((b, tt, GLA_DV), F32),
            jax.ShapeDtypeStruct((b, tt, SSD_INNER), F32),
            jax.ShapeDtypeStruct((b, tt, GLA_DV), F32),
            jax.ShapeDtypeStruct((b, tt, SSD_INNER), F32),
        ],
        scratch_shapes=[
            pltpu.VMEM((2, GLA_HEADS, GLA_DVH, GLA_DKH), F32),
            pltpu.VMEM((2, SSD_G, SSD_N, SSD_GW), F32),
        ],
        compiler_params=pltpu.CompilerParams(
            dimension_semantics=("arbitrary", "arbitrary"), vmem_limit_bytes=VMEM_LIMIT),
        name="gla_ssd_scan",
    )(u, u, u, u, u, u, u, u, u, u, *consts)


FIN_TM = 256


def _group_rms(x, width):
    parts = []
    for s in range(x.shape[1] // width):
        xs = x[:, s * width:(s + 1) * width]
        parts.append(xs * lax.rsqrt(jnp.mean(xs * xs, axis=-1, keepdims=True) + EPS))
    return jnp.concatenate(parts, axis=1)


def _fin_kernel(yaf_ref, yab_ref, ybf_ref, ybb_ref, ycf_ref, ycb_ref, lg_ref, gg_ref, z_ref, mg_ref,
                xc_ref, xl_ref, ml_ref, mc_ref, gng_ref, sng_ref, pog_ref,
                wpa_ref, wpb_ref, wpc_ref, wo_ref, oc_ref, ol_ref, *, nctx):
    i = pl.program_id(1)
    d = D_MODEL
    rows = slice(None)
    ya = yaf_ref[0] + yab_ref[0]
    pa = _dot((ya * _silu(_wide(lg_ref, rows))).astype(BF16), wpa_ref[...])
    yb = ybf_ref[0] + ybb_ref[0]
    ob = (_group_rms(yb, GLA_DVH) * gng_ref[...]) * _silu(_wide(gg_ref, rows))
    pb = _dot(ob.astype(BF16), wpb_ref[...])
    yc = ycf_ref[0] + ycb_ref[0]
    oc = _group_rms(yc * _silu(_wide(z_ref, rows)), SSD_GW) * sng_ref[...]
    pc = _dot(oc.astype(BF16), wpc_ref[...])
    gates = _sigmoid(_wide(mg_ref, rows))
    merged = gates[:, 0:d] * pa + gates[:, d:2 * d] * pb + gates[:, 2 * d:3 * d] * pc
    out = _dot(merged.astype(BF16), wo_ref[...])
    normed = _group_rms(out, d) * pog_ref[...]

    @pl.when(i < nctx)
    def _():
        oc_ref[0] = xc_ref[0] + mc_ref[:, 2 * d:3 * d] * normed

    @pl.when(i >= nctx)
    def _():
        ol_ref[0] = xl_ref[0] + ml_ref[0][:, 2 * d:3 * d] * normed


def _finish(ya_f, ya_b, yb_f, yb_b, yc_f, yc_b, u, x_ctx, x_lat, mod_lat, mod_ctx,
            gng, sng, post_g, w_pa, w_pb, w_pc, w_out):
    b, lc, d = x_ctx.shape
    seq = x_lat.shape[1]
    tm = FIN_TM
    nctx = lc // tm
    ctx_spec = pl.BlockSpec((1, tm, d), lambda bi, i: (bi, jnp.minimum(i, nctx - 1), 0))
    lat_spec = pl.BlockSpec((1, tm, d), lambda bi, i: (bi, jnp.maximum(i - nctx, 0), 0))

    def tok(width):
        return pl.BlockSpec((1, tm, width), lambda bi, i: (bi, i, 0))

    def blk(i):
        return i

    def const_spec(a):
        nd = a.ndim
        return pl.BlockSpec(a.shape, lambda bi, i: (0,) * nd, pipeline_mode=pl.Buffered(1))

    consts = (gng, sng, post_g, w_pa, w_pb, w_pc, w_out)
    return pl.pallas_call(
        functools.partial(_fin_kernel, nctx=nctx),
        grid=(b, (lc + seq) // tm),
        in_specs=[
            tok(d), tok(d), tok(GLA_DV), tok(GLA_DV), tok(SSD_INNER), tok(SSD_INNER),
            _seg_spec(tm, C_LG, d, blk), _seg_spec(tm, C_GG, GLA_DV, blk), _seg_spec(tm, C_Z, SSD_INNER, blk),
            _seg_spec(tm, C_MRG, 3 * d, blk), ctx_spec, lat_spec,
            pl.BlockSpec((1, 1, 3 * d), lambda bi, i: (bi, 0, 0)),
            pl.BlockSpec((1, 3 * d), lambda bi, i: (0, 0)),
        ] + [const_spec(a) for a in consts],
        out_specs=[ctx_spec, lat_spec],
        out_shape=[jax.ShapeDtypeStruct((b, lc, d), F32), jax.ShapeDtypeStruct((b, seq, d), F32)],
        compiler_params=pltpu.CompilerParams(
            dimension_semantics=("arbitrary", "arbitrary"), vmem_limit_bytes=VMEM_LIMIT),
        name="finish",
    )(ya_f, ya_b, yb_f, yb_b, yc_f, yc_b, u, u, u, u, x_ctx, x_lat, mod_lat, mod_ctx, *consts)


def _reorder_in_weights(w_in):
    lead = w_in.shape[:-1]
    o = 0
    seg = {}
    for name, wd in (("xa", D_MODEL), ("lg", D_MODEL), ("q", GLA_DK), ("k", GLA_DK), ("v", GLA_DV),
                     ("gg", GLA_DV), ("low", 2 * GLA_RANK), ("z", SSD_INNER), ("xbc", SSD_XBC),
                     ("dt", 2 * SSD_HEADS), ("mrg", 3 * D_MODEL)):
        seg[name] = w_in[..., o:o + wd]
        o += wd
    pad = jnp.zeros(lead + (SM_W - GLA_RANK - SSD_HEADS,), w_in.dtype)
    cols = [seg["xbc"], seg["mrg"], seg["z"], seg["xa"], seg["v"], seg["lg"], seg["gg"],
            seg["q"], seg["k"]]
    for dr in range(2):
        cols += [seg["dt"][..., dr * SSD_HEADS:(dr + 1) * SSD_HEADS],
                 seg["low"][..., dr * GLA_RANK:(dr + 1) * GLA_RANK], pad]
    cols.append(jnp.zeros(lead + (N_TOT - C_SM - 2 * SM_W,), w_in.dtype))
    return jnp.concatenate(cols, axis=-1)


def _conv_tables(conv_a_w, conv_a_b, conv_c_w, conv_c_b):
    cw = jnp.zeros((CONV_W, N_TOT), F32)
    cw = cw.at[:, C_XBC:C_XBC + SSD_XBC].set(conv_c_w).at[:, C_XA:C_XA + D_MODEL].set(conv_a_w)
    cb = jnp.zeros((1, N_TOT), F32)
    cb = cb.at[0, C_XBC:C_XBC + SSD_XBC].set(conv_c_b).at[0, C_XA:C_XA + D_MODEL].set(conv_a_b)
    return cw, cb


def _scan_tables(gla_alpha_up, ssd_dt_bias, ssd_a_log):
    wup = jnp.zeros((2, SM_W, GLA_DK), F32).at[:, SM_LOW:SM_LOW + GLA_RANK, :].set(gla_alpha_up)
    dtb = jnp.zeros((2, SM_W), F32).at[:, SM_DT:SM_DT + SSD_HEADS].set(ssd_dt_bias)
    a_log = jnp.zeros((2, SM_W), F32).at[:, SM_DT:SM_DT + SSD_HEADS].set(ssd_a_log)
    valid = np.zeros((2, SM_W), np.float32)
    valid[:, SM_DT:SM_DT + SSD_HEADS] = 1.0
    e3 = np.zeros((SM_W, SSD_INNER), np.float32)
    for k in range(3 * SSD_HEADS):
        h = k % SSD_HEADS
        e3[k, h * SSD_P:(h + 1) * SSD_P] = 1.0
    return wup.astype(BF16), jnp.asarray(e3, BF16), dtb, a_log, jnp.asarray(valid)


def _to_col_major(h, rows):
    b, length, dm = h.shape
    return h.reshape(b, rows, GRID_W, dm).transpose(0, 2, 1, 3).reshape(b, length, dm)


def _from_col_major(h, rows):
    b, length, dm = h.shape
    return h.reshape(b, GRID_W, rows, dm).transpose(0, 2, 1, 3).reshape(b, length, dm)


def _neg_exp_kernel(a_ref, v_ref, o_ref):
    o_ref[...] = -jnp.exp(a_ref[...]) * v_ref[...]


def kernel(x, c, ctx, c_ctx, ada_w, ada_b, pre_g, post_g, w_in, conv_a_w, conv_a_b, lru_wr, lru_br,
           lru_wi, lru_bi, lru_lam, gla_alpha_up, gla_alpha_b, gla_norm_g, conv_c_w, conv_c_b,
           ssd_a_log, ssd_dt_bias, ssd_d, ssd_norm_g, w_pa, w_pb, w_pc, w_out):
    b, seq, d = x.shape
    lc = ctx.shape[1]
    depth = w_in.shape[0]
    rows = seq // GRID_W
    assert d == D_MODEL and lc % SCAN_TB == 0 and seq % SCAN_TB == 0 and b % 8 == 0

    pad_rows = (-(b + 1)) % 8
    cc = jnp.concatenate([c, c_ctx[None, :], jnp.zeros((pad_rows, d), F32)], axis=0)
    mods = _adaln(cc, ada_w, ada_b)

    w_in_p = _reorder_in_weights(w_in.astype(BF16))
    x_lat, x_ctx = x, ctx
    for l in range(depth):
        col_major = l % 2 == 1
        line = rows if col_major else GRID_W
        mod_lat = mods[l, :b].reshape(b, 1, 3 * d)
        mod_ctx = mods[l, b:b + 1]
        lat = _to_col_major(x_lat, rows) if col_major else x_lat

        cw, cb = _conv_tables(conv_a_w[l], conv_a_b[l], conv_c_w[l], conv_c_b[l])
        u = _in_proj(x_ctx, lat, mod_lat, mod_ctx, pre_g[l].reshape(1, d), w_in_p[l],
                     cw, cb, line=line)

        ya_f, ya_b = _lru(u, lru_wr[l].astype(BF16), lru_wi[l].astype(BF16), lru_br[l], lru_bi[l],
                          lru_lam[l], lc=lc)

        wup, e3, dtb, a_log, valid = _scan_tables(gla_alpha_up[l], ssd_dt_bias[l], ssd_a_log[l])
        na = pl.pallas_call(_neg_exp_kernel, out_shape=jax.ShapeDtypeStruct(a_log.shape, F32),
                            name="ssd_neg_a")(a_log, valid)
        sd_x = jnp.repeat(ssd_d[l], SSD_P).reshape(1, SSD_INNER)
        yb_f, yc_f, yb_b, yc_b = _scan(u, wup, gla_alpha_b[l], e3, dtb, na, sd_x, lc=lc)

        x_ctx, lat = _finish(ya_f, ya_b, yb_f, yb_b, yc_f, yc_b, u, x_ctx, lat, mod_lat, mod_ctx,
                             jnp.tile(gla_norm_g[l], GLA_HEADS).reshape(1, GLA_DV),
                             ssd_norm_g[l].reshape(1, SSD_INNER),
                             post_g[l].reshape(1, d),
                             w_pa[l].astype(BF16), w_pb[l].astype(BF16), w_pc[l].astype(BF16),
                             w_out[l].astype(BF16))
        x_lat = _from_col_major(lat, rows) if col_major else lat
    return x_lat
```
